```python
import math
import jax
import jax.numpy as jnp
from jax import lax
import numpy as np

D_MODEL = 1024
BATCH = 2
SEQ = 8192
DEPTH = 2

N_A_LAYERS = DEPTH // 2
N_B_LAYERS = DEPTH - N_A_LAYERS
EPS = 1e-6
NEG_INF = -1e30

GDN_QK_HEADS = 8
GDN_V_HEADS = 16
GDN_HEAD_DIM = 128
GDN_QK_DIM = GDN_QK_HEADS * GDN_HEAD_DIM
GDN_V_DIM = GDN_V_HEADS * GDN_HEAD_DIM
GDN_CONV_DIM = 2 * GDN_QK_DIM + GDN_V_DIM
GDN_IN_DIM = GDN_CONV_DIM + GDN_V_DIM + 2 * GDN_V_HEADS
GDN_CONV_WIDTH = 4
GDN_CHUNK = 64

SWA_Q_HEADS = 16
SWA_KV_HEADS = 4
SWA_GROUP = SWA_Q_HEADS // SWA_KV_HEADS
SWA_HEAD_DIM = 64
SWA_WINDOW = 128
SWA_BLOCK = 128

REL_BUCKETS = 32
REL_MAX_DISTANCE = 128

D_FF = 2816
FFN_CONV_WIDTH = 3

kernel_name = 'hybrid_gdn_swa_yoco'


def rmsnorm(x, w):
    xf = x.astype(jnp.float32)
    y = xf * lax.rsqrt(jnp.mean(xf * xf, axis=-1, keepdims=True) + EPS)
    return (y * w.astype(jnp.float32)).astype(x.dtype)


def l2norm(x):
    return x * lax.rsqrt(jnp.sum(x * x, axis=-1, keepdims=True) + EPS)


def causal_dwconv(x, w, b=None):
    width = w.shape[0]
    t = x.shape[1]
    xp = jnp.pad(x, ((0, 0), (width - 1, 0), (0, 0)))
    y = sum(xp[:, j:j + t] * w[j] for j in range(width))
    return y if b is None else y + b


def chunked_gated_delta_rule(q, k, v, g, beta):
    bsz, t, h, dk = q.shape
    dv = v.shape[-1]
    c = GDN_CHUNK
    n = t // c
    f32 = jnp.float32
    q = q.astype(f32).reshape(bsz, n, c, h, dk).transpose(0, 1, 3, 2, 4)
    k = k.astype(f32).reshape(bsz, n, c, h, dk).transpose(0, 1, 3, 2, 4)
    v = v.astype(f32).reshape(bsz, n, c, h, dv).transpose(0, 1, 3, 2, 4)
    g = g.astype(f32).reshape(bsz, n, c, h).transpose(0, 1, 3, 2)
    beta = beta.astype(f32).reshape(bsz, n, c, h).transpose(0, 1, 3, 2)

    gc = jnp.cumsum(g, axis=-1)
    tril = jnp.tril(jnp.ones((c, c), dtype=bool))
    strict = jnp.tril(jnp.ones((c, c), dtype=bool), -1)
    diff = gc[..., :, None] - gc[..., None, :]
    decay = jnp.where(tril, jnp.exp(jnp.where(tril, diff, 0.0)), 0.0)

    k_beta = k * beta[..., None]
    v_beta = v * beta[..., None]
    m = jnp.where(strict, jnp.einsum('bnhcd,bnhsd->bnhcs', k_beta, k) * decay, 0.0)
    eye = jnp.broadcast_to(jnp.eye(c, dtype=f32), m.shape)
    t_mat = lax.linalg.triangular_solve(m + eye, eye, left_side=True, lower=True,
                                        unit_diagonal=True)
    u = jnp.einsum('bnhcs,bnhse->bnhce', t_mat, v_beta)
    w = jnp.einsum('bnhcs,bnhsd->bnhcd', t_mat, k_beta * jnp.exp(gc)[..., None])
    a_intra = jnp.einsum('bnhcd,bnhsd->bnhcs', q, k) * decay
    q_dec = q * jnp.exp(gc)[..., None]
    k_dec = k * jnp.exp(gc[..., -1:] - gc)[..., None]
    g_last = jnp.exp(gc[..., -1])

    def step(state, inp):
        w_i, u_i, qd_i, kd_i, a_i, gl_i = inp
        v_new = u_i - jnp.einsum('bhcd,bhde->bhce', w_i, state)
        o_i = (jnp.einsum('bhcd,bhde->bhce', qd_i, state)
               + jnp.einsum('bhcs,bhse->bhce', a_i, v_new))
        state = state * gl_i[..., None, None] + jnp.einsum('bhcd,bhce->bhde', kd_i, v_new)
        return state, o_i

    xs = tuple(jnp.moveaxis(z, 1, 0) for z in (w, u, q_dec, k_dec, a_intra, g_last))
    s0 = jnp.zeros((bsz, h, dk, dv), f32)
    _, o = lax.scan(step, s0, xs)
    return o.transpose(1, 0, 3, 2, 4).reshape(bsz, t, h, dv)


def gated_deltanet(hn, w_in, conv_w, a_log, dt_bias, out_norm_w, w_out):
    bsz, t, _ = hn.shape
    proj = hn @ w_in
    qkv, z, b, a = jnp.split(
        proj, [GDN_CONV_DIM, GDN_CONV_DIM + GDN_V_DIM, GDN_CONV_DIM + GDN_V_DIM + GDN_V_HEADS],
        axis=-1)
    qkv = jax.nn.silu(causal_dwconv(qkv, conv_w))
    q, k, v = jnp.split(qkv, [GDN_QK_DIM, 2 * GDN_QK_DIM], axis=-1)
    q = l2norm(q.reshape(bsz, t, GDN_QK_HEADS, GDN_HEAD_DIM)) * (GDN_HEAD_DIM ** -0.5)
    k = l2norm(k.reshape(bsz, t, GDN_QK_HEADS, GDN_HEAD_DIM))
    v = v.reshape(bsz, t, GDN_V_HEADS, GDN_HEAD_DIM)
    rep = GDN_V_HEADS // GDN_QK_HEADS
    q = jnp.repeat(q, rep, axis=2)
    k = jnp.repeat(k, rep, axis=2)
    beta = jax.nn.sigmoid(b.astype(jnp.float32))
    g = -jnp.exp(a_log.astype(jnp.float32)) * jax.nn.softplus(
        a.astype(jnp.float32) + dt_bias.astype(jnp.float32))
    o = chunked_gated_delta_rule(q, k, v, g, beta)
    z = z.reshape(bsz, t, GDN_V_HEADS, GDN_HEAD_DIM).astype(jnp.float32)
    o = rmsnorm(o, out_norm_w) * jax.nn.silu(z)
    return o.reshape(bsz, t, GDN_V_DIM).astype(hn.dtype) @ w_out


def shared_kv(h, kv_norm_w, w_kv):
    bsz, t, _ = h.shape
    kv = rmsnorm(h, kv_norm_w) @ w_kv
    k, v = jnp.split(kv, 2, axis=-1)
    shape = (bsz, t, SWA_KV_HEADS, SWA_HEAD_DIM)
    return k.reshape(shape), v.reshape(shape)


def t5_bucket(dist):
    n = jnp.maximum(dist, 0)
    max_exact = REL_BUCKETS // 2
    nf = jnp.maximum(n, 1).astype(jnp.float32)
    large = max_exact + (jnp.log(nf / max_exact) / math.log(REL_MAX_DISTANCE / max_exact)
                         * (REL_BUCKETS - max_exact)).astype(jnp.int32)
    large = jnp.minimum(large, REL_BUCKETS - 1)
    return jnp.where(n < max_exact, n, large)


def band_bias_and_mask(rel_table, nb):
    qi = jnp.arange(SWA_BLOCK)[:, None]
    ki = jnp.arange(2 * SWA_BLOCK)[None, :]
    dist = qi + SWA_BLOCK - ki
    bias = rel_table.astype(jnp.float32)[t5_bucket(dist)]
    bias = jnp.transpose(bias, (2, 0, 1)).reshape(
        SWA_KV_HEADS, SWA_GROUP, SWA_BLOCK, 2 * SWA_BLOCK)
    in_window = (dist >= 0) & (dist < SWA_WINDOW)
    key_pos = jnp.arange(nb)[:, None, None] * SWA_BLOCK + ki[None] - SWA_BLOCK
    mask = in_window[None] & (key_pos >= 0)
    return bias, mask


def sliding_window_sink_attention(hn, k, v, w_q, sinks, w_o, bias, mask):
    bsz, t, _ = hn.shape
    nb = t // SWA_BLOCK
    q = (hn @ w_q).reshape(bsz, nb, SWA_BLOCK, SWA_KV_HEADS, SWA_GROUP, SWA_HEAD_DIM)
    q = q * (SWA_HEAD_DIM ** -0.5)

    def band(z):
        prev = jnp.concatenate([jnp.zeros_like(z[:, :SWA_BLOCK]), z[:, :-SWA_BLOCK]], axis=1)
        shape = (bsz, nb, SWA_BLOCK, SWA_KV_HEADS, SWA_HEAD_DIM)
        return jnp.concatenate([prev.reshape(shape), z.reshape(shape)], axis=2)

    kb, vb = band(k), band(v)
    s = jnp.einsum('bnqhgd,bnkhd->bnhgqk', q, kb).astype(jnp.float32) + bias
    s = jnp.where(mask[None, :, None, None], s, NEG_INF)
    sink = sinks.astype(jnp.float32).reshape(SWA_KV_HEADS, SWA_GROUP)[:, :, None]
    m = jnp.maximum(jnp.max(s, axis=-1), sink)
    p = jnp.exp(s - m[..., None])
    denom = jnp.sum(p, axis=-1) + jnp.exp(sink - m)
    probs = (p / denom[..., None]).astype(vb.dtype)
    o = jnp.einsum('bnhgqk,bnkhd->bnqhgd', probs, vb)
    return o.reshape(bsz, t, SWA_Q_HEADS * SWA_HEAD_DIM) @ w_o


def conv_gated_mlp(hn, w_up, conv_w, conv_b, w_down):
    u = causal_dwconv(hn @ w_up, conv_w, conv_b)
    gate, val = jnp.split(u, 2, axis=-1)
    return (jax.nn.silu(gate) * val) @ w_down


def setup_inputs(seed: int = 0) -> dict:
    key = jax.random.key(seed)
    ks = jax.random.split(key, 24)
    f32 = jnp.float32

    def nrm(k, shape, scale):
        return jax.random.normal(k, shape, f32) * scale

    def gain(k, shape):
        return 1.0 + 0.02 * jax.random.normal(k, shape, f32)

    dt = jnp.exp(jax.random.uniform(ks[5], (N_A_LAYERS, GDN_V_HEADS), f32,
                                    minval=math.log(1e-3), maxval=math.log(1e-1)))
    return {
        'x': nrm(ks[0], (BATCH, SEQ, D_MODEL), 1.0),
        'a_norm_w': gain(ks[1], (N_A_LAYERS, D_MODEL)),
        'a_w_in': nrm(ks[2], (N_A_LAYERS, D_MODEL, GDN_IN_DIM), D_MODEL ** -0.5),
        'a_conv_w': nrm(ks[3], (N_A_LAYERS, GDN_CONV_WIDTH, GDN_CONV_DIM), GDN_CONV_WIDTH ** -0.5),
        'a_a_log': jnp.log(jax.random.uniform(ks[4], (N_A_LAYERS, GDN_V_HEADS), f32,
                                              minval=1.0, maxval=16.0)),
        'a_dt_bias': dt + jnp.log(-jnp.expm1(-dt)),
        'a_out_norm_w': gain(ks[6], (N_A_LAYERS, GDN_HEAD_DIM)),
        'a_w_out': nrm(ks[7], (N_A_LAYERS, GDN_V_DIM, D_MODEL), GDN_V_DIM ** -0.5),
        'kv_norm_w': gain(ks[8], (D_MODEL,)),
        'w_kv': nrm(ks[9], (D_MODEL, 2 * SWA_KV_HEADS * SWA_HEAD_DIM), D_MODEL ** -0.5),
        'b_norm_w': gain(ks[10], (N_B_LAYERS, D_MODEL)),
        'b_w_q': nrm(ks[11], (N_B_LAYERS, D_MODEL, SWA_Q_HEADS * SWA_HEAD_DIM), D_MODEL ** -0.5),
        'b_sinks': nrm(ks[12], (N_B_LAYERS, SWA_Q_HEADS), 0.5),
        'b_w_o': nrm(ks[13], (N_B_LAYERS, SWA_Q_HEADS * SWA_HEAD_DIM, D_MODEL),
                     (SWA_Q_HEADS * SWA_HEAD_DIM) ** -0.5),
        'rel_bias_table': nrm(ks[14], (REL_BUCKETS, SWA_Q_HEADS), 0.5),
        'ffn_norm_w': gain(ks[15], (DEPTH, D_MODEL)),
        'ffn_w_up': nrm(ks[16], (DEPTH, D_MODEL, 2 * D_FF), D_MODEL ** -0.5),
        'ffn_conv_w': nrm(ks[17], (DEPTH, FFN_CONV_WIDTH, 2 * D_FF), FFN_CONV_WIDTH ** -0.5),
        'ffn_conv_b': nrm(ks[18], (DEPTH, 2 * D_FF), 0.02),
        'ffn_w_down': nrm(ks[19], (DEPTH, D_FF, D_MODEL), D_FF ** -0.5),
        'final_norm_w': gain(ks[20], (D_MODEL,)),
    }


def reference(x, a_norm_w, a_w_in, a_conv_w, a_a_log, a_dt_bias, a_out_norm_w, a_w_out,
              kv_norm_w, w_kv, b_norm_w, b_w_q, b_sinks, b_w_o, rel_bias_table,
              ffn_norm_w, ffn_w_up, ffn_conv_w, ffn_conv_b, ffn_w_down, final_norm_w):
    nb = x.shape[1] // SWA_BLOCK
    bias, mask = band_bias_and_mask(rel_bias_table, nb)
    h = x
    k_sh = None
    v_sh = None
    for layer in range(DEPTH):
        if layer < N_A_LAYERS:
            i = layer
            h = h + gated_deltanet(rmsnorm(h, a_norm_w[i]), a_w_in[i], a_conv_w[i], a_a_log[i],
                                   a_dt_bias[i], a_out_norm_w[i], a_w_out[i])
        else:
            j = layer - N_A_LAYERS
            if j == 0:
                k_sh, v_sh = shared_kv(h, kv_norm_w, w_kv)
            h = h + sliding_window_sink_attention(rmsnorm(h, b_norm_w[j]), k_sh, v_sh, b_w_q[j],
                                                  b_sinks[j], b_w_o[j], bias, mask)
        h = h + conv_gated_mlp(rmsnorm(h, ffn_norm_w[layer]), ffn_w_up[layer], ffn_conv_w[layer],
                               ffn_conv_b[layer], ffn_w_down[layer])
    return rmsnorm(h, final_norm_w)
```

```python
import functools
import math

import jax
import jax.numpy as jnp
import numpy as np
from jax import lax
from jax.experimental import pallas as pl
from jax.experimental.pallas import tpu as pltpu

F32 = jnp.float32
BF16 = jnp.bfloat16

EPS = 1e-6
NEG_INF = -1e30

D_MODEL = 1024
GDN_QK_HEADS = 8
GDN_V_HEADS = 16
GDN_HEAD_DIM = 128
GDN_QK_DIM = GDN_QK_HEADS * GDN_HEAD_DIM
GDN_V_DIM = GDN_V_HEADS * GDN_HEAD_DIM
GDN_CONV_DIM = 2 * GDN_QK_DIM + GDN_V_DIM
GDN_MAIN_DIM = GDN_CONV_DIM + GDN_V_DIM
GDN_CONV_WIDTH = 4
GDN_CHUNK = 64

SWA_Q_HEADS = 16
SWA_KV_HEADS = 4
SWA_GROUP = SWA_Q_HEADS // SWA_KV_HEADS
SWA_HEAD_DIM = 64
SWA_WINDOW = 128
SWA_BLOCK = 128
REL_BUCKETS = 32
REL_MAX_DISTANCE = 128

D_FF = 2816
FFN_CONV_WIDTH = 3
FFN_CHUNK = 256

SUBLANES = 8
LANES = 128
VMEM_LIMIT_BYTES = 56 * 1024 * 1024

TM_PROJ = 256
TM_GDN = 256
TM_FFN = 256
TM_MM = 512


def _params(*sem):
    return pltpu.CompilerParams(dimension_semantics=sem, vmem_limit_bytes=VMEM_LIMIT_BYTES)


def _const_spec(shape):
    zeros = (0,) * len(shape)
    return pl.BlockSpec(shape, lambda *_: zeros)


def _dot(a, b):
    return jnp.dot(a.astype(BF16), b.astype(BF16), preferred_element_type=F32)


def _dot_nt(a, b):
    return lax.dot_general(a.astype(BF16), b.astype(BF16), (((1,), (1,)), ((), ())),
                           preferred_element_type=F32)


def _dot_tn(a, b):
    return lax.dot_general(a.astype(BF16), b.astype(BF16), (((0,), (0,)), ((), ())),
                           preferred_element_type=F32)


def _dot_f32(a, b):
    return jnp.dot(a, b, preferred_element_type=F32, precision=lax.Precision.HIGHEST)


def _rms_scale(x):
    return x * lax.rsqrt(jnp.mean(x * x, axis=-1, keepdims=True) + EPS)


def _silu(x):
    return x * (1.0 / (1.0 + jnp.exp(-x)))


def _shift_rows(cur, prev8, j):
    if j == 0:
        return cur
    r = pltpu.roll(cur, j, 0)
    pr = pltpu.roll(prev8, j, 0)
    row = lax.broadcasted_iota(jnp.int32, pr.shape, 0)
    top = jnp.where(row < j, pr, r[:SUBLANES])
    return jnp.concatenate([top, r[SUBLANES:]], axis=0)


def _in_proj_kernel(x_ref, nw_ref, wm_ref, wba_ref, wbat_ref, main_ref, ba_ref, bat_ref):
    hn = (_rms_scale(x_ref[...]) * nw_ref[...]).astype(BF16)
    n = main_ref.shape[1]
    step = 512
    for c in range(0, n, step):
        main_ref[:, c:c + step] = jnp.dot(hn, wm_ref[:, c:c + step], preferred_element_type=F32)
    ba_ref[...] = jnp.dot(hn, wba_ref[...], preferred_element_type=F32)
    bat_ref[...] = _dot_nt(wbat_ref[...], hn)


def _in_proj(x2, norm_w, w_main, w_ba, w_bat):
    m, k = x2.shape
    n = w_main.shape[1]
    nb = w_ba.shape[1]
    tm = TM_PROJ
    return pl.pallas_call(
        _in_proj_kernel,
        grid=(m // tm,),
        in_specs=[
            pl.BlockSpec((tm, k), lambda i: (i, 0)),
            _const_spec((1, k)),
            _const_spec((k, n)),
            _const_spec((k, nb)),
            _const_spec((nb, k)),
        ],
        out_specs=[
            pl.BlockSpec((tm, n), lambda i: (i, 0)),
            pl.BlockSpec((tm, nb), lambda i: (i, 0)),
            pl.BlockSpec((nb, tm), lambda i: (0, i)),
        ],
        out_shape=[
            jax.ShapeDtypeStruct((m, n), F32),
            jax.ShapeDtypeStruct((m, nb), F32),
            jax.ShapeDtypeStruct((nb, m), F32),
        ],
        compiler_params=_params("parallel"),
        name="gdn_in_proj",
    )(x2, norm_w, w_main, w_ba, w_bat)


def _tri_inverse(m_strict):
    c = m_strict.shape[0]
    row = lax.broadcasted_iota(jnp.int32, (c, c), 0)
    col = lax.broadcasted_iota(jnp.int32, (c, c), 1)
    eye = (row == col).astype(F32)
    x = eye - m_strict
    p = _dot(m_strict, m_strict)
    levels = int(math.log2(c)) - 1
    for lvl in range(levels):
        if lvl + 1 < levels:
            xp = _dot(jnp.concatenate([x, p], axis=0), p)
            x = x + xp[:c]
            p = xp[c:]
        else:
            x = x + _dot(x, p)
    return x


def _gdn_core_kernel(proj_ref, prev_ref, ba_ref, bat_ref, cw_ref, alog_ref, dtb_ref,
                     alogt_ref, dtbt_ref, onw_ref, out_ref,
                     qkv_scr, s_scr, gc_scr, gct_scr, beta_scr):
    t = pl.program_id(1)
    tm = proj_ref.shape[0]
    c = GDN_CHUNK
    nchunk = tm // c
    hd = GDN_HEAD_DIM

    @pl.when(t == 0)
    def _():
        s_scr[...] = jnp.zeros_like(s_scr)

    keep_prev = (t > 0).astype(F32)
    for cg in range(GDN_CONV_DIM // hd):
        sl = slice(cg * hd, (cg + 1) * hd)
        cur = proj_ref[:, sl]
        prev8 = prev_ref[:, sl] * keep_prev
        y = cur * cw_ref[GDN_CONV_WIDTH - 1:GDN_CONV_WIDTH, sl]
        for j in range(1, GDN_CONV_WIDTH):
            y = y + _shift_rows(cur, prev8, j) * cw_ref[GDN_CONV_WIDTH - 1 - j:GDN_CONV_WIDTH - j, sl]
        y = _silu(y)
        if cg < 2 * GDN_QK_HEADS:
            y = y * lax.rsqrt(jnp.sum(y * y, axis=-1, keepdims=True) + EPS)
            if cg < GDN_QK_HEADS:
                y = y * (hd ** -0.5)
        qkv_scr[:, sl] = y

    nh = GDN_V_HEADS
    b_lin = ba_ref[:, :nh]
    a_lin = ba_ref[:, nh:2 * nh] + dtb_ref[...]
    beta_scr[...] = 1.0 / (1.0 + jnp.exp(-b_lin))
    g = -jnp.exp(alog_ref[...]) * (jnp.maximum(a_lin, 0.0) + jnp.log1p(jnp.exp(-jnp.abs(a_lin))))
    a_lin_t = bat_ref[nh:2 * nh, :] + dtbt_ref[...]
    g_t = -jnp.exp(alogt_ref[...]) * (jnp.maximum(a_lin_t, 0.0)
                                      + jnp.log1p(jnp.exp(-jnp.abs(a_lin_t))))
    row = lax.broadcasted_iota(jnp.int32, (tm, tm), 0)
    col = lax.broadcasted_iota(jnp.int32, (tm, tm), 1)
    same = (row // c) == (col // c)
    lblk = (same & (col <= row)).astype(F32)
    ublk = (same & (row <= col)).astype(F32)
    gc_scr[...] = _dot_f32(lblk, g)
    gct = _dot_f32(g_t, ublk)
    for ci in range(nchunk):
        gct_scr[ci] = gct[:, ci * c:(ci + 1) * c]

    ri = lax.broadcasted_iota(jnp.int32, (c, c), 0)
    cj = lax.broadcasted_iota(jnp.int32, (c, c), 1)
    tril = cj <= ri
    strict = cj < ri
    onw = onw_ref[...]

    def chunk_body(ci, carry):
        r0 = pl.multiple_of(ci * c, c)
        rows = pl.ds(r0, c)
        gc = gc_scr[rows, :]
        gct_c = gct_scr[ci]
        beta = beta_scr[rows, :]
        egc = jnp.exp(gc)
        gl = gc[c - 1:c, :]
        ekd = jnp.exp(gl - gc)
        egl = jnp.exp(gl)
        for j in range(GDN_QK_HEADS):
            q_j = qkv_scr[rows, j * hd:(j + 1) * hd]
            k_j = qkv_scr[rows, GDN_QK_DIM + j * hd:GDN_QK_DIM + (j + 1) * hd]
            gq = _dot_nt(jnp.concatenate([k_j, q_j], axis=0), k_j)
            gram = gq[:c]
            qk = gq[c:]
            for e in range(GDN_V_HEADS // GDN_QK_HEADS):
                h = j * (GDN_V_HEADS // GDN_QK_HEADS) + e
                v_h = qkv_scr[rows, 2 * GDN_QK_DIM + h * hd:2 * GDN_QK_DIM + (h + 1) * hd]
                beta_h = beta[:, h:h + 1]
                diff = gc[:, h:h + 1] - gct_c[h:h + 1, :]
                dec = jnp.where(tril, jnp.exp(jnp.where(tril, diff, 0.0)), 0.0)
                m_h = jnp.where(strict, beta_h * gram * dec, 0.0)
                a_h = qk * dec
                t_h = _tri_inverse(m_h)
                kb = k_j * beta_h
                uw = _dot(t_h, jnp.concatenate([v_h * beta_h, kb * egc[:, h:h + 1]], axis=1))
                u = uw[:, :hd]
                w = uw[:, hd:]
                qd = q_j * egc[:, h:h + 1]
                kd = k_j * ekd[:, h:h + 1]
                s_h = s_scr[h]
                p = _dot(jnp.concatenate([w, qd], axis=0), s_h)
                vn = u - p[:c]
                o = p[c:] + _dot(a_h, vn)
                s_scr[h] = s_h * egl[:, h:h + 1] + _dot_tn(kd, vn)
                z_h = proj_ref[rows, GDN_CONV_DIM + h * hd:GDN_CONV_DIM + (h + 1) * hd]
                o = _rms_scale(o) * onw * _silu(z_h)
                out_ref[rows, h * hd:(h + 1) * hd] = o.astype(out_ref.dtype)
        return carry

    lax.fori_loop(0, nchunk, chunk_body, 0)


def _gdn_core(proj, ba, bat, conv_w, a_log, dt_bias, out_norm_w, bsz, seq):
    m = proj.shape[0]
    tm = TM_GDN
    nt = seq // tm
    nh = GDN_V_HEADS
    nb = ba.shape[1]
    rows8 = tm // SUBLANES

    def cur_map(b, t):
        return (b * nt + t, 0)

    def prev_map(b, t):
        return (jnp.maximum((b * nt + t) * rows8 - 1, 0), 0)

    return pl.pallas_call(
        _gdn_core_kernel,
        grid=(bsz, nt),
        in_specs=[
            pl.BlockSpec((tm, GDN_MAIN_DIM), cur_map),
            pl.BlockSpec((SUBLANES, GDN_CONV_DIM), prev_map),
            pl.BlockSpec((tm, nb), cur_map),
            pl.BlockSpec((nb, tm), lambda b, t: (0, b * nt + t)),
            _const_spec((GDN_CONV_WIDTH, GDN_CONV_DIM)),
            _const_spec((1, nh)),
            _const_spec((1, nh)),
            _const_spec((nh, 1)),
            _const_spec((nh, 1)),
            _const_spec((1, GDN_HEAD_DIM)),
        ],
        out_specs=pl.BlockSpec((tm, GDN_V_DIM), cur_map),
        out_shape=jax.ShapeDtypeStruct((m, GDN_V_DIM), BF16),
        scratch_shapes=[
            pltpu.VMEM((tm, GDN_CONV_DIM), F32),
            pltpu.VMEM((nh, GDN_HEAD_DIM, GDN_HEAD_DIM), F32),
            pltpu.VMEM((tm, nh), F32),
            pltpu.VMEM((tm // GDN_CHUNK, nh, GDN_CHUNK), F32),
            pltpu.VMEM((tm, nh), F32),
        ],
        compiler_params=_params("parallel", "arbitrary"),
        name="gdn_core",
    )(proj, proj, ba, bat, conv_w, a_log.reshape(1, nh), dt_bias.reshape(1, nh),
      a_log.reshape(nh, 1), dt_bias.reshape(nh, 1), out_norm_w.reshape(1, GDN_HEAD_DIM))


def _mm_res_kernel(x_ref, w_ref, res_ref, out_ref):
    out_ref[...] = res_ref[...] + jnp.dot(x_ref[...].astype(BF16), w_ref[...],
                                          preferred_element_type=F32)


def _mm_res(x2, w, res):
    m, k = x2.shape
    n = w.shape[1]
    tm = TM_MM
    return pl.pallas_call(
        _mm_res_kernel,
        grid=(m // tm,),
        in_specs=[
            pl.BlockSpec((tm, k), lambda i: (i, 0)),
            _const_spec((k, n)),
            pl.BlockSpec((tm, n), lambda i: (i, 0)),
        ],
        out_specs=pl.BlockSpec((tm, n), lambda i: (i, 0)),
        out_shape=jax.ShapeDtypeStruct((m, n), F32),
        compiler_params=_params("parallel"),
        name="mm_res",
    )(x2, w, res)


def _ffn_kernel(h_ref, nw_ref, wup_ref, cw_ref, cb_ref, wdn_ref, fnw_ref, out_ref,
                carry_scr, acc_scr, *, final_norm):
    t = pl.program_id(1)

    @pl.when(t == 0)
    def _():
        carry_scr[...] = jnp.zeros_like(carry_scr)

    h = h_ref[...]
    tm = h.shape[0]
    hn = (_rms_scale(h) * nw_ref[...]).astype(BF16)
    acc_scr[...] = h
    for c0 in range(0, D_FF, FFN_CHUNK):
        halves = []
        for off in (0, D_FF):
            sl = slice(off + c0, off + c0 + FFN_CHUNK)
            cur = jnp.dot(hn, wup_ref[:, sl], preferred_element_type=F32)
            prev8 = carry_scr[:, sl]
            carry_scr[:, sl] = cur[tm - SUBLANES:]
            y = cur * cw_ref[FFN_CONV_WIDTH - 1:FFN_CONV_WIDTH, sl] + cb_ref[:, sl]
            for j in range(1, FFN_CONV_WIDTH):
                y = y + (_shift_rows(cur, prev8, j)
                         * cw_ref[FFN_CONV_WIDTH - 1 - j:FFN_CONV_WIDTH - j, sl])
            halves.append(y)
        act = (_silu(halves[0]) * halves[1]).astype(BF16)
        acc_scr[...] += jnp.dot(act, wdn_ref[c0:c0 + FFN_CHUNK, :], preferred_element_type=F32)
    res = acc_scr[...]
    if final_norm:
        res = _rms_scale(res) * fnw_ref[...]
    out_ref[...] = res


def _ffn(h2, norm_w, w_up, conv_w, conv_b, w_down, final_w, bsz, seq, final_norm):
    m, k = h2.shape
    tm = TM_FFN
    nt = seq // tm

    def row_map(b, t):
        return (b * nt + t, 0)

    return pl.pallas_call(
        functools.partial(_ffn_kernel, final_norm=final_norm),
        grid=(bsz, nt),
        in_specs=[
            pl.BlockSpec((tm, k), row_map),
            _const_spec((1, k)),
            _const_spec((k, 2 * D_FF)),
            _const_spec((FFN_CONV_WIDTH, 2 * D_FF)),
            _const_spec((1, 2 * D_FF)),
            _const_spec((D_FF, k)),
            _const_spec((1, k)),
        ],
        out_specs=pl.BlockSpec((tm, k), row_map),
        out_shape=jax.ShapeDtypeStruct((m, k), F32),
        scratch_shapes=[
            pltpu.VMEM((SUBLANES, 2 * D_FF), F32),
            pltpu.VMEM((tm, k), F32),
        ],
        compiler_params=_params("parallel", "arbitrary"),
        name="ffn_final" if final_norm else "ffn",
    )(h2, norm_w, w_up, conv_w, conv_b, w_down, final_w)


def _qkv_proj_kernel(h_ref, qnw_ref, kvnw_ref, wq_ref, wkv_ref, q_ref, kv_ref):
    y = _rms_scale(h_ref[...])
    q_ref[...] = jnp.dot((y * qnw_ref[...]).astype(BF16), wq_ref[...],
                         preferred_element_type=F32)
    kv_ref[...] = jnp.dot((y * kvnw_ref[...]).astype(BF16), wkv_ref[...],
                          preferred_element_type=F32)


def _qkv_proj(h2, q_norm_w, kv_norm_w, w_q, w_kv):
    m, k = h2.shape
    nq = w_q.shape[1]
    nkv = w_kv.shape[1]
    tm = TM_MM
    return pl.pallas_call(
        _qkv_proj_kernel,
        grid=(m // tm,),
        in_specs=[
            pl.BlockSpec((tm, k), lambda i: (i, 0)),
            _const_spec((1, k)),
            _const_spec((1, k)),
            _const_spec((k, nq)),
            _const_spec((k, nkv)),
        ],
        out_specs=[
            pl.BlockSpec((tm, nq), lambda i: (i, 0)),
            pl.BlockSpec((tm, nkv), lambda i: (i, 0)),
        ],
        out_shape=[
            jax.ShapeDtypeStruct((m, nq), F32),
            jax.ShapeDtypeStruct((m, nkv), F32),
        ],
        compiler_params=_params("parallel"),
        name="qkv_proj",
    )(h2, q_norm_w, kv_norm_w, w_q, w_kv)


def _t5_bucket_map():
    qi = np.arange(SWA_BLOCK)[:, None]
    ki = np.arange(2 * SWA_BLOCK)[None, :]
    dist = qi + SWA_BLOCK - ki
    n = np.maximum(dist, 0)
    max_exact = REL_BUCKETS // 2
    nf = np.maximum(n, 1).astype(np.float32)
    large = max_exact + (np.log(nf / np.float32(max_exact)).astype(np.float32)
                         / np.float32(math.log(REL_MAX_DISTANCE / max_exact))
                         * np.float32(REL_BUCKETS - max_exact)).astype(np.int32)
    large = np.minimum(large, REL_BUCKETS - 1)
    bucket = np.where(n < max_exact, n, large).astype(np.int32)
    in_window = (dist >= 0) & (dist < SWA_WINDOW)
    return np.where(in_window, bucket, -1).astype(np.int32)


def _swa_kernel(table_ref, sinks_ref, q_ref, kvc_ref, kvp_ref, bmap_ref, out_ref, bias_scr):
    n = pl.program_id(1)
    blk = SWA_BLOCK
    hd = SWA_HEAD_DIM
    kvd = SWA_KV_HEADS * hd

    @pl.when((pl.program_id(0) == 0) & (n == 0))
    def _():
        bmap = bmap_ref[...]
        for hq in range(SWA_Q_HEADS):
            bias_scr[hq] = jnp.where(bmap < 0, NEG_INF, 0.0)

        def bucket_body(b, carry):
            hit = bmap == b
            for hq in range(SWA_Q_HEADS):
                bias_scr[hq] = jnp.where(hit, table_ref[b, hq], bias_scr[hq])
            return carry

        lax.fori_loop(0, REL_BUCKETS, bucket_body, 0)

    kv_band = jnp.concatenate([kvp_ref[...], kvc_ref[...]], axis=0)
    lane = lax.broadcasted_iota(jnp.int32, (1, LANES), 1)
    lo = lane < hd
    key_ok = (lax.broadcasted_iota(jnp.int32, (1, 2 * blk), 1) >= blk) | (n > 0)
    for h in range(SWA_KV_HEADS):
        grp = (h * hd) // LANES
        k_grp = kv_band[:, grp * LANES:(grp + 1) * LANES]
        v_grp = kv_band[:, kvd + grp * LANES:kvd + (grp + 1) * LANES]
        mine = lo if (h * hd) % LANES == 0 else ~lo
        k_own = jnp.where(mine, k_grp, 0.0)
        v_own = jnp.where(mine, v_grp, 0.0)
        k_dup = (k_own + pltpu.roll(k_own, hd, 1)).astype(BF16)
        v_dup = (v_own + pltpu.roll(v_own, hd, 1)).astype(BF16)
        q_rows = []
        for gidx in range(SWA_GROUP):
            hq = h * SWA_GROUP + gidx
            qg = q_ref[:, (hq * hd) // LANES * LANES:((hq * hd) // LANES + 1) * LANES]
            sel = lo if (hq * hd) % LANES == 0 else ~lo
            q_rows.append(jnp.where(sel, qg * (hd ** -0.5), 0.0))
        q_stack = jnp.concatenate(q_rows, axis=0).astype(BF16)
        s = _dot_nt(q_stack, k_dup)
        outs = []
        for gidx in range(SWA_GROUP):
            hq = h * SWA_GROUP + gidx
            s_g = s[gidx * blk:(gidx + 1) * blk] + bias_scr[hq]
            s_g = jnp.where(key_ok, s_g, NEG_INF)
            sink = sinks_ref[hq]
            mx = jnp.maximum(jnp.max(s_g, axis=-1, keepdims=True), sink)
            p = jnp.exp(s_g - mx)
            denom = jnp.sum(p, axis=-1, keepdims=True) + jnp.exp(sink - mx)
            probs = (p / denom).astype(BF16)
            outs.append(jnp.dot(probs, v_dup, preferred_element_type=F32))
        for pair in range(SWA_GROUP // 2):
            hq0 = h * SWA_GROUP + 2 * pair
            col = (hq0 * hd) // LANES * LANES
            out_ref[:, col:col + LANES] = jnp.where(lo, outs[2 * pair],
                                                    outs[2 * pair + 1]).astype(out_ref.dtype)


def _swa(q, kv, rel_table, sinks, bsz, seq):
    m = q.shape[0]
    nb = seq // SWA_BLOCK
    bmap = jnp.asarray(_t5_bucket_map())
    grid_spec = pltpu.PrefetchScalarGridSpec(
        num_scalar_prefetch=0,
        grid=(bsz, nb),
        in_specs=[
            pl.BlockSpec(memory_space=pltpu.SMEM),
            pl.BlockSpec(memory_space=pltpu.SMEM),
            pl.BlockSpec((SWA_BLOCK, q.shape[1]), lambda b, n: (b * nb + n, 0)),
            pl.BlockSpec((SWA_BLOCK, kv.shape[1]), lambda b, n: (b * nb + n, 0)),
            pl.BlockSpec((SWA_BLOCK, kv.shape[1]),
                         lambda b, n: (b * nb + jnp.maximum(n - 1, 0), 0)),
            _const_spec((SWA_BLOCK, 2 * SWA_BLOCK)),
        ],
        out_specs=pl.BlockSpec((SWA_BLOCK, q.shape[1]), lambda b, n: (b * nb + n, 0)),
        scratch_shapes=[pltpu.VMEM((SWA_Q_HEADS, SWA_BLOCK, 2 * SWA_BLOCK), F32)],
    )
    return pl.pallas_call(
        _swa_kernel,
        grid_spec=grid_spec,
        out_shape=jax.ShapeDtypeStruct((m, q.shape[1]), BF16),
        compiler_params=_params("arbitrary", "arbitrary"),
        name="swa",
    )(rel_table, sinks, q, kv, kv, bmap)


def kernel(x, a_norm_w, a_w_in, a_conv_w, a_a_log, a_dt_bias, a_out_norm_w, a_w_out,
           kv_norm_w, w_kv, b_norm_w, b_w_q, b_sinks, b_w_o, rel_bias_table,
           ffn_norm_w, ffn_w_up, ffn_conv_w, ffn_conv_b, ffn_w_down, final_norm_w):
    bsz, seq, d = x.shape
    n_a = a_w_in.shape[0]
    n_b = b_w_q.shape[0]
    depth = n_a + n_b
    h = x.reshape(bsz * seq, d)
    kv = None
    for layer in range(depth):
        if layer < n_a:
            i = layer
            w_in = a_w_in[i]
            w_ba = w_in[:, GDN_MAIN_DIM:].astype(BF16)
            proj, ba, bat = _in_proj(h, a_norm_w[i].reshape(1, d),
                                     w_in[:, :GDN_MAIN_DIM].astype(BF16), w_ba, w_ba.T)
            o = _gdn_core(proj, ba, bat, a_conv_w[i], a_a_log[i], a_dt_bias[i],
                          a_out_norm_w[i], bsz, seq)
            h = _mm_res(o, a_w_out[i].astype(BF16), h)
        else:
            j = layer - n_a
            if j == 0:
                q, kv = _qkv_proj(h, b_norm_w[j].reshape(1, d), kv_norm_w.reshape(1, d),
                                  b_w_q[j].astype(BF16), w_kv.astype(BF16))
            else:
                q, _ = _qkv_proj(h, b_norm_w[j].reshape(1, d), kv_norm_w.reshape(1, d),
                                 b_w_q[j].astype(BF16), w_kv.astype(BF16))
            o = _swa(q, kv, rel_bias_table, b_sinks[j], bsz, seq)
            h = _mm_res(o, b_w_o[j].astype(BF16), h)
        last = layer == depth - 1
        h = _ffn(h, ffn_norm_w[layer].reshape(1, d), ffn_w_up[layer].astype(BF16),
                 ffn_conv_w[layer], ffn_conv_b[layer].reshape(1, 2 * D_FF),
                 ffn_w_down[layer].astype(BF16), final_norm_w.reshape(1, d), bsz, seq, last)
    return h.reshape(bsz, seq, d)
```

```python
import functools
import math

import jax
import jax.numpy as jnp
import numpy as np
from jax import lax
from jax.experimental import pallas as pl
from jax.experimental.pallas import tpu as pltpu

F32 = jnp.float32
BF16 = jnp.bfloat16

EPS = 1e-6
NEG_INF = -1e30

D_MODEL = 1024
GDN_QK_HEADS = 8
GDN_V_HEADS = 16
GDN_HEAD_DIM = 128
GDN_QK_DIM = GDN_QK_HEADS * GDN_HEAD_DIM
GDN_V_DIM = GDN_V_HEADS * GDN_HEAD_DIM
GDN_CONV_DIM = 2 * GDN_QK_DIM + GDN_V_DIM
GDN_MAIN_DIM = GDN_CONV_DIM + GDN_V_DIM
GDN_CONV_WIDTH = 4
GDN_CHUNK = 64

SWA_Q_HEADS = 16
SWA_KV_HEADS = 4
SWA_GROUP = SWA_Q_HEADS // SWA_KV_HEADS
SWA_HEAD_DIM = 64
SWA_WINDOW = 128
SWA_BLOCK = 128
REL_BUCKETS = 32
REL_MAX_DISTANCE = 128

D_FF = 2816
FFN_CONV_WIDTH = 3
FFN_CHUNK = 256

SUBLANES = 8
LANES = 128
VMEM_LIMIT_BYTES = 56 * 1024 * 1024

TM_PROJ = 256
TM_GDN = 256
TM_FFN = 256
TM_MM = 512


def _params(*sem):
    return pltpu.CompilerParams(dimension_semantics=sem, vmem_limit_bytes=VMEM_LIMIT_BYTES)


def _const_spec(shape):
    zeros = (0,) * len(shape)
    return pl.BlockSpec(shape, lambda *_: zeros)


def _dot(a, b):
    return jnp.dot(a.astype(BF16), b.astype(BF16), preferred_element_type=F32)


def _dot_nt(a, b):
    return lax.dot_general(a.astype(BF16), b.astype(BF16), (((1,), (1,)), ((), ())),
                           preferred_element_type=F32)


def _dot_tn(a, b):
    return lax.dot_general(a.astype(BF16), b.astype(BF16), (((0,), (0,)), ((), ())),
                           preferred_element_type=F32)


def _dot_f32(a, b):
    return jnp.dot(a, b, preferred_element_type=F32, precision=lax.Precision.HIGHEST)


def _rms_scale(x):
    return x * lax.rsqrt(jnp.mean(x * x, axis=-1, keepdims=True) + EPS)


def _silu(x):
    return x * (1.0 / (1.0 + jnp.exp(-x)))


def _shift_rows(cur, prev8, j):
    if j == 0:
        return cur
    r = pltpu.roll(cur, j, 0)
    pr = pltpu.roll(prev8, j, 0)
    row = lax.broadcasted_iota(jnp.int32, pr.shape, 0)
    top = jnp.where(row < j, pr, r[:SUBLANES])
    return jnp.concatenate([top, r[SUBLANES:]], axis=0)


def _in_proj_kernel(x_ref, nw_ref, wm_ref, wba_ref, wbat_ref, main_ref, ba_ref, bat_ref):
    hn = (_rms_scale(x_ref[...]) * nw_ref[...]).astype(BF16)
    n = main_ref.shape[1]
    step = 512
    for c in range(0, n, step):
        main_ref[:, c:c + step] = jnp.dot(hn, wm_ref[:, c:c + step], preferred_element_type=F32)
    ba_ref[...] = jnp.dot(hn, wba_ref[...], preferred_element_type=F32)
    bat_ref[...] = _dot_nt(wbat_ref[...], hn)


def _in_proj(x2, norm_w, w_main, w_ba, w_bat):
    m, k = x2.shape
    n = w_main.shape[1]
    nb = w_ba.shape[1]
    tm = TM_PROJ
    return pl.pallas_call(
        _in_proj_kernel,
        grid=(m // tm,),
        in_specs=[
            pl.BlockSpec((tm, k), lambda i: (i, 0)),
            _const_spec((1, k)),
            _const_spec((k, n)),
            _const_spec((k, nb)),
            _const_spec((nb, k)),
        ],
        out_specs=[
            pl.BlockSpec((tm, n), lambda i: (i, 0)),
            pl.BlockSpec((tm, nb), lambda i: (i, 0)),
            pl.BlockSpec((nb, tm), lambda i: (0, i)),
        ],
        out_shape=[
            jax.ShapeDtypeStruct((m, n), F32),
            jax.ShapeDtypeStruct((m, nb), F32),
            jax.ShapeDtypeStruct((nb, m), F32),
        ],
        compiler_params=_params("parallel"),
        name="gdn_in_proj",
    )(x2, norm_w, w_main, w_ba, w_bat)


def _tri_inverse(m_strict):
    c = m_strict.shape[0]
    row = lax.broadcasted_iota(jnp.int32, (c, c), 0)
    col = lax.broadcasted_iota(jnp.int32, (c, c), 1)
    eye = (row == col).astype(F32)
    x = eye - m_strict
    p = _dot(m_strict, m_strict)
    levels = int(math.log2(c)) - 1
    for lvl in range(levels):
        if lvl + 1 < levels:
            xp = _dot(jnp.concatenate([x, p], axis=0), p)
            x = x + xp[:c]
            p = xp[c:]
        else:
            x = x + _dot(x, p)
    return x


def _gdn_core_kernel(proj_ref, prev_ref, ba_ref, bat_ref, cw_ref, alog_ref, dtb_ref,
                     alogt_ref, dtbt_ref, onw_ref, out_ref,
                     qkv_scr, s_scr, gc_scr, gct_scr, beta_scr):
    t = pl.program_id(1)
    tm = proj_ref.shape[0]
    c = GDN_CHUNK
    nchunk = tm // c
    hd = GDN_HEAD_DIM

    @pl.when(t == 0)
    def _():
        s_scr[...] = jnp.zeros_like(s_scr)

    keep_prev = (t > 0).astype(F32)
    for cg in range(GDN_CONV_DIM // hd):
        sl = slice(cg * hd, (cg + 1) * hd)
        cur = proj_ref[:, sl]
        prev8 = prev_ref[:, sl] * keep_prev
        y = cur * cw_ref[GDN_CONV_WIDTH - 1:GDN_CONV_WIDTH, sl]
        for j in range(1, GDN_CONV_WIDTH):
            y = y + _shift_rows(cur, prev8, j) * cw_ref[GDN_CONV_WIDTH - 1 - j:GDN_CONV_WIDTH - j, sl]
        y = _silu(y)
        if cg < 2 * GDN_QK_HEADS:
            y = y * lax.rsqrt(jnp.sum(y * y, axis=-1, keepdims=True) + EPS)
            if cg < GDN_QK_HEADS:
                y = y * (hd ** -0.5)
        qkv_scr[:, sl] = y

    nh = GDN_V_HEADS
    b_lin = ba_ref[:, :nh]
    a_lin = ba_ref[:, nh:2 * nh] + dtb_ref[...]
    beta_scr[...] = 1.0 / (1.0 + jnp.exp(-b_lin))
    g = -jnp.exp(alog_ref[...]) * (jnp.maximum(a_lin, 0.0) + jnp.log1p(jnp.exp(-jnp.abs(a_lin))))
    a_lin_t = bat_ref[nh:2 * nh, :] + dtbt_ref[...]
    g_t = -jnp.exp(alogt_ref[...]) * (jnp.maximum(a_lin_t, 0.0)
                                      + jnp.log1p(jnp.exp(-jnp.abs(a_lin_t))))
    row = lax.broadcasted_iota(jnp.int32, (tm, tm), 0)
    col = lax.broadcasted_iota(jnp.int32, (tm, tm), 1)
    same = (row // c) == (col // c)
    lblk = (same & (col <= row)).astype(F32)
    ublk = (same & (row <= col)).astype(F32)
    gc_scr[...] = _dot_f32(lblk, g)
    gct = _dot_f32(g_t, ublk)
    for ci in range(nchunk):
        gct_scr[ci] = gct[:, ci * c:(ci + 1) * c]

    ri = lax.broadcasted_iota(jnp.int32, (c, c), 0)
    cj = lax.broadcasted_iota(jnp.int32, (c, c), 1)
    tril = cj <= ri
    strict = cj < ri
    onw = onw_ref[...]
    rep = GDN_V_HEADS // GDN_QK_HEADS
    heads = range(GDN_V_HEADS)
    eye = (ri == cj).astype(F32)

    def chunk_body(ci, carry):
        r0 = pl.multiple_of(ci * c, c)
        rows = pl.ds(r0, c)
        gc = gc_scr[rows, :]
        gct_c = gct_scr[ci]
        beta = beta_scr[rows, :]
        egc = jnp.exp(gc)
        gl = gc[c - 1:c, :]
        ekd = jnp.exp(gl - gc)
        egl = jnp.exp(gl)
        q = [qkv_scr[rows, j * hd:(j + 1) * hd] for j in range(GDN_QK_HEADS)]
        k = [qkv_scr[rows, GDN_QK_DIM + j * hd:GDN_QK_DIM + (j + 1) * hd]
             for j in range(GDN_QK_HEADS)]
        gq = [_dot_nt(jnp.concatenate([k[j], q[j]], axis=0), k[j]) for j in range(GDN_QK_HEADS)]
        dec = [jnp.where(tril, jnp.exp(jnp.where(tril, gc[:, h:h + 1] - gct_c[h:h + 1, :], 0.0)), 0.0)
               for h in heads]
        m = [jnp.where(strict, beta[:, h:h + 1] * gq[h // rep][:c] * dec[h], 0.0) for h in heads]
        a = [gq[h // rep][c:] * dec[h] for h in heads]
        x = [eye - m[h] for h in heads]
        p = [_dot(m[h], m[h]) for h in heads]
        levels = int(math.log2(c)) - 1
        for lvl in range(levels):
            if lvl + 1 < levels:
                xp = [_dot(jnp.concatenate([x[h], p[h]], axis=0), p[h]) for h in heads]
                x = [x[h] + xp[h][:c] for h in heads]
                p = [xp[h][c:] for h in heads]
            else:
                xp = [_dot(x[h], p[h]) for h in heads]
                x = [x[h] + xp[h] for h in heads]
        v = [qkv_scr[rows, 2 * GDN_QK_DIM + h * hd:2 * GDN_QK_DIM + (h + 1) * hd] for h in heads]
        uw = [_dot(x[h], jnp.concatenate(
            [v[h] * beta[:, h:h + 1], k[h // rep] * (beta[:, h:h + 1] * egc[:, h:h + 1])], axis=1))
            for h in heads]
        s_old = [s_scr[h] for h in heads]
        pq = [_dot(jnp.concatenate([uw[h][:, hd:], q[h // rep] * egc[:, h:h + 1]], axis=0), s_old[h])
              for h in heads]
        vn = [uw[h][:, :hd] - pq[h][:c] for h in heads]
        o = [pq[h][c:] + _dot(a[h], vn[h]) for h in heads]
        for h in heads:
            s_scr[h] = (s_old[h] * egl[:, h:h + 1]
                        + _dot_tn(k[h // rep] * ekd[:, h:h + 1], vn[h]))
        for h in heads:
            z_h = proj_ref[rows, GDN_CONV_DIM + h * hd:GDN_CONV_DIM + (h + 1) * hd]
            out_ref[rows, h * hd:(h + 1) * hd] = (
                _rms_scale(o[h]) * onw * _silu(z_h)).astype(out_ref.dtype)
        return carry

    lax.fori_loop(0, nchunk, chunk_body, 0)


def _gdn_core(proj, ba, bat, conv_w, a_log, dt_bias, out_norm_w, bsz, seq):
    m = proj.shape[0]
    tm = TM_GDN
    nt = seq // tm
    nh = GDN_V_HEADS
    nb = ba.shape[1]
    rows8 = tm // SUBLANES

    def cur_map(b, t):
        return (b * nt + t, 0)

    def prev_map(b, t):
        return (jnp.maximum((b * nt + t) * rows8 - 1, 0), 0)

    return pl.pallas_call(
        _gdn_core_kernel,
        grid=(bsz, nt),
        in_specs=[
            pl.BlockSpec((tm, GDN_MAIN_DIM), cur_map),
            pl.BlockSpec((SUBLANES, GDN_CONV_DIM), prev_map),
            pl.BlockSpec((tm, nb), cur_map),
            pl.BlockSpec((nb, tm), lambda b, t: (0, b * nt + t)),
            _const_spec((GDN_CONV_WIDTH, GDN_CONV_DIM)),
            _const_spec((1, nh)),
            _const_spec((1, nh)),
            _const_spec((nh, 1)),
            _const_spec((nh, 1)),
            _const_spec((1, GDN_HEAD_DIM)),
        ],
        out_specs=pl.BlockSpec((tm, GDN_V_DIM), cur_map),
        out_shape=jax.ShapeDtypeStruct((m, GDN_V_DIM), BF16),
        scratch_shapes=[
            pltpu.VMEM((tm, GDN_CONV_DIM), F32),
            pltpu.VMEM((nh, GDN_HEAD_DIM, GDN_HEAD_DIM), F32),
            pltpu.VMEM((tm, nh), F32),
            pltpu.VMEM((tm // GDN_CHUNK, nh, GDN_CHUNK), F32),
            pltpu.VMEM((tm, nh), F32),
        ],
        compiler_params=_params("parallel", "arbitrary"),
        name="gdn_core",
    )(proj, proj, ba, bat, conv_w, a_log.reshape(1, nh), dt_bias.reshape(1, nh),
      a_log.reshape(nh, 1), dt_bias.reshape(nh, 1), out_norm_w.reshape(1, GDN_HEAD_DIM))


def _mm_res_kernel(x_ref, w_ref, res_ref, out_ref):
    out_ref[...] = res_ref[...] + jnp.dot(x_ref[...].astype(BF16), w_ref[...],
                                          preferred_element_type=F32)


def _mm_res(x2, w, res):
    m, k = x2.shape
    n = w.shape[1]
    tm = TM_MM
    return pl.pallas_call(
        _mm_res_kernel,
        grid=(m // tm,),
        in_specs=[
            pl.BlockSpec((tm, k), lambda i: (i, 0)),
            _const_spec((k, n)),
            pl.BlockSpec((tm, n), lambda i: (i, 0)),
        ],
        out_specs=pl.BlockSpec((tm, n), lambda i: (i, 0)),
        out_shape=jax.ShapeDtypeStruct((m, n), F32),
        compiler_params=_params("parallel"),
        name="mm_res",
    )(x2, w, res)


def _ffn_kernel(h_ref, nw_ref, wup_ref, cw_ref, cb_ref, wdn_ref, fnw_ref, out_ref,
                carry_scr, acc_scr, *, final_norm):
    t = pl.program_id(1)

    @pl.when(t == 0)
    def _():
        carry_scr[...] = jnp.zeros_like(carry_scr)

    h = h_ref[...]
    tm = h.shape[0]
    hn = (_rms_scale(h) * nw_ref[...]).astype(BF16)
    acc_scr[...] = h
    for c0 in range(0, D_FF, FFN_CHUNK):
        halves = []
        for off in (0, D_FF):
            sl = slice(off + c0, off + c0 + FFN_CHUNK)
            cur = jnp.dot(hn, wup_ref[:, sl], preferred_element_type=F32)
            prev8 = carry_scr[:, sl]
            carry_scr[:, sl] = cur[tm - SUBLANES:]
            y = cur * cw_ref[FFN_CONV_WIDTH - 1:FFN_CONV_WIDTH, sl] + cb_ref[:, sl]
            for j in range(1, FFN_CONV_WIDTH):
                y = y + (_shift_rows(cur, prev8, j)
                         * cw_ref[FFN_CONV_WIDTH - 1 - j:FFN_CONV_WIDTH - j, sl])
            halves.append(y)
        act = (_silu(halves[0]) * halves[1]).astype(BF16)
        acc_scr[...] += jnp.dot(act, wdn_ref[c0:c0 + FFN_CHUNK, :], preferred_element_type=F32)
    res = acc_scr[...]
    if final_norm:
        res = _rms_scale(res) * fnw_ref[...]
    out_ref[...] = res


def _ffn(h2, norm_w, w_up, conv_w, conv_b, w_down, final_w, bsz, seq, final_norm):
    m, k = h2.shape
    tm = TM_FFN
    nt = seq // tm

    def row_map(b, t):
        return (b * nt + t, 0)

    return pl.pallas_call(
        functools.partial(_ffn_kernel, final_norm=final_norm),
        grid=(bsz, nt),
        in_specs=[
            pl.BlockSpec((tm, k), row_map),
            _const_spec((1, k)),
            _const_spec((k, 2 * D_FF)),
            _const_spec((FFN_CONV_WIDTH, 2 * D_FF)),
            _const_spec((1, 2 * D_FF)),
            _const_spec((D_FF, k)),
            _const_spec((1, k)),
        ],
        out_specs=pl.BlockSpec((tm, k), row_map),
        out_shape=jax.ShapeDtypeStruct((m, k), F32),
        scratch_shapes=[
            pltpu.VMEM((SUBLANES, 2 * D_FF), F32),
            pltpu.VMEM((tm, k), F32),
        ],
        compiler_params=_params("parallel", "arbitrary"),
        name="ffn_final" if final_norm else "ffn",
    )(h2, norm_w, w_up, conv_w, conv_b, w_down, final_w)


def _qkv_proj_kernel(h_ref, qnw_ref, kvnw_ref, wq_ref, wkv_ref, q_ref, kv_ref):
    y = _rms_scale(h_ref[...])
    q_ref[...] = jnp.dot((y * qnw_ref[...]).astype(BF16), wq_ref[...],
                         preferred_element_type=F32)
    kv_ref[...] = jnp.dot((y * kvnw_ref[...]).astype(BF16), wkv_ref[...],
                          preferred_element_type=F32)


def _qkv_proj(h2, q_norm_w, kv_norm_w, w_q, w_kv):
    m, k = h2.shape
    nq = w_q.shape[1]
    nkv = w_kv.shape[1]
    tm = TM_MM
    return pl.pallas_call(
        _qkv_proj_kernel,
        grid=(m // tm,),
        in_specs=[
            pl.BlockSpec((tm, k), lambda i: (i, 0)),
            _const_spec((1, k)),
            _const_spec((1, k)),
            _const_spec((k, nq)),
            _const_spec((k, nkv)),
        ],
        out_specs=[
            pl.BlockSpec((tm, nq), lambda i: (i, 0)),
            pl.BlockSpec((tm, nkv), lambda i: (i, 0)),
        ],
        out_shape=[
            jax.ShapeDtypeStruct((m, nq), F32),
            jax.ShapeDtypeStruct((m, nkv), F32),
        ],
        compiler_params=_params("parallel"),
        name="qkv_proj",
    )(h2, q_norm_w, kv_norm_w, w_q, w_kv)


def _t5_bucket_map():
    qi = np.arange(SWA_BLOCK)[:, None]
    ki = np.arange(2 * SWA_BLOCK)[None, :]
    dist = qi + SWA_BLOCK - ki
    n = np.maximum(dist, 0)
    max_exact = REL_BUCKETS // 2
    nf = np.maximum(n, 1).astype(np.float32)
    large = max_exact + (np.log(nf / np.float32(max_exact)).astype(np.float32)
                         / np.float32(math.log(REL_MAX_DISTANCE / max_exact))
                         * np.float32(REL_BUCKETS - max_exact)).astype(np.int32)
    large = np.minimum(large, REL_BUCKETS - 1)
    bucket = np.where(n < max_exact, n, large).astype(np.int32)
    in_window = (dist >= 0) & (dist < SWA_WINDOW)
    return np.where(in_window, bucket, -1).astype(np.int32)


def _swa_kernel(table_ref, sinks_ref, q_ref, kvc_ref, kvp_ref, bmap_ref, out_ref, bias_scr):
    n = pl.program_id(1)
    blk = SWA_BLOCK
    hd = SWA_HEAD_DIM
    kvd = SWA_KV_HEADS * hd

    @pl.when((pl.program_id(0) == 0) & (n == 0))
    def _():
        bmap = bmap_ref[...]
        for hq in range(SWA_Q_HEADS):
            bias_scr[hq] = jnp.where(bmap < 0, NEG_INF, 0.0)

        def bucket_body(b, carry):
            hit = bmap == b
            for hq in range(SWA_Q_HEADS):
                bias_scr[hq] = jnp.where(hit, table_ref[b, hq], bias_scr[hq])
            return carry

        lax.fori_loop(0, REL_BUCKETS, bucket_body, 0)

    kv_band = jnp.concatenate([kvp_ref[...], kvc_ref[...]], axis=0)
    lane = lax.broadcasted_iota(jnp.int32, (1, LANES), 1)
    lo = lane < hd
    key_ok = (lax.broadcasted_iota(jnp.int32, (1, 2 * blk), 1) >= blk) | (n > 0)
    for h in range(SWA_KV_HEADS):
        grp = (h * hd) // LANES
        k_grp = kv_band[:, grp * LANES:(grp + 1) * LANES]
        v_grp = kv_band[:, kvd + grp * LANES:kvd + (grp + 1) * LANES]
        mine = lo if (h * hd) % LANES == 0 else ~lo
        k_own = jnp.where(mine, k_grp, 0.0)
        v_own = jnp.where(mine, v_grp, 0.0)
        k_dup = (k_own + pltpu.roll(k_own, hd, 1)).astype(BF16)
        v_dup = (v_own + pltpu.roll(v_own, hd, 1)).astype(BF16)
        q_rows = []
        for gidx in range(SWA_GROUP):
            hq = h * SWA_GROUP + gidx
            qg = q_ref[:, (hq * hd) // LANES * LANES:((hq * hd) // LANES + 1) * LANES]
            sel = lo if (hq * hd) % LANES == 0 else ~lo
            q_rows.append(jnp.where(sel, qg * (hd ** -0.5), 0.0))
        q_stack = jnp.concatenate(q_rows, axis=0).astype(BF16)
        s = _dot_nt(q_stack, k_dup)
        outs = []
        for gidx in range(SWA_GROUP):
            hq = h * SWA_GROUP + gidx
            s_g = s[gidx * blk:(gidx + 1) * blk] + bias_scr[hq]
            s_g = jnp.where(key_ok, s_g, NEG_INF)
            sink = sinks_ref[hq]
            mx = jnp.maximum(jnp.max(s_g, axis=-1, keepdims=True), sink)
            p = jnp.exp(s_g - mx)
            denom = jnp.sum(p, axis=-1, keepdims=True) + jnp.exp(sink - mx)
            probs = (p / denom).astype(BF16)
            outs.append(jnp.dot(probs, v_dup, preferred_element_type=F32))
        for pair in range(SWA_GROUP // 2):
            hq0 = h * SWA_GROUP + 2 * pair
            col = (hq0 * hd) // LANES * LANES
            out_ref[:, col:col + LANES] = jnp.where(lo, outs[2 * pair],
                                                    outs[2 * pair + 1]).astype(out_ref.dtype)


def _swa(q, kv, rel_table, sinks, bsz, seq):
    m = q.shape[0]
    nb = seq // SWA_BLOCK
    bmap = jnp.asarray(_t5_bucket_map())
    grid_spec = pltpu.PrefetchScalarGridSpec(
        num_scalar_prefetch=0,
        grid=(bsz, nb),
        in_specs=[
            pl.BlockSpec(memory_space=pltpu.SMEM),
            pl.BlockSpec(memory_space=pltpu.SMEM),
            pl.BlockSpec((SWA_BLOCK, q.shape[1]), lambda b, n: (b * nb + n, 0)),
            pl.BlockSpec((SWA_BLOCK, kv.shape[1]), lambda b, n: (b * nb + n, 0)),
            pl.BlockSpec((SWA_BLOCK, kv.shape[1]),
                         lambda b, n: (b * nb + jnp.maximum(n - 1, 0), 0)),
            _const_spec((SWA_BLOCK, 2 * SWA_BLOCK)),
        ],
        out_specs=pl.BlockSpec((SWA_BLOCK, q.shape[1]), lambda b, n: (b * nb + n, 0)),
        scratch_shapes=[pltpu.VMEM((SWA_Q_HEADS, SWA_BLOCK, 2 * SWA_BLOCK), F32)],
    )
    return pl.pallas_call(
        _swa_kernel,
        grid_spec=grid_spec,
        out_shape=jax.ShapeDtypeStruct((m, q.shape[1]), BF16),
        compiler_params=_params("arbitrary", "arbitrary"),
        name="swa",
    )(rel_table, sinks, q, kv, kv, bmap)


def kernel(x, a_norm_w, a_w_in, a_conv_w, a_a_log, a_dt_bias, a_out_norm_w, a_w_out,
           kv_norm_w, w_kv, b_norm_w, b_w_q, b_sinks, b_w_o, rel_bias_table,
           ffn_norm_w, ffn_w_up, ffn_conv_w, ffn_conv_b, ffn_w_down, final_norm_w):
    bsz, seq, d = x.shape
    n_a = a_w_in.shape[0]
    n_b = b_w_q.shape[0]
    depth = n_a + n_b
    h = x.reshape(bsz * seq, d)
    kv = None
    for layer in range(depth):
        if layer < n_a:
            i = layer
            w_in = a_w_in[i]
            w_ba = w_in[:, GDN_MAIN_DIM:].astype(BF16)
            proj, ba, bat = _in_proj(h, a_norm_w[i].reshape(1, d),
                                     w_in[:, :GDN_MAIN_DIM].astype(BF16), w_ba, w_ba.T)
            o = _gdn_core(proj, ba, bat, a_conv_w[i], a_a_log[i], a_dt_bias[i],
                          a_out_norm_w[i], bsz, seq)
            h = _mm_res(o, a_w_out[i].astype(BF16), h)
        else:
            j = layer - n_a
            if j == 0:
                q, kv = _qkv_proj(h, b_norm_w[j].reshape(1, d), kv_norm_w.reshape(1, d),
                                  b_w_q[j].astype(BF16), w_kv.astype(BF16))
            else:
                q, _ = _qkv_proj(h, b_norm_w[j].reshape(1, d), kv_norm_w.reshape(1, d),
                                 b_w_q[j].astype(BF16), w_kv.astype(BF16))
            o = _swa(q, kv, rel_bias_table, b_sinks[j], bsz, seq)
            h = _mm_res(o, b_w_o[j].astype(BF16), h)
        last = layer == depth - 1
        h = _ffn(h, ffn_norm_w[layer].reshape(1, d), ffn_w_up[layer].astype(BF16),
                 ffn_conv_w[layer], ffn_conv_b[layer].reshape(1, 2 * D_FF),
                 ffn_w_down[layer].astype(BF16), final_norm_w.reshape(1, d), bsz, seq, last)
    return h.reshape(bsz, seq, d)
```

```python
import functools
import math

import jax
import jax.numpy as jnp
import numpy as np
from jax import lax
from jax.experimental import pallas as pl
from jax.experimental.pallas import tpu as pltpu

F32 = jnp.float32
BF16 = jnp.bfloat16

EPS = 1e-6
NEG_INF = -1e30

D_MODEL = 1024
GDN_QK_HEADS = 8
GDN_V_HEADS = 16
GDN_HEAD_DIM = 128
GDN_QK_DIM = GDN_QK_HEADS * GDN_HEAD_DIM
GDN_V_DIM = GDN_V_HEADS * GDN_HEAD_DIM
GDN_CONV_DIM = 2 * GDN_QK_DIM + GDN_V_DIM
GDN_MAIN_DIM = GDN_CONV_DIM + GDN_V_DIM
GDN_CONV_WIDTH = 4
GDN_CHUNK = 64

SWA_Q_HEADS = 16
SWA_KV_HEADS = 4
SWA_GROUP = SWA_Q_HEADS // SWA_KV_HEADS
SWA_HEAD_DIM = 64
SWA_WINDOW = 128
SWA_BLOCK = 128
REL_BUCKETS = 32
REL_MAX_DISTANCE = 128

D_FF = 2816
FFN_CONV_WIDTH = 3
FFN_CHUNK = 256

SUBLANES = 8
LANES = 128
VMEM_LIMIT_BYTES = 56 * 1024 * 1024

TM_PROJ = 256
TM_GDN = 256
TM_FFN = 256
TM_MM = 512


def _params(*sem):
    return pltpu.CompilerParams(dimension_semantics=sem, vmem_limit_bytes=VMEM_LIMIT_BYTES)


def _const_spec(shape):
    zeros = (0,) * len(shape)
    return pl.BlockSpec(shape, lambda *_: zeros)


def _dot(a, b):
    return jnp.dot(a.astype(BF16), b.astype(BF16), preferred_element_type=F32)


def _dot_nt(a, b):
    return lax.dot_general(a.astype(BF16), b.astype(BF16), (((1,), (1,)), ((), ())),
                           preferred_element_type=F32)


def _dot_tn(a, b):
    return lax.dot_general(a.astype(BF16), b.astype(BF16), (((0,), (0,)), ((), ())),
                           preferred_element_type=F32)


def _dot_f32(a, b):
    return jnp.dot(a, b, preferred_element_type=F32, precision=lax.Precision.HIGHEST)


def _rms_scale(x):
    return x * lax.rsqrt(jnp.mean(x * x, axis=-1, keepdims=True) + EPS)


def _silu(x):
    return x * (1.0 / (1.0 + jnp.exp(-x)))


def _shift_rows(cur, prev8, j):
    if j == 0:
        return cur
    r = pltpu.roll(cur, j, 0)
    pr = pltpu.roll(prev8, j, 0)
    row = lax.broadcasted_iota(jnp.int32, pr.shape, 0)
    top = jnp.where(row < j, pr, r[:SUBLANES])
    return jnp.concatenate([top, r[SUBLANES:]], axis=0)


def _in_proj_kernel(x_ref, nw_ref, wm_ref, wba_ref, wbat_ref, main_ref, ba_ref, bat_ref):
    hn = (_rms_scale(x_ref[...]) * nw_ref[...]).astype(BF16)
    n = main_ref.shape[1]
    step = 512
    for c in range(0, n, step):
        main_ref[:, c:c + step] = jnp.dot(hn, wm_ref[:, c:c + step], preferred_element_type=F32)
    ba_ref[...] = jnp.dot(hn, wba_ref[...], preferred_element_type=F32)
    bat_ref[...] = _dot_nt(wbat_ref[...], hn)


def _in_proj(x2, norm_w, w_main, w_ba, w_bat):
    m, k = x2.shape
    n = w_main.shape[1]
    nb = w_ba.shape[1]
    tm = TM_PROJ
    return pl.pallas_call(
        _in_proj_kernel,
        grid=(m // tm,),
        in_specs=[
            pl.BlockSpec((tm, k), lambda i: (i, 0)),
            _const_spec((1, k)),
            _const_spec((k, n)),
            _const_spec((k, nb)),
            _const_spec((nb, k)),
        ],
        out_specs=[
            pl.BlockSpec((tm, n), lambda i: (i, 0)),
            pl.BlockSpec((tm, nb), lambda i: (i, 0)),
            pl.BlockSpec((nb, tm), lambda i: (0, i)),
        ],
        out_shape=[
            jax.ShapeDtypeStruct((m, n), F32),
            jax.ShapeDtypeStruct((m, nb), F32),
            jax.ShapeDtypeStruct((nb, m), F32),
        ],
        compiler_params=_params("parallel"),
        name="gdn_in_proj",
    )(x2, norm_w, w_main, w_ba, w_bat)


def _tri_inverse(m_strict):
    c = m_strict.shape[0]
    row = lax.broadcasted_iota(jnp.int32, (c, c), 0)
    col = lax.broadcasted_iota(jnp.int32, (c, c), 1)
    eye = (row == col).astype(F32)
    x = eye - m_strict
    p = _dot(m_strict, m_strict)
    levels = int(math.log2(c)) - 1
    for lvl in range(levels):
        if lvl + 1 < levels:
            xp = _dot(jnp.concatenate([x, p], axis=0), p)
            x = x + xp[:c]
            p = xp[c:]
        else:
            x = x + _dot(x, p)
    return x


def _gdn_core_kernel(proj_ref, prev_ref, ba_ref, bat_ref, cw_ref, alog_ref, dtb_ref,
                     alogt_ref, dtbt_ref, onw_ref, out_ref,
                     qkv_scr, s_scr, gc_scr, gct_scr, beta_scr):
    t = pl.program_id(1)
    tm = proj_ref.shape[0]
    c = GDN_CHUNK
    nchunk = tm // c
    hd = GDN_HEAD_DIM

    @pl.when(t == 0)
    def _():
        s_scr[...] = jnp.zeros_like(s_scr)

    keep_prev = (t > 0).astype(F32)
    for cg in range(GDN_CONV_DIM // hd):
        sl = slice(cg * hd, (cg + 1) * hd)
        cur = proj_ref[:, sl]
        prev8 = prev_ref[:, sl] * keep_prev
        y = cur * cw_ref[GDN_CONV_WIDTH - 1:GDN_CONV_WIDTH, sl]
        for j in range(1, GDN_CONV_WIDTH):
            y = y + _shift_rows(cur, prev8, j) * cw_ref[GDN_CONV_WIDTH - 1 - j:GDN_CONV_WIDTH - j, sl]
        y = _silu(y)
        if cg < 2 * GDN_QK_HEADS:
            y = y * lax.rsqrt(jnp.sum(y * y, axis=-1, keepdims=True) + EPS)
            if cg < GDN_QK_HEADS:
                y = y * (hd ** -0.5)
        qkv_scr[:, sl] = y

    nh = GDN_V_HEADS
    b_lin = ba_ref[:, :nh]
    a_lin = ba_ref[:, nh:2 * nh] + dtb_ref[...]
    beta_scr[...] = 1.0 / (1.0 + jnp.exp(-b_lin))
    g = -jnp.exp(alog_ref[...]) * (jnp.maximum(a_lin, 0.0) + jnp.log1p(jnp.exp(-jnp.abs(a_lin))))
    a_lin_t = bat_ref[nh:2 * nh, :] + dtbt_ref[...]
    g_t = -jnp.exp(alogt_ref[...]) * (jnp.maximum(a_lin_t, 0.0)
                                      + jnp.log1p(jnp.exp(-jnp.abs(a_lin_t))))
    row = lax.broadcasted_iota(jnp.int32, (tm, tm), 0)
    col = lax.broadcasted_iota(jnp.int32, (tm, tm), 1)
    same = (row // c) == (col // c)
    lblk = (same & (col <= row)).astype(F32)
    ublk = (same & (row <= col)).astype(F32)
    gc_scr[...] = _dot_f32(lblk, g)
    gct = _dot_f32(g_t, ublk)
    for ci in range(nchunk):
        gct_scr[ci] = gct[:, ci * c:(ci + 1) * c]

    ri = lax.broadcasted_iota(jnp.int32, (c, c), 0)
    cj = lax.broadcasted_iota(jnp.int32, (c, c), 1)
    tril = cj <= ri
    strict = cj < ri
    onw = onw_ref[...]
    rep = GDN_V_HEADS // GDN_QK_HEADS
    heads = range(GDN_V_HEADS)
    eye = (ri == cj).astype(F32)

    def chunk_body(ci, carry):
        r0 = pl.multiple_of(ci * c, c)
        rows = pl.ds(r0, c)
        gc = gc_scr[rows, :]
        gct_c = gct_scr[ci]
        beta = beta_scr[rows, :]
        egc = jnp.exp(gc)
        gl = gc[c - 1:c, :]
        ekd = jnp.exp(gl - gc)
        egl = jnp.exp(gl)
        q = [qkv_scr[rows, j * hd:(j + 1) * hd] for j in range(GDN_QK_HEADS)]
        k = [qkv_scr[rows, GDN_QK_DIM + j * hd:GDN_QK_DIM + (j + 1) * hd]
             for j in range(GDN_QK_HEADS)]
        gq = [_dot_nt(jnp.concatenate([k[j], q[j]], axis=0), k[j]) for j in range(GDN_QK_HEADS)]
        dec = [jnp.where(tril, jnp.exp(jnp.where(tril, gc[:, h:h + 1] - gct_c[h:h + 1, :], 0.0)), 0.0)
               for h in heads]
        m = [jnp.where(strict, beta[:, h:h + 1] * gq[h // rep][:c] * dec[h], 0.0) for h in heads]
        a = [gq[h // rep][c:] * dec[h] for h in heads]
        x = [eye - m[h] for h in heads]
        p = [_dot(m[h], m[h]) for h in heads]
        levels = int(math.log2(c)) - 1
        for lvl in range(levels):
            if lvl + 1 < levels:
                xp = [_dot(jnp.concatenate([x[h], p[h]], axis=0), p[h]) for h in heads]
                x = [x[h] + xp[h][:c] for h in heads]
                p = [xp[h][c:] for h in heads]
            else:
                xp = [_dot(x[h], p[h]) for h in heads]
                x = [x[h] + xp[h] for h in heads]
        v = [qkv_scr[rows, 2 * GDN_QK_DIM + h * hd:2 * GDN_QK_DIM + (h + 1) * hd] for h in heads]
        uw = [_dot(x[h], jnp.concatenate(
            [v[h] * beta[:, h:h + 1], k[h // rep] * (beta[:, h:h + 1] * egc[:, h:h + 1])], axis=1))
            for h in heads]
        s_old = [s_scr[h] for h in heads]
        pq = [_dot(jnp.concatenate([uw[h][:, hd:], q[h // rep] * egc[:, h:h + 1]], axis=0), s_old[h])
              for h in heads]
        vn = [uw[h][:, :hd] - pq[h][:c] for h in heads]
        o = [pq[h][c:] + _dot(a[h], vn[h]) for h in heads]
        for h in heads:
            s_scr[h] = (s_old[h] * egl[:, h:h + 1]
                        + _dot_tn(k[h // rep] * ekd[:, h:h + 1], vn[h]))
        for h in heads:
            z_h = proj_ref[rows, GDN_CONV_DIM + h * hd:GDN_CONV_DIM + (h + 1) * hd]
            out_ref[rows, h * hd:(h + 1) * hd] = (
                _rms_scale(o[h]) * onw * _silu(z_h)).astype(out_ref.dtype)
        return carry

    lax.fori_loop(0, nchunk, chunk_body, 0)


def _gdn_core(proj, ba, bat, conv_w, a_log, dt_bias, out_norm_w, bsz, seq):
    m = proj.shape[0]
    tm = TM_GDN
    nt = seq // tm
    nh = GDN_V_HEADS
    nb = ba.shape[1]
    rows8 = tm // SUBLANES

    def cur_map(b, t):
        return (b * nt + t, 0)

    def prev_map(b, t):
        return (jnp.maximum((b * nt + t) * rows8 - 1, 0), 0)

    return pl.pallas_call(
        _gdn_core_kernel,
        grid=(bsz, nt),
        in_specs=[
            pl.BlockSpec((tm, GDN_MAIN_DIM), cur_map),
            pl.BlockSpec((SUBLANES, GDN_CONV_DIM), prev_map),
            pl.BlockSpec((tm, nb), cur_map),
            pl.BlockSpec((nb, tm), lambda b, t: (0, b * nt + t)),
            _const_spec((GDN_CONV_WIDTH, GDN_CONV_DIM)),
            _const_spec((1, nh)),
            _const_spec((1, nh)),
            _const_spec((nh, 1)),
            _const_spec((nh, 1)),
            _const_spec((1, GDN_HEAD_DIM)),
        ],
        out_specs=pl.BlockSpec((tm, GDN_V_DIM), cur_map),
        out_shape=jax.ShapeDtypeStruct((m, GDN_V_DIM), BF16),
        scratch_shapes=[
            pltpu.VMEM((tm, GDN_CONV_DIM), F32),
            pltpu.VMEM((nh, GDN_HEAD_DIM, GDN_HEAD_DIM), F32),
            pltpu.VMEM((tm, nh), F32),
            pltpu.VMEM((tm // GDN_CHUNK, nh, GDN_CHUNK), F32),
            pltpu.VMEM((tm, nh), F32),
        ],
        compiler_params=_params("parallel", "arbitrary"),
        name="gdn_core",
    )(proj, proj, ba, bat, conv_w, a_log.reshape(1, nh), dt_bias.reshape(1, nh),
      a_log.reshape(nh, 1), dt_bias.reshape(nh, 1), out_norm_w.reshape(1, GDN_HEAD_DIM))


def _mm_res_kernel(x_ref, w_ref, res_ref, out_ref):
    out_ref[...] = res_ref[...] + jnp.dot(x_ref[...].astype(BF16), w_ref[...],
                                          preferred_element_type=F32)


def _mm_res(x2, w, res):
    m, k = x2.shape
    n = w.shape[1]
    tm = TM_MM
    return pl.pallas_call(
        _mm_res_kernel,
        grid=(m // tm,),
        in_specs=[
            pl.BlockSpec((tm, k), lambda i: (i, 0)),
            _const_spec((k, n)),
            pl.BlockSpec((tm, n), lambda i: (i, 0)),
        ],
        out_specs=pl.BlockSpec((tm, n), lambda i: (i, 0)),
        out_shape=jax.ShapeDtypeStruct((m, n), F32),
        compiler_params=_params("parallel"),
        name="mm_res",
    )(x2, w, res)


def _ffn_kernel(h_ref, nw_ref, wup_ref, cw_ref, cb_ref, wdn_ref, fnw_ref, out_ref,
                carry_scr, act_scr, *, final_norm):
    t = pl.program_id(1)

    @pl.when(t == 0)
    def _():
        carry_scr[...] = jnp.zeros_like(carry_scr)

    h = h_ref[...]
    tm = h.shape[0]
    hn = (_rms_scale(h) * nw_ref[...]).astype(BF16)
    for c0 in range(0, D_FF, FFN_CHUNK):
        halves = []
        for off in (0, D_FF):
            sl = slice(off + c0, off + c0 + FFN_CHUNK)
            cur = jnp.dot(hn, wup_ref[:, sl], preferred_element_type=F32)
            prev8 = carry_scr[:, sl]
            carry_scr[:, sl] = cur[tm - SUBLANES:]
            y = cur * cw_ref[FFN_CONV_WIDTH - 1:FFN_CONV_WIDTH, sl] + cb_ref[:, sl]
            for j in range(1, FFN_CONV_WIDTH):
                y = y + (_shift_rows(cur, prev8, j)
                         * cw_ref[FFN_CONV_WIDTH - 1 - j:FFN_CONV_WIDTH - j, sl])
            halves.append(y)
        act_scr[:, c0:c0 + FFN_CHUNK] = (_silu(halves[0]) * halves[1]).astype(BF16)
    res = h + jnp.dot(act_scr[...], wdn_ref[...], preferred_element_type=F32)
    if final_norm:
        res = _rms_scale(res) * fnw_ref[...]
    out_ref[...] = res


def _ffn(h2, norm_w, w_up, conv_w, conv_b, w_down, final_w, bsz, seq, final_norm):
    m, k = h2.shape
    tm = TM_FFN
    nt = seq // tm

    def row_map(b, t):
        return (b * nt + t, 0)

    return pl.pallas_call(
        functools.partial(_ffn_kernel, final_norm=final_norm),
        grid=(bsz, nt),
        in_specs=[
            pl.BlockSpec((tm, k), row_map),
            _const_spec((1, k)),
            _const_spec((k, 2 * D_FF)),
            _const_spec((FFN_CONV_WIDTH, 2 * D_FF)),
            _const_spec((1, 2 * D_FF)),
            _const_spec((D_FF, k)),
            _const_spec((1, k)),
        ],
        out_specs=pl.BlockSpec((tm, k), row_map),
        out_shape=jax.ShapeDtypeStruct((m, k), F32),
        scratch_shapes=[
            pltpu.VMEM((SUBLANES, 2 * D_FF), F32),
            pltpu.VMEM((tm, D_FF), BF16),
        ],
        compiler_params=_params("parallel", "arbitrary"),
        name="ffn_final" if final_norm else "ffn",
    )(h2, norm_w, w_up, conv_w, conv_b, w_down, final_w)


def _qkv_proj_kernel(h_ref, qnw_ref, kvnw_ref, wq_ref, wkv_ref, q_ref, kv_ref):
    y = _rms_scale(h_ref[...])
    q_ref[...] = jnp.dot((y * qnw_ref[...]).astype(BF16), wq_ref[...],
                         preferred_element_type=F32)
    kv_ref[...] = jnp.dot((y * kvnw_ref[...]).astype(BF16), wkv_ref[...],
                          preferred_element_type=F32)


def _qkv_proj(h2, q_norm_w, kv_norm_w, w_q, w_kv):
    m, k = h2.shape
    nq = w_q.shape[1]
    nkv = w_kv.shape[1]
    tm = TM_MM
    return pl.pallas_call(
        _qkv_proj_kernel,
        grid=(m // tm,),
        in_specs=[
            pl.BlockSpec((tm, k), lambda i: (i, 0)),
            _const_spec((1, k)),
            _const_spec((1, k)),
            _const_spec((k, nq)),
            _const_spec((k, nkv)),
        ],
        out_specs=[
            pl.BlockSpec((tm, nq), lambda i: (i, 0)),
            pl.BlockSpec((tm, nkv), lambda i: (i, 0)),
        ],
        out_shape=[
            jax.ShapeDtypeStruct((m, nq), F32),
            jax.ShapeDtypeStruct((m, nkv), F32),
        ],
        compiler_params=_params("parallel"),
        name="qkv_proj",
    )(h2, q_norm_w, kv_norm_w, w_q, w_kv)


def _t5_bucket_map():
    qi = np.arange(SWA_BLOCK)[:, None]
    ki = np.arange(2 * SWA_BLOCK)[None, :]
    dist = qi + SWA_BLOCK - ki
    n = np.maximum(dist, 0)
    max_exact = REL_BUCKETS // 2
    nf = np.maximum(n, 1).astype(np.float32)
    large = max_exact + (np.log(nf / np.float32(max_exact)).astype(np.float32)
                         / np.float32(math.log(REL_MAX_DISTANCE / max_exact))
                         * np.float32(REL_BUCKETS - max_exact)).astype(np.int32)
    large = np.minimum(large, REL_BUCKETS - 1)
    bucket = np.where(n < max_exact, n, large).astype(np.int32)
    in_window = (dist >= 0) & (dist < SWA_WINDOW)
    return np.where(in_window, bucket, -1).astype(np.int32)


def _swa_kernel(table_ref, sinks_ref, q_ref, kvc_ref, kvp_ref, bmap_ref, out_ref, bias_scr):
    n = pl.program_id(1)
    blk = SWA_BLOCK
    hd = SWA_HEAD_DIM
    kvd = SWA_KV_HEADS * hd

    @pl.when((pl.program_id(0) == 0) & (n == 0))
    def _():
        bmap = bmap_ref[...]
        for hq in range(SWA_Q_HEADS):
            bias_scr[0, hq] = jnp.where(bmap < 0, NEG_INF, 0.0)

        def bucket_body(b, carry):
            hit = bmap == b
            for hq in range(SWA_Q_HEADS):
                bias_scr[0, hq] = jnp.where(hit, table_ref[b, hq], bias_scr[0, hq])
            return carry

        lax.fori_loop(0, REL_BUCKETS, bucket_body, 0)
        key_col = lax.broadcasted_iota(jnp.int32, (blk, 2 * blk), 1)
        for hq in range(SWA_Q_HEADS):
            bias_scr[1, hq] = jnp.where(key_col >= blk, bias_scr[0, hq], NEG_INF)
        for hq in range(SWA_Q_HEADS):
            for tbl in range(2):
                bias_scr[tbl, hq] = jnp.where(key_col == 0, sinks_ref[hq], bias_scr[tbl, hq])

    first = (n == 0).astype(jnp.int32)
    kv_band = jnp.concatenate([kvp_ref[...], kvc_ref[...]], axis=0)
    lane = lax.broadcasted_iota(jnp.int32, (1, LANES), 1)
    lo = lane < hd
    ones_blk = jnp.ones((2 * blk, LANES), BF16)
    not_slot0 = lax.broadcasted_iota(jnp.int32, (2 * blk, 1), 0) > 0
    kv_heads = range(SWA_KV_HEADS)
    k_dup, v_aug, q_stack = [], [], []
    for h in kv_heads:
        grp = (h * hd) // LANES
        k_grp = kv_band[:, grp * LANES:(grp + 1) * LANES]
        v_grp = kv_band[:, kvd + grp * LANES:kvd + (grp + 1) * LANES]
        mine = (lo if (h * hd) % LANES == 0 else ~lo) & not_slot0
        k_own = jnp.where(mine, k_grp, 0.0)
        v_own = jnp.where(mine, v_grp, 0.0)
        k_dup.append((k_own + pltpu.roll(k_own, hd, 1)).astype(BF16))
        v_aug.append(jnp.concatenate(
            [(v_own + pltpu.roll(v_own, hd, 1)).astype(BF16), ones_blk], axis=1))
        q_rows = []
        for gidx in range(SWA_GROUP):
            hq = h * SWA_GROUP + gidx
            qg = q_ref[:, (hq * hd) // LANES * LANES:((hq * hd) // LANES + 1) * LANES]
            sel = lo if (hq * hd) % LANES == 0 else ~lo
            q_rows.append(jnp.where(sel, qg * (hd ** -0.5), 0.0))
        q_stack.append(jnp.concatenate(q_rows, axis=0).astype(BF16))
    s = [_dot_nt(q_stack[h], k_dup[h]) for h in kv_heads]
    sb = [s[h] + bias_scr[first, pl.ds(h * SWA_GROUP, SWA_GROUP)].reshape(SWA_GROUP * blk, 2 * blk)
          for h in kv_heads]
    mx = [jnp.max(sb[h], axis=-1, keepdims=True) for h in kv_heads]
    p = [jnp.exp(sb[h] - mx[h]).astype(BF16) for h in kv_heads]
    oa = [jnp.dot(p[h], v_aug[h], preferred_element_type=F32) for h in kv_heads]
    outs = [oa[h][:, :LANES] / oa[h][:, LANES:] for h in kv_heads]
    for h in kv_heads:
        for pair in range(SWA_GROUP // 2):
            hq0 = h * SWA_GROUP + 2 * pair
            col = (hq0 * hd) // LANES * LANES
            even = outs[h][(2 * pair) * blk:(2 * pair + 1) * blk]
            odd = outs[h][(2 * pair + 1) * blk:(2 * pair + 2) * blk]
            out_ref[:, col:col + LANES] = jnp.where(lo, even, odd).astype(out_ref.dtype)


def _swa(q, kv, rel_table, sinks, bsz, seq):
    m = q.shape[0]
    nb = seq // SWA_BLOCK
    bmap = jnp.asarray(_t5_bucket_map())
    grid_spec = pltpu.PrefetchScalarGridSpec(
        num_scalar_prefetch=0,
        grid=(bsz, nb),
        in_specs=[
            pl.BlockSpec(memory_space=pltpu.SMEM),
            pl.BlockSpec(memory_space=pltpu.SMEM),
            pl.BlockSpec((SWA_BLOCK, q.shape[1]), lambda b, n: (b * nb + n, 0)),
            pl.BlockSpec((SWA_BLOCK, kv.shape[1]), lambda b, n: (b * nb + n, 0)),
            pl.BlockSpec((SWA_BLOCK, kv.shape[1]),
                         lambda b, n: (b * nb + jnp.maximum(n - 1, 0), 0)),
            _const_spec((SWA_BLOCK, 2 * SWA_BLOCK)),
        ],
        out_specs=pl.BlockSpec((SWA_BLOCK, q.shape[1]), lambda b, n: (b * nb + n, 0)),
        scratch_shapes=[pltpu.VMEM((2, SWA_Q_HEADS, SWA_BLOCK, 2 * SWA_BLOCK), F32)],
    )
    return pl.pallas_call(
        _swa_kernel,
        grid_spec=grid_spec,
        out_shape=jax.ShapeDtypeStruct((m, q.shape[1]), BF16),
        compiler_params=_params("arbitrary", "arbitrary"),
        name="swa",
    )(rel_table, sinks, q, kv, kv, bmap)


def kernel(x, a_norm_w, a_w_in, a_conv_w, a_a_log, a_dt_bias, a_out_norm_w, a_w_out,
           kv_norm_w, w_kv, b_norm_w, b_w_q, b_sinks, b_w_o, rel_bias_table,
           ffn_norm_w, ffn_w_up, ffn_conv_w, ffn_conv_b, ffn_w_down, final_norm_w):
    bsz, seq, d = x.shape
    n_a = a_w_in.shape[0]
    n_b = b_w_q.shape[0]
    depth = n_a + n_b
    h = x.reshape(bsz * seq, d)
    kv = None
    for layer in range(depth):
        if layer < n_a:
            i = layer
            w_in = a_w_in[i]
            w_ba = w_in[:, GDN_MAIN_DIM:].astype(BF16)
            proj, ba, bat = _in_proj(h, a_norm_w[i].reshape(1, d),
                                     w_in[:, :GDN_MAIN_DIM].astype(BF16), w_ba, w_ba.T)
            o = _gdn_core(proj, ba, bat, a_conv_w[i], a_a_log[i], a_dt_bias[i],
                          a_out_norm_w[i], bsz, seq)
            h = _mm_res(o, a_w_out[i].astype(BF16), h)
        else:
            j = layer - n_a
            if j == 0:
                q, kv = _qkv_proj(h, b_norm_w[j].reshape(1, d), kv_norm_w.reshape(1, d),
                                  b_w_q[j].astype(BF16), w_kv.astype(BF16))
            else:
                q, _ = _qkv_proj(h, b_norm_w[j].reshape(1, d), kv_norm_w.reshape(1, d),
                                 b_w_q[j].astype(BF16), w_kv.astype(BF16))
            o = _swa(q, kv, rel_bias_table, b_sinks[j], bsz, seq)
            h = _mm_res(o, b_w_o[j].astype(BF16), h)
        last = layer == depth - 1
        h = _ffn(h, ffn_norm_w[layer].reshape(1, d), ffn_w_up[layer].astype(BF16),
                 ffn_conv_w[layer], ffn_conv_b[layer].reshape(1, 2 * D_FF),
                 ffn_w_down[layer].astype(BF16), final_norm_w.reshape(1, d), bsz, seq, last)
    return h.reshape(bsz, seq, d)
```

```python
import functools
import math

import jax
import jax.numpy as jnp
import numpy as np
from jax import lax
from jax.experimental import pallas as pl
from jax.experimental.pallas import tpu as pltpu

F32 = jnp.float32
BF16 = jnp.bfloat16

EPS = 1e-6
NEG_INF = -1e30

D_MODEL = 1024
GDN_QK_HEADS = 8
GDN_V_HEADS = 16
GDN_HEAD_DIM = 128
GDN_QK_DIM = GDN_QK_HEADS * GDN_HEAD_DIM
GDN_V_DIM = GDN_V_HEADS * GDN_HEAD_DIM
GDN_CONV_DIM = 2 * GDN_QK_DIM + GDN_V_DIM
GDN_MAIN_DIM = GDN_CONV_DIM + GDN_V_DIM
GDN_CONV_WIDTH = 4
GDN_CHUNK = 64

SWA_Q_HEADS = 16
SWA_KV_HEADS = 4
SWA_GROUP = SWA_Q_HEADS // SWA_KV_HEADS
SWA_HEAD_DIM = 64
SWA_WINDOW = 128
SWA_BLOCK = 128
REL_BUCKETS = 32
REL_MAX_DISTANCE = 128

D_FF = 2816
FFN_CONV_WIDTH = 3
FFN_CHUNK = 256
CONV_ROWS = 64

SUBLANES = 8
LANES = 128
VMEM_LIMIT_BYTES = 56 * 1024 * 1024

TM_PROJ = 256
TM_GDN = 256
TM_FFN = 256
TM_MM = 512


def _params(*sem):
    return pltpu.CompilerParams(dimension_semantics=sem, vmem_limit_bytes=VMEM_LIMIT_BYTES)


def _const_spec(shape):
    zeros = (0,) * len(shape)
    return pl.BlockSpec(shape, lambda *_: zeros)


def _dot(a, b):
    return jnp.dot(a.astype(BF16), b.astype(BF16), preferred_element_type=F32)


def _dot_nt(a, b):
    return lax.dot_general(a.astype(BF16), b.astype(BF16), (((1,), (1,)), ((), ())),
                           preferred_element_type=F32)


def _dot_tn(a, b):
    return lax.dot_general(a.astype(BF16), b.astype(BF16), (((0,), (0,)), ((), ())),
                           preferred_element_type=F32)


def _dot_f32(a, b):
    return jnp.dot(a, b, preferred_element_type=F32, precision=lax.Precision.HIGHEST)


def _rms_scale(x):
    return x * lax.rsqrt(jnp.mean(x * x, axis=-1, keepdims=True) + EPS)


def _silu(x):
    half = 0.5 * x
    return half + half * jnp.tanh(half)


def _shift_rows(cur, prev8, j):
    if j == 0:
        return cur
    r = pltpu.roll(cur, j, 0)
    pr = pltpu.roll(prev8, j, 0)
    row = lax.broadcasted_iota(jnp.int32, pr.shape, 0)
    top = jnp.where(row < j, pr, r[:SUBLANES])
    return jnp.concatenate([top, r[SUBLANES:]], axis=0)


def _in_proj_kernel(x_ref, nw_ref, wm_ref, wba_ref, wbat_ref, cw_ref,
                    qkv_ref, z_ref, ba_ref, bat_ref, carry_scr):
    t = pl.program_id(1)

    @pl.when(t == 0)
    def _():
        carry_scr[...] = jnp.zeros_like(carry_scr)

    tm = x_ref.shape[0]
    hd = GDN_HEAD_DIM
    hn = (_rms_scale(x_ref[...]) * nw_ref[...]).astype(BF16)
    step = 2 * hd
    zevery = GDN_CONV_DIM // GDN_V_DIM
    for c0 in range(0, GDN_CONV_DIM, step):
        if (c0 // step) % zevery == 0:
            z0 = c0 // zevery
            z_ref[:, z0:z0 + step] = jnp.dot(
                hn, wm_ref[:, GDN_CONV_DIM + z0:GDN_CONV_DIM + z0 + step], preferred_element_type=F32)
        sl = slice(c0, c0 + step)
        cur = jnp.dot(hn, wm_ref[:, sl], preferred_element_type=F32)
        prev8 = carry_scr[:, sl]
        carry_scr[:, sl] = cur[tm - SUBLANES:]
        for hc in range(c0, c0 + step, hd):
            lanes = slice(hc - c0, hc - c0 + hd)
            wts = [cw_ref[GDN_CONV_WIDTH - 1 - j:GDN_CONV_WIDTH - j, hc:hc + hd]
                   for j in range(GDN_CONV_WIDTH)]
            for r0 in range(0, tm, CONV_ROWS):
                piece = cur[r0:r0 + CONV_ROWS, lanes]
                before = prev8[:, lanes] if r0 == 0 else cur[r0 - SUBLANES:r0, lanes]
                yh = piece * wts[0]
                for j in range(1, GDN_CONV_WIDTH):
                    yh = yh + _shift_rows(piece, before, j) * wts[j]
                yh = _silu(yh)
                if hc < 2 * GDN_QK_DIM:
                    yh = yh * lax.rsqrt(jnp.sum(yh * yh, axis=-1, keepdims=True) + EPS)
                    if hc < GDN_QK_DIM:
                        yh = yh * (hd ** -0.5)
                qkv_ref[r0:r0 + CONV_ROWS, hc:hc + hd] = yh
    ba_ref[...] = jnp.dot(hn, wba_ref[...], preferred_element_type=F32)
    bat_ref[...] = _dot_nt(wbat_ref[...], hn)


def _in_proj(x2, norm_w, w_main, w_ba, w_bat, conv_w, bsz, seq):
    m, k = x2.shape
    n = w_main.shape[1]
    nb = w_ba.shape[1]
    tm = TM_PROJ
    nt = seq // tm

    def row_map(b, t):
        return (b * nt + t, 0)

    return pl.pallas_call(
        _in_proj_kernel,
        grid=(bsz, nt),
        in_specs=[
            pl.BlockSpec((tm, k), row_map),
            _const_spec((1, k)),
            _const_spec((k, n)),
            _const_spec((k, nb)),
            _const_spec((nb, k)),
            _const_spec((GDN_CONV_WIDTH, GDN_CONV_DIM)),
        ],
        out_specs=[
            pl.BlockSpec((tm, GDN_CONV_DIM), row_map),
            pl.BlockSpec((tm, GDN_V_DIM), row_map),
            pl.BlockSpec((tm, nb), row_map),
            pl.BlockSpec((nb, tm), lambda b, t: (0, b * nt + t)),
        ],
        out_shape=[
            jax.ShapeDtypeStruct((m, GDN_CONV_DIM), F32),
            jax.ShapeDtypeStruct((m, GDN_V_DIM), F32),
            jax.ShapeDtypeStruct((m, nb), F32),
            jax.ShapeDtypeStruct((nb, m), F32),
        ],
        scratch_shapes=[pltpu.VMEM((SUBLANES, GDN_CONV_DIM), F32)],
        compiler_params=_params("parallel", "arbitrary"),
        name="gdn_in_proj",
    )(x2, norm_w, w_main, w_ba, w_bat, conv_w)


def _gdn_core_kernel(qkv_ref, z_ref, ba_ref, bat_ref, alog_ref, dtb_ref,
                     alogt_ref, dtbt_ref, onw_ref, out_ref,
                     s_scr, gc_scr, gct_scr, beta_scr):
    t = pl.program_id(1)
    tm = qkv_ref.shape[0]
    c = GDN_CHUNK
    nchunk = tm // c
    hd = GDN_HEAD_DIM

    @pl.when(t == 0)
    def _():
        s_scr[...] = jnp.zeros_like(s_scr)

    nh = GDN_V_HEADS
    b_lin = ba_ref[:, :nh]
    a_lin = ba_ref[:, nh:2 * nh] + dtb_ref[...]
    beta_scr[...] = 1.0 / (1.0 + jnp.exp(-b_lin))
    g = -jnp.exp(alog_ref[...]) * (jnp.maximum(a_lin, 0.0) + jnp.log1p(jnp.exp(-jnp.abs(a_lin))))
    a_lin_t = bat_ref[nh:2 * nh, :] + dtbt_ref[...]
    g_t = -jnp.exp(alogt_ref[...]) * (jnp.maximum(a_lin_t, 0.0)
                                      + jnp.log1p(jnp.exp(-jnp.abs(a_lin_t))))
    row = lax.broadcasted_iota(jnp.int32, (tm, tm), 0)
    col = lax.broadcasted_iota(jnp.int32, (tm, tm), 1)
    same = (row // c) == (col // c)
    lblk = (same & (col <= row)).astype(F32)
    ublk = (same & (row <= col)).astype(F32)
    gc_scr[...] = _dot_f32(lblk, g)
    gct = _dot_f32(g_t, ublk)
    for ci in range(nchunk):
        gct_scr[ci] = gct[:, ci * c:(ci + 1) * c]

    ri = lax.broadcasted_iota(jnp.int32, (c, c), 0)
    cj = lax.broadcasted_iota(jnp.int32, (c, c), 1)
    tril = cj <= ri
    strict = cj < ri
    onw = onw_ref[...]
    rep = GDN_V_HEADS // GDN_QK_HEADS
    heads = range(GDN_V_HEADS)
    eye = (ri == cj).astype(F32)

    def chunk_body(ci, carry):
        r0 = pl.multiple_of(ci * c, c)
        rows = pl.ds(r0, c)
        gc = gc_scr[rows, :]
        gct_c = gct_scr[ci]
        beta = beta_scr[rows, :]
        egc = jnp.exp(gc)
        gl = gc[c - 1:c, :]
        ekd = jnp.exp(gl - gc)
        egl = jnp.exp(gl)
        q = [qkv_ref[rows, j * hd:(j + 1) * hd] for j in range(GDN_QK_HEADS)]
        k = [qkv_ref[rows, GDN_QK_DIM + j * hd:GDN_QK_DIM + (j + 1) * hd]
             for j in range(GDN_QK_HEADS)]
        gq = [_dot_nt(jnp.concatenate([k[j], q[j]], axis=0), k[j]) for j in range(GDN_QK_HEADS)]
        dec = [jnp.where(tril, jnp.exp(jnp.where(tril, gc[:, h:h + 1] - gct_c[h:h + 1, :], 0.0)), 0.0)
               for h in heads]
        m = [jnp.where(strict, beta[:, h:h + 1] * gq[h // rep][:c] * dec[h], 0.0) for h in heads]
        a = [gq[h // rep][c:] * dec[h] for h in heads]
        x = [eye - m[h] for h in heads]
        p = [_dot(m[h], m[h]) for h in heads]
        levels = int(math.log2(c)) - 1
        for lvl in range(levels):
            if lvl + 1 < levels:
                xp = [_dot(jnp.concatenate([x[h], p[h]], axis=0), p[h]) for h in heads]
                x = [x[h] + xp[h][:c] for h in heads]
                p = [xp[h][c:] for h in heads]
            else:
                xp = [_dot(x[h], p[h]) for h in heads]
                x = [x[h] + xp[h] for h in heads]
        v = [qkv_ref[rows, 2 * GDN_QK_DIM + h * hd:2 * GDN_QK_DIM + (h + 1) * hd] for h in heads]
        uw = [_dot(x[h], jnp.concatenate(
            [v[h] * beta[:, h:h + 1], k[h // rep] * (beta[:, h:h + 1] * egc[:, h:h + 1])], axis=1))
            for h in heads]
        s_old = [s_scr[h] for h in heads]
        pq = [_dot(jnp.concatenate([uw[h][:, hd:], q[h // rep] * egc[:, h:h + 1]], axis=0), s_old[h])
              for h in heads]
        vn = [uw[h][:, :hd] - pq[h][:c] for h in heads]
        o = [pq[h][c:] + _dot(a[h], vn[h]) for h in heads]
        for h in heads:
            s_scr[h] = (s_old[h] * egl[:, h:h + 1]
                        + _dot_tn(k[h // rep] * ekd[:, h:h + 1], vn[h]))
        for h in heads:
            z_h = z_ref[rows, h * hd:(h + 1) * hd]
            out_ref[rows, h * hd:(h + 1) * hd] = (
                _rms_scale(o[h]) * onw * _silu(z_h)).astype(out_ref.dtype)
        return carry

    lax.fori_loop(0, nchunk, chunk_body, 0)


def _gdn_core(qkv, z, ba, bat, a_log, dt_bias, out_norm_w, bsz, seq):
    m = qkv.shape[0]
    tm = TM_GDN
    nt = seq // tm
    nh = GDN_V_HEADS
    nb = ba.shape[1]

    def cur_map(b, t):
        return (b * nt + t, 0)

    return pl.pallas_call(
        _gdn_core_kernel,
        grid=(bsz, nt),
        in_specs=[
            pl.BlockSpec((tm, GDN_CONV_DIM), cur_map),
            pl.BlockSpec((tm, GDN_V_DIM), cur_map),
            pl.BlockSpec((tm, nb), cur_map),
            pl.BlockSpec((nb, tm), lambda b, t: (0, b * nt + t)),
            _const_spec((1, nh)),
            _const_spec((1, nh)),
            _const_spec((nh, 1)),
            _const_spec((nh, 1)),
            _const_spec((1, GDN_HEAD_DIM)),
        ],
        out_specs=pl.BlockSpec((tm, GDN_V_DIM), cur_map),
        out_shape=jax.ShapeDtypeStruct((m, GDN_V_DIM), BF16),
        scratch_shapes=[
            pltpu.VMEM((nh, GDN_HEAD_DIM, GDN_HEAD_DIM), F32),
            pltpu.VMEM((tm, nh), F32),
            pltpu.VMEM((tm // GDN_CHUNK, nh, GDN_CHUNK), F32),
            pltpu.VMEM((tm, nh), F32),
        ],
        compiler_params=_params("parallel", "arbitrary"),
        name="gdn_core",
    )(qkv, z, ba, bat, a_log.reshape(1, nh), dt_bias.reshape(1, nh),
      a_log.reshape(nh, 1), dt_bias.reshape(nh, 1), out_norm_w.reshape(1, GDN_HEAD_DIM))


def _mm_res_kernel(x_ref, w_ref, res_ref, out_ref):
    out_ref[...] = res_ref[...] + jnp.dot(x_ref[...].astype(BF16), w_ref[...],
                                          preferred_element_type=F32)


def _mm_res(x2, w, res):
    m, k = x2.shape
    n = w.shape[1]
    tm = TM_MM
    return pl.pallas_call(
        _mm_res_kernel,
        grid=(m // tm,),
        in_specs=[
            pl.BlockSpec((tm, k), lambda i: (i, 0)),
            _const_spec((k, n)),
            pl.BlockSpec((tm, n), lambda i: (i, 0)),
        ],
        out_specs=pl.BlockSpec((tm, n), lambda i: (i, 0)),
        out_shape=jax.ShapeDtypeStruct((m, n), F32),
        compiler_params=_params("parallel"),
        name="mm_res",
    )(x2, w, res)


def _ffn_kernel(h_ref, nw_ref, wup_ref, cw_ref, cb_ref, wdn_ref, fnw_ref, out_ref,
                carry_scr, act_scr, *, final_norm):
    t = pl.program_id(1)

    @pl.when(t == 0)
    def _():
        carry_scr[...] = jnp.zeros_like(carry_scr)

    h = h_ref[...]
    tm = h.shape[0]
    hn = (_rms_scale(h) * nw_ref[...]).astype(BF16)
    for c0 in range(0, D_FF, FFN_CHUNK):
        halves = []
        for off in (0, D_FF):
            sl = slice(off + c0, off + c0 + FFN_CHUNK)
            cur = jnp.dot(hn, wup_ref[:, sl], preferred_element_type=F32)
            prev8 = carry_scr[:, sl]
            carry_scr[:, sl] = cur[tm - SUBLANES:]
            y = cur * cw_ref[FFN_CONV_WIDTH - 1:FFN_CONV_WIDTH, sl] + cb_ref[:, sl]
            for j in range(1, FFN_CONV_WIDTH):
                y = y + (_shift_rows(cur, prev8, j)
                         * cw_ref[FFN_CONV_WIDTH - 1 - j:FFN_CONV_WIDTH - j, sl])
            halves.append(y)
        act_scr[:, c0:c0 + FFN_CHUNK] = (_silu(halves[0]) * halves[1]).astype(BF16)
    res = h + jnp.dot(act_scr[...], wdn_ref[...], preferred_element_type=F32)
    if final_norm:
        res = _rms_scale(res) * fnw_ref[...]
    out_ref[...] = res


def _ffn(h2, norm_w, w_up, conv_w, conv_b, w_down, final_w, bsz, seq, final_norm):
    m, k = h2.shape
    tm = TM_FFN
    nt = seq // tm

    def row_map(b, t):
        return (b * nt + t, 0)

    return pl.pallas_call(
        functools.partial(_ffn_kernel, final_norm=final_norm),
        grid=(bsz, nt),
        in_specs=[
            pl.BlockSpec((tm, k), row_map),
            _const_spec((1, k)),
            _const_spec((k, 2 * D_FF)),
            _const_spec((FFN_CONV_WIDTH, 2 * D_FF)),
            _const_spec((1, 2 * D_FF)),
            _const_spec((D_FF, k)),
            _const_spec((1, k)),
        ],
        out_specs=pl.BlockSpec((tm, k), row_map),
        out_shape=jax.ShapeDtypeStruct((m, k), F32),
        scratch_shapes=[
            pltpu.VMEM((SUBLANES, 2 * D_FF), F32),
            pltpu.VMEM((tm, D_FF), BF16),
        ],
        compiler_params=_params("parallel", "arbitrary"),
        name="ffn_final" if final_norm else "ffn",
    )(h2, norm_w, w_up, conv_w, conv_b, w_down, final_w)


def _qkv_proj_kernel(h_ref, qnw_ref, kvnw_ref, wq_ref, wkv_ref, q_ref, kv_ref):
    y = _rms_scale(h_ref[...])
    q_ref[...] = jnp.dot((y * qnw_ref[...]).astype(BF16), wq_ref[...],
                         preferred_element_type=F32)
    kv_ref[...] = jnp.dot((y * kvnw_ref[...]).astype(BF16), wkv_ref[...],
                          preferred_element_type=F32)


def _qkv_proj(h2, q_norm_w, kv_norm_w, w_q, w_kv):
    m, k = h2.shape
    nq = w_q.shape[1]
    nkv = w_kv.shape[1]
    tm = TM_MM
    return pl.pallas_call(
        _qkv_proj_kernel,
        grid=(m // tm,),
        in_specs=[
            pl.BlockSpec((tm, k), lambda i: (i, 0)),
            _const_spec((1, k)),
            _const_spec((1, k)),
            _const_spec((k, nq)),
            _const_spec((k, nkv)),
        ],
        out_specs=[
            pl.BlockSpec((tm, nq), lambda i: (i, 0)),
            pl.BlockSpec((tm, nkv), lambda i: (i, 0)),
        ],
        out_shape=[
            jax.ShapeDtypeStruct((m, nq), F32),
            jax.ShapeDtypeStruct((m, nkv), F32),
        ],
        compiler_params=_params("parallel"),
        name="qkv_proj",
    )(h2, q_norm_w, kv_norm_w, w_q, w_kv)


def _t5_bucket_map():
    qi = np.arange(SWA_BLOCK)[:, None]
    ki = np.arange(2 * SWA_BLOCK)[None, :]
    dist = qi + SWA_BLOCK - ki
    n = np.maximum(dist, 0)
    max_exact = REL_BUCKETS // 2
    nf = np.maximum(n, 1).astype(np.float32)
    large = max_exact + (np.log(nf / np.float32(max_exact)).astype(np.float32)
                         / np.float32(math.log(REL_MAX_DISTANCE / max_exact))
                         * np.float32(REL_BUCKETS - max_exact)).astype(np.int32)
    large = np.minimum(large, REL_BUCKETS - 1)
    bucket = np.where(n < max_exact, n, large).astype(np.int32)
    in_window = (dist >= 0) & (dist < SWA_WINDOW)
    return np.where(in_window, bucket, -1).astype(np.int32)


def _swa_kernel(table_ref, sinks_ref, q_ref, kvc_ref, kvp_ref, bmap_ref, out_ref, bias_scr):
    n = pl.program_id(1)
    blk = SWA_BLOCK
    hd = SWA_HEAD_DIM
    kvd = SWA_KV_HEADS * hd

    @pl.when((pl.program_id(0) == 0) & (n == 0))
    def _():
        bmap = bmap_ref[...]
        for hq in range(SWA_Q_HEADS):
            bias_scr[0, hq] = jnp.where(bmap < 0, NEG_INF, 0.0)

        def bucket_body(b, carry):
            hit = bmap == b
            for hq in range(SWA_Q_HEADS):
                bias_scr[0, hq] = jnp.where(hit, table_ref[b, hq], bias_scr[0, hq])
            return carry

        lax.fori_loop(0, REL_BUCKETS, bucket_body, 0)
        key_col = lax.broadcasted_iota(jnp.int32, (blk, 2 * blk), 1)
        for hq in range(SWA_Q_HEADS):
            bias_scr[1, hq] = jnp.where(key_col >= blk, bias_scr[0, hq], NEG_INF)
        for hq in range(SWA_Q_HEADS):
            for tbl in range(2):
                bias_scr[tbl, hq] = jnp.where(key_col == 0, sinks_ref[hq], bias_scr[tbl, hq])

    first = (n == 0).astype(jnp.int32)
    kv_band = jnp.concatenate([kvp_ref[...], kvc_ref[...]], axis=0)
    lane = lax.broadcasted_iota(jnp.int32, (1, LANES), 1)
    lo = lane < hd
    ones_blk = jnp.ones((2 * blk, LANES), BF16)
    not_slot0 = lax.broadcasted_iota(jnp.int32, (2 * blk, 1), 0) > 0
    kv_heads = range(SWA_KV_HEADS)
    k_dup, v_aug, q_stack = [], [], []
    for h in kv_heads:
        grp = (h * hd) // LANES
        k_grp = kv_band[:, grp * LANES:(grp + 1) * LANES]
        v_grp = kv_band[:, kvd + grp * LANES:kvd + (grp + 1) * LANES]
        mine = (lo if (h * hd) % LANES == 0 else ~lo) & not_slot0
        k_own = jnp.where(mine, k_grp, 0.0)
        v_own = jnp.where(mine, v_grp, 0.0)
        k_dup.append((k_own + pltpu.roll(k_own, hd, 1)).astype(BF16))
        v_aug.append(jnp.concatenate(
            [(v_own + pltpu.roll(v_own, hd, 1)).astype(BF16), ones_blk], axis=1))
        q_rows = []
        for gidx in range(SWA_GROUP):
            hq = h * SWA_GROUP + gidx
            qg = q_ref[:, (hq * hd) // LANES * LANES:((hq * hd) // LANES + 1) * LANES]
            sel = lo if (hq * hd) % LANES == 0 else ~lo
            q_rows.append(jnp.where(sel, qg * (hd ** -0.5), 0.0))
        q_stack.append(jnp.concatenate(q_rows, axis=0).astype(BF16))
    s = [_dot_nt(q_stack[h], k_dup[h]) for h in kv_heads]
    sb = [s[h] + bias_scr[first, pl.ds(h * SWA_GROUP, SWA_GROUP)].reshape(SWA_GROUP * blk, 2 * blk)
          for h in kv_heads]
    mx = [jnp.max(sb[h], axis=-1, keepdims=True) for h in kv_heads]
    p = [jnp.exp(sb[h] - mx[h]).astype(BF16) for h in kv_heads]
    oa = [jnp.dot(p[h], v_aug[h], preferred_element_type=F32) for h in kv_heads]
    outs = [oa[h][:, :LANES] / oa[h][:, LANES:] for h in kv_heads]
    for h in kv_heads:
        for pair in range(SWA_GROUP // 2):
            hq0 = h * SWA_GROUP + 2 * pair
            col = (hq0 * hd) // LANES * LANES
            even = outs[h][(2 * pair) * blk:(2 * pair + 1) * blk]
            odd = outs[h][(2 * pair + 1) * blk:(2 * pair + 2) * blk]
            out_ref[:, col:col + LANES] = jnp.where(lo, even, odd).astype(out_ref.dtype)


def _swa(q, kv, rel_table, sinks, bsz, seq):
    m = q.shape[0]
    nb = seq // SWA_BLOCK
    bmap = jnp.asarray(_t5_bucket_map())
    grid_spec = pltpu.PrefetchScalarGridSpec(
        num_scalar_prefetch=0,
        grid=(bsz, nb),
        in_specs=[
            pl.BlockSpec(memory_space=pltpu.SMEM),
            pl.BlockSpec(memory_space=pltpu.SMEM),
            pl.BlockSpec((SWA_BLOCK, q.shape[1]), lambda b, n: (b * nb + n, 0)),
            pl.BlockSpec((SWA_BLOCK, kv.shape[1]), lambda b, n: (b * nb + n, 0)),
            pl.BlockSpec((SWA_BLOCK, kv.shape[1]),
                         lambda b, n: (b * nb + jnp.maximum(n - 1, 0), 0)),
            _const_spec((SWA_BLOCK, 2 * SWA_BLOCK)),
        ],
        out_specs=pl.BlockSpec((SWA_BLOCK, q.shape[1]), lambda b, n: (b * nb + n, 0)),
        scratch_shapes=[pltpu.VMEM((2, SWA_Q_HEADS, SWA_BLOCK, 2 * SWA_BLOCK), F32)],
    )
    return pl.pallas_call(
        _swa_kernel,
        grid_spec=grid_spec,
        out_shape=jax.ShapeDtypeStruct((m, q.shape[1]), BF16),
        compiler_params=_params("arbitrary", "arbitrary"),
        name="swa",
    )(rel_table, sinks, q, kv, kv, bmap)


def kernel(x, a_norm_w, a_w_in, a_conv_w, a_a_log, a_dt_bias, a_out_norm_w, a_w_out,
           kv_norm_w, w_kv, b_norm_w, b_w_q, b_sinks, b_w_o, rel_bias_table,
           ffn_norm_w, ffn_w_up, ffn_conv_w, ffn_conv_b, ffn_w_down, final_norm_w):
    bsz, seq, d = x.shape
    n_a = a_w_in.shape[0]
    n_b = b_w_q.shape[0]
    depth = n_a + n_b
    h = x.reshape(bsz * seq, d)
    kv = None
    for layer in range(depth):
        if layer < n_a:
            i = layer
            w_in = a_w_in[i]
            w_ba = w_in[:, GDN_MAIN_DIM:].astype(BF16)
            qkv, z, ba, bat = _in_proj(h, a_norm_w[i].reshape(1, d),
                                       w_in[:, :GDN_MAIN_DIM].astype(BF16), w_ba, w_ba.T,
                                       a_conv_w[i], bsz, seq)
            o = _gdn_core(qkv, z, ba, bat, a_a_log[i], a_dt_bias[i], a_out_norm_w[i], bsz, seq)
            h = _mm_res(o, a_w_out[i].astype(BF16), h)
        else:
            j = layer - n_a
            if j == 0:
                q, kv = _qkv_proj(h, b_norm_w[j].reshape(1, d), kv_norm_w.reshape(1, d),
                                  b_w_q[j].astype(BF16), w_kv.astype(BF16))
            else:
                q, _ = _qkv_proj(h, b_norm_w[j].reshape(1, d), kv_norm_w.reshape(1, d),
                                 b_w_q[j].astype(BF16), w_kv.astype(BF16))
            o = _swa(q, kv, rel_bias_table, b_sinks[j], bsz, seq)
            h = _mm_res(o, b_w_o[j].astype(BF16), h)
        last = layer == depth - 1
        h = _ffn(h, ffn_norm_w[layer].reshape(1, d), ffn_w_up[layer].astype(BF16),
                 ffn_conv_w[layer], ffn_conv_b[layer].reshape(1, 2 * D_FF),
                 ffn_w_down[layer].astype(BF16), final_norm_w.reshape(1, d), bsz, seq, last)
    return h.reshape(bsz, seq, d)
```

```python
import functools
import math

import jax
import jax.numpy as jnp
import numpy as np
from jax import lax
from jax.experimental import pallas as pl
from jax.experimental.pallas import tpu as pltpu

F32 = jnp.float32
BF16 = jnp.bfloat16

EPS = 1e-6
NEG_INF = -1e30

D_MODEL = 1024
GDN_QK_HEADS = 8
GDN_V_HEADS = 16
GDN_HEAD_DIM = 128
GDN_QK_DIM = GDN_QK_HEADS * GDN_HEAD_DIM
GDN_V_DIM = GDN_V_HEADS * GDN_HEAD_DIM
GDN_CONV_DIM = 2 * GDN_QK_DIM + GDN_V_DIM
GDN_MAIN_DIM = GDN_CONV_DIM + GDN_V_DIM
GDN_CONV_WIDTH = 4
GDN_CHUNK = 64

SWA_Q_HEADS = 16
SWA_KV_HEADS = 4
SWA_GROUP = SWA_Q_HEADS // SWA_KV_HEADS
SWA_HEAD_DIM = 64
SWA_WINDOW = 128
SWA_BLOCK = 128
REL_BUCKETS = 32
REL_MAX_DISTANCE = 128

D_FF = 2816
FFN_CONV_WIDTH = 3
FFN_CHUNK = 256
CONV_ROWS = 64
PREP_CHUNKS = 4

SUBLANES = 8
LANES = 128
VMEM_LIMIT_BYTES = 56 * 1024 * 1024

TM_PROJ = 256
TM_GDN = 256
TM_FFN = 256
TM_MM = 512


def _params(*sem):
    return pltpu.CompilerParams(dimension_semantics=sem, vmem_limit_bytes=VMEM_LIMIT_BYTES)


def _const_spec(shape):
    zeros = (0,) * len(shape)
    return pl.BlockSpec(shape, lambda *_: zeros)


def _dot(a, b):
    return jnp.dot(a.astype(BF16), b.astype(BF16), preferred_element_type=F32)


def _dot_nt(a, b):
    return lax.dot_general(a.astype(BF16), b.astype(BF16), (((1,), (1,)), ((), ())),
                           preferred_element_type=F32)


def _rms_scale(x):
    return x * lax.rsqrt(jnp.mean(x * x, axis=-1, keepdims=True) + EPS)


def _silu(x):
    half = 0.5 * x
    return half + half * jnp.tanh(half)


def _shift_rows(cur, prev8, j):
    if j == 0:
        return cur
    r = pltpu.roll(cur, j, 0)
    pr = pltpu.roll(prev8, j, 0)
    row = lax.broadcasted_iota(jnp.int32, pr.shape, 0)
    top = jnp.where(row < j, pr, r[:SUBLANES])
    return jnp.concatenate([top, r[SUBLANES:]], axis=0)


def _split3(x):
    hi = x.astype(BF16)
    r1 = x - hi.astype(F32)
    mid = r1.astype(BF16)
    lo = (r1 - mid.astype(F32)).astype(BF16)
    return hi, mid, lo


def _in_proj_kernel(x_ref, nw_ref, wm_ref, wba_ref, wbat_ref, cw_ref,
                    qkv_ref, z_ref, ba_ref, bat_ref, carry_scr):
    t = pl.program_id(1)

    @pl.when(t == 0)
    def _():
        carry_scr[...] = jnp.zeros_like(carry_scr)

    tm = x_ref.shape[0]
    hd = GDN_HEAD_DIM
    hn = (_rms_scale(x_ref[...]) * nw_ref[...]).astype(BF16)
    step = 2 * hd
    zevery = GDN_CONV_DIM // GDN_V_DIM
    for c0 in range(0, GDN_CONV_DIM, step):
        if (c0 // step) % zevery == 0:
            z0 = c0 // zevery
            z_ref[:, z0:z0 + step] = jnp.dot(
                hn, wm_ref[:, GDN_CONV_DIM + z0:GDN_CONV_DIM + z0 + step], preferred_element_type=F32)
        sl = slice(c0, c0 + step)
        cur = jnp.dot(hn, wm_ref[:, sl], preferred_element_type=F32)
        prev8 = carry_scr[:, sl]
        carry_scr[:, sl] = cur[tm - SUBLANES:]
        for hc in range(c0, c0 + step, hd):
            lanes = slice(hc - c0, hc - c0 + hd)
            wts = [cw_ref[GDN_CONV_WIDTH - 1 - j:GDN_CONV_WIDTH - j, hc:hc + hd]
                   for j in range(GDN_CONV_WIDTH)]
            for r0 in range(0, tm, CONV_ROWS):
                piece = cur[r0:r0 + CONV_ROWS, lanes]
                before = prev8[:, lanes] if r0 == 0 else cur[r0 - SUBLANES:r0, lanes]
                yh = piece * wts[0]
                for j in range(1, GDN_CONV_WIDTH):
                    yh = yh + _shift_rows(piece, before, j) * wts[j]
                yh = _silu(yh)
                if hc < 2 * GDN_QK_DIM:
                    yh = yh * lax.rsqrt(jnp.sum(yh * yh, axis=-1, keepdims=True) + EPS)
                    if hc < GDN_QK_DIM:
                        yh = yh * (hd ** -0.5)
                qkv_ref[r0:r0 + CONV_ROWS, hc:hc + hd] = yh
    ba_ref[...] = jnp.dot(hn, wba_ref[...], preferred_element_type=F32)
    bat_ref[...] = _dot_nt(wbat_ref[...], hn)


def _in_proj(x2, norm_w, w_main, w_ba, w_bat, conv_w, bsz, seq):
    m, k = x2.shape
    n = w_main.shape[1]
    nb = w_ba.shape[1]
    nbt = w_bat.shape[0]
    tm = TM_PROJ
    nt = seq // tm

    def row_map(b, t):
        return (b * nt + t, 0)

    return pl.pallas_call(
        _in_proj_kernel,
        grid=(bsz, nt),
        in_specs=[
            pl.BlockSpec((tm, k), row_map),
            _const_spec((1, k)),
            _const_spec((k, n)),
            _const_spec((k, nb)),
            _const_spec((nbt, k)),
            _const_spec((GDN_CONV_WIDTH, GDN_CONV_DIM)),
        ],
        out_specs=[
            pl.BlockSpec((tm, GDN_CONV_DIM), row_map),
            pl.BlockSpec((tm, GDN_V_DIM), row_map),
            pl.BlockSpec((tm, nb), row_map),
            pl.BlockSpec((nbt, tm), lambda b, t: (0, b * nt + t)),
        ],
        out_shape=[
            jax.ShapeDtypeStruct((m, GDN_CONV_DIM), F32),
            jax.ShapeDtypeStruct((m, GDN_V_DIM), F32),
            jax.ShapeDtypeStruct((m, nb), F32),
            jax.ShapeDtypeStruct((nbt, m), F32),
        ],
        scratch_shapes=[pltpu.VMEM((SUBLANES, GDN_CONV_DIM), F32)],
        compiler_params=_params("parallel", "arbitrary"),
        name="gdn_in_proj",
    )(x2, norm_w, w_main, w_ba, w_bat, conv_w)


def _gdn_core_kernel(qkv_ref, z_ref, ba_ref, bat_ref, alog_ref, dtb_ref,
                     alogt_ref, dtbt_ref, onw_ref, out_ref,
                     s_scr, gcx_scr, betap_scr, rows_scr, u_scr, lhs_scr, a_scr, kt_scr):
    t = pl.program_id(1)
    tm = qkv_ref.shape[0]
    c = GDN_CHUNK
    nchunk = tm // c
    hd = GDN_HEAD_DIM
    nh = GDN_V_HEADS
    npair = GDN_QK_HEADS
    heads = range(nh)
    pairs = range(npair)

    @pl.when(t == 0)
    def _():
        s_scr[...] = jnp.zeros_like(s_scr)

    beta = 1.0 / (1.0 + jnp.exp(-ba_ref[:, :nh]))
    a_lin = ba_ref[:, nh:2 * nh] + dtb_ref[...]
    g = -jnp.exp(alog_ref[...]) * (jnp.maximum(a_lin, 0.0) + jnp.log1p(jnp.exp(-jnp.abs(a_lin))))
    row = lax.broadcasted_iota(jnp.int32, (tm, tm), 0)
    col = lax.broadcasted_iota(jnp.int32, (tm, tm), 1)
    same = (row // c) == (col // c)
    lblk = (same & (col <= row)).astype(BF16)
    ublk = (same & (row <= col)).astype(BF16)
    gc = jnp.dot(jnp.concatenate([lblk] * 3, axis=1), jnp.concatenate(_split3(g), axis=0),
                 preferred_element_type=F32)
    e_row = lax.broadcasted_iota(jnp.int32, (3 * nh, npair * 2 * c), 0) % nh
    e_col = lax.broadcasted_iota(jnp.int32, (3 * nh, npair * 2 * c), 1)
    expand = (e_row == e_col // c).astype(BF16)
    betap_scr[...] = jnp.dot(jnp.concatenate(_split3(beta), axis=1), expand,
                             preferred_element_type=F32)
    e_row = lax.broadcasted_iota(jnp.int32, (3 * nh, nh * hd), 0) % nh
    e_col = lax.broadcasted_iota(jnp.int32, (3 * nh, nh * hd), 1)
    expand_hd = (e_row == e_col // hd).astype(BF16)
    gcx_scr[...] = jnp.dot(jnp.concatenate(_split3(gc), axis=1), expand_hd,
                           preferred_element_type=F32)

    beta_t = 1.0 / (1.0 + jnp.exp(-bat_ref[:nh, :]))
    a_lin_t = bat_ref[nh:, :] + dtbt_ref[...]
    g_t = -jnp.exp(alogt_ref[...]) * (jnp.maximum(a_lin_t, 0.0)
                                      + jnp.log1p(jnp.exp(-jnp.abs(a_lin_t))))
    gct = jnp.dot(jnp.concatenate(_split3(g_t), axis=1), jnp.concatenate([ublk] * 3, axis=0),
                  preferred_element_type=F32)
    begct = beta_t * jnp.exp(gct)
    for ci in range(nchunk):
        for slot, arr in enumerate((gct, beta_t, begct)):
            rows_scr[slot, ci] = jnp.concatenate([arr[:npair, ci * c:(ci + 1) * c],
                                                  arr[npair:, ci * c:(ci + 1) * c]], axis=1)

    ri = lax.broadcasted_iota(jnp.int32, (c, 2 * c), 0)
    cj = lax.broadcasted_iota(jnp.int32, (c, 2 * c), 1) % c
    left = lax.broadcasted_iota(jnp.int32, (1, 2 * c), 1) < c
    tril = cj <= ri
    strict = cj < ri
    eye = (ri == cj).astype(F32)
    onw = onw_ref[...]
    zero_hd = jnp.zeros((c, hd), BF16)

    def block_diag(xc):
        xb = xc.astype(BF16)
        zero = jnp.zeros_like(xb)
        return jnp.concatenate([jnp.where(left, xb, zero), jnp.where(left, zero, xb)], axis=0)

    def two_blocks(top, bottom):
        return jnp.concatenate([jnp.concatenate([top, zero_hd], axis=1),
                                jnp.concatenate([zero_hd, bottom], axis=1)], axis=0)

    def prepare(chunk_ids):
        items = [(n, j) for n in range(len(chunk_ids)) for j in pairs]
        rows = [pl.ds(pl.multiple_of(ci * c, c), c) for ci in chunk_ids]
        q = {(n, j): qkv_ref[rows[n], j * hd:(j + 1) * hd].astype(BF16) for n, j in items}
        k = {(n, j): qkv_ref[rows[n], GDN_QK_DIM + j * hd:GDN_QK_DIM + (j + 1) * hd].astype(BF16)
             for n, j in items}
        gq = {i: _dot_nt(jnp.concatenate([k[i], q[i]], axis=0), jnp.concatenate([k[i], k[i]], axis=0))
              for i in items}
        dec = {}
        for n, j in items:
            gc_pair = jnp.where(left, gcx_scr[rows[n], 2 * j * hd:2 * j * hd + 2 * c],
                                gcx_scr[rows[n], (2 * j + 1) * hd:(2 * j + 1) * hd + 2 * c])
            diff = gc_pair - rows_scr[0, chunk_ids[n], pl.ds(j, 1), :]
            dec[n, j] = jnp.where(tril, jnp.exp(jnp.where(tril, diff, 0.0)), 0.0)
        m = {(n, j): jnp.where(strict, betap_scr[rows[n], j * 2 * c:(j + 1) * 2 * c]
                               * gq[n, j][:c] * dec[n, j], 0.0) for n, j in items}
        for n, j in items:
            a_scr[chunk_ids[n], j] = (gq[n, j][c:] * dec[n, j]).astype(BF16)
        x = {i: eye - m[i] for i in items}
        p = {i: _dot(m[i], block_diag(m[i])) for i in items}
        levels = int(math.log2(c)) - 1
        for lvl in range(levels):
            if lvl + 1 < levels:
                xp = {i: _dot(jnp.concatenate([x[i], p[i]], axis=0), block_diag(p[i])) for i in items}
                x = {i: x[i] + xp[i][:c] for i in items}
                p = {i: xp[i][c:] for i in items}
            else:
                xp = {i: _dot(x[i], block_diag(p[i])) for i in items}
                x = {i: x[i] + xp[i] for i in items}
        u, w = {}, {}
        for n, j in items:
            ci = chunk_ids[n]
            t_u = x[n, j] * rows_scr[1, ci, pl.ds(j, 1), :]
            t_w = x[n, j] * rows_scr[2, ci, pl.ds(j, 1), :]
            v_a = qkv_ref[rows[n], 2 * GDN_QK_DIM + 2 * j * hd:2 * GDN_QK_DIM + (2 * j + 1) * hd]
            v_b = qkv_ref[rows[n], 2 * GDN_QK_DIM + (2 * j + 1) * hd:2 * GDN_QK_DIM + (2 * j + 2) * hd]
            u[n, j] = _dot(t_u, two_blocks(v_a.astype(BF16), v_b.astype(BF16)))
            w[n, j] = _dot(t_w, two_blocks(k[n, j], k[n, j]))
        for n, j in items:
            ci = chunk_ids[n]
            for e in range(2):
                h = 2 * j + e
                u_scr[ci, h] = u[n, j][:, e * hd:(e + 1) * hd]
                lhs_scr[ci, h] = jnp.concatenate(
                    [w[n, j][:, e * hd:(e + 1) * hd].astype(BF16), q[n, j]], axis=0)
            if j % 2 == 0:
                k2 = qkv_ref[rows[n], GDN_QK_DIM + j * hd:GDN_QK_DIM + (j + 2) * hd]
                kt_scr[ci, j // 2] = jnp.concatenate(
                    [k2[:, :hd], k2[:, hd:]], axis=0).T.astype(BF16)

    def prepare_body(i, carry):
        prepare([i * PREP_CHUNKS + n for n in range(PREP_CHUNKS)])
        return carry

    lax.fori_loop(0, nchunk // PREP_CHUNKS, prepare_body, 0)

    def state_body(ci, carry):
        r0 = pl.multiple_of(ci * c, c)
        rows = pl.ds(r0, c)
        gcx = [gcx_scr[rows, h * hd:(h + 1) * hd] for h in heads]
        egc = [jnp.exp(gcx[h]) for h in heads]
        ekd = [jnp.exp(gcx[h][c - 1:c, :] - gcx[h]) for h in heads]
        s_old = [s_scr[j] for j in pairs]
        pq = [_dot(lhs_scr[ci, h], s_old[h // 2][:, (h % 2) * hd:(h % 2 + 1) * hd])
              for h in heads]
        vn = [u_scr[ci, h] - pq[h][:c] for h in heads]
        vnb = [vn[h].astype(BF16) for h in heads]
        avn = [_dot(a_scr[ci, j], two_blocks(vnb[2 * j], vnb[2 * j + 1])) for j in pairs]
        vs = [jnp.concatenate([vn[2 * j] * ekd[2 * j], vn[2 * j + 1] * ekd[2 * j + 1]],
                              axis=1).astype(BF16) for j in pairs]
        zero_vs = jnp.zeros_like(vs[0])
        ds = [_dot(kt_scr[ci, j // 2], jnp.concatenate(
            [vs[j], zero_vs] if j % 2 == 0 else [zero_vs, vs[j]], axis=0)) for j in pairs]
        for j in pairs:
            egl = jnp.exp(jnp.concatenate([gcx[2 * j][c - 1:c, :], gcx[2 * j + 1][c - 1:c, :]], axis=1))
            s_scr[j] = s_old[j] * egl + ds[j]
        for h in heads:
            o_h = (pq[h][c:] * egc[h]
                   + avn[h // 2][:, (h % 2) * hd:(h % 2 + 1) * hd])
            z_h = z_ref[rows, h * hd:(h + 1) * hd]
            out_ref[rows, h * hd:(h + 1) * hd] = (
                _rms_scale(o_h) * onw * _silu(z_h)).astype(out_ref.dtype)
        return carry

    lax.fori_loop(0, nchunk, state_body, 0)


def _gdn_core(qkv, z, ba, bat, a_log, dt_bias, out_norm_w, head_order, bsz, seq):
    m = qkv.shape[0]
    tm = TM_GDN
    nt = seq // tm
    nh = GDN_V_HEADS
    nb = ba.shape[1]
    nchunk = tm // GDN_CHUNK

    def cur_map(b, t):
        return (b * nt + t, 0)

    return pl.pallas_call(
        _gdn_core_kernel,
        grid=(bsz, nt),
        in_specs=[
            pl.BlockSpec((tm, GDN_CONV_DIM), cur_map),
            pl.BlockSpec((tm, GDN_V_DIM), cur_map),
            pl.BlockSpec((tm, nb), cur_map),
            pl.BlockSpec((2 * nh, tm), lambda b, t: (0, b * nt + t)),
            _const_spec((1, nh)),
            _const_spec((1, nh)),
            _const_spec((nh, 1)),
            _const_spec((nh, 1)),
            _const_spec((1, GDN_HEAD_DIM)),
        ],
        out_specs=pl.BlockSpec((tm, GDN_V_DIM), cur_map),
        out_shape=jax.ShapeDtypeStruct((m, GDN_V_DIM), BF16),
        scratch_shapes=[
            pltpu.VMEM((GDN_QK_HEADS, GDN_HEAD_DIM, 2 * GDN_HEAD_DIM), F32),
            pltpu.VMEM((tm, nh * GDN_HEAD_DIM), F32),
            pltpu.VMEM((tm, nh * GDN_CHUNK), F32),
            pltpu.VMEM((3, nchunk, GDN_QK_HEADS, 2 * GDN_CHUNK), F32),
            pltpu.VMEM((nchunk, nh, GDN_CHUNK, GDN_HEAD_DIM), F32),
            pltpu.VMEM((nchunk, nh, 2 * GDN_CHUNK, GDN_HEAD_DIM), BF16),
            pltpu.VMEM((nchunk, GDN_QK_HEADS, GDN_CHUNK, 2 * GDN_CHUNK), BF16),
            pltpu.VMEM((nchunk, GDN_QK_HEADS // 2, GDN_HEAD_DIM, 2 * GDN_CHUNK), BF16),
        ],
        compiler_params=_params("parallel", "arbitrary"),
        name="gdn_core",
    )(qkv, z, ba, bat, a_log.reshape(1, nh), dt_bias.reshape(1, nh),
      a_log[head_order].reshape(nh, 1), dt_bias[head_order].reshape(nh, 1),
      out_norm_w.reshape(1, GDN_HEAD_DIM))


def _mm_res_kernel(x_ref, w_ref, res_ref, out_ref):
    out_ref[...] = res_ref[...] + jnp.dot(x_ref[...].astype(BF16), w_ref[...],
                                          preferred_element_type=F32)


def _mm_res(x2, w, res):
    m, k = x2.shape
    n = w.shape[1]
    tm = TM_MM
    return pl.pallas_call(
        _mm_res_kernel,
        grid=(m // tm,),
        in_specs=[
            pl.BlockSpec((tm, k), lambda i: (i, 0)),
            _const_spec((k, n)),
            pl.BlockSpec((tm, n), lambda i: (i, 0)),
        ],
        out_specs=pl.BlockSpec((tm, n), lambda i: (i, 0)),
        out_shape=jax.ShapeDtypeStruct((m, n), F32),
        compiler_params=_params("parallel"),
        name="mm_res",
    )(x2, w, res)


def _ffn_kernel(h_ref, nw_ref, wup_ref, cw_ref, cb_ref, wdn_ref, fnw_ref, out_ref,
                carry_scr, act_scr, *, final_norm):
    t = pl.program_id(1)

    @pl.when(t == 0)
    def _():
        carry_scr[...] = jnp.zeros_like(carry_scr)

    h = h_ref[...]
    tm = h.shape[0]
    hn = (_rms_scale(h) * nw_ref[...]).astype(BF16)
    for c0 in range(0, D_FF, FFN_CHUNK):
        halves = []
        for off in (0, D_FF):
            sl = slice(off + c0, off + c0 + FFN_CHUNK)
            cur = jnp.dot(hn, wup_ref[:, sl], preferred_element_type=F32)
            prev8 = carry_scr[:, sl]
            carry_scr[:, sl] = cur[tm - SUBLANES:]
            y = cur * cw_ref[FFN_CONV_WIDTH - 1:FFN_CONV_WIDTH, sl] + cb_ref[:, sl]
            for j in range(1, FFN_CONV_WIDTH):
                y = y + (_shift_rows(cur, prev8, j)
                         * cw_ref[FFN_CONV_WIDTH - 1 - j:FFN_CONV_WIDTH - j, sl])
            halves.append(y)
        act_scr[:, c0:c0 + FFN_CHUNK] = (_silu(halves[0]) * halves[1]).astype(BF16)
    res = h + jnp.dot(act_scr[...], wdn_ref[...], preferred_element_type=F32)
    if final_norm:
        res = _rms_scale(res) * fnw_ref[...]
    out_ref[...] = res


def _ffn(h2, norm_w, w_up, conv_w, conv_b, w_down, final_w, bsz, seq, final_norm):
    m, k = h2.shape
    tm = TM_FFN
    nt = seq // tm

    def row_map(b, t):
        return (b * nt + t, 0)

    return pl.pallas_call(
        functools.partial(_ffn_kernel, final_norm=final_norm),
        grid=(bsz, nt),
        in_specs=[
            pl.BlockSpec((tm, k), row_map),
            _const_spec((1, k)),
            _const_spec((k, 2 * D_FF)),
            _const_spec((FFN_CONV_WIDTH, 2 * D_FF)),
            _const_spec((1, 2 * D_FF)),
            _const_spec((D_FF, k)),
            _const_spec((1, k)),
        ],
        out_specs=pl.BlockSpec((tm, k), row_map),
        out_shape=jax.ShapeDtypeStruct((m, k), F32),
        scratch_shapes=[
            pltpu.VMEM((SUBLANES, 2 * D_FF), F32),
            pltpu.VMEM((tm, D_FF), BF16),
        ],
        compiler_params=_params("parallel", "arbitrary"),
        name="ffn_final" if final_norm else "ffn",
    )(h2, norm_w, w_up, conv_w, conv_b, w_down, final_w)


def _qkv_proj_kernel(h_ref, qnw_ref, kvnw_ref, wq_ref, wkv_ref, q_ref, kv_ref):
    y = _rms_scale(h_ref[...])
    q_ref[...] = jnp.dot((y * qnw_ref[...]).astype(BF16), wq_ref[...],
                         preferred_element_type=F32)
    kv_ref[...] = jnp.dot((y * kvnw_ref[...]).astype(BF16), wkv_ref[...],
                          preferred_element_type=F32)


def _qkv_proj(h2, q_norm_w, kv_norm_w, w_q, w_kv):
    m, k = h2.shape
    nq = w_q.shape[1]
    nkv = w_kv.shape[1]
    tm = TM_MM
    return pl.pallas_call(
        _qkv_proj_kernel,
        grid=(m // tm,),
        in_specs=[
            pl.BlockSpec((tm, k), lambda i: (i, 0)),
            _const_spec((1, k)),
            _const_spec((1, k)),
            _const_spec((k, nq)),
            _const_spec((k, nkv)),
        ],
        out_specs=[
            pl.BlockSpec((tm, nq), lambda i: (i, 0)),
            pl.BlockSpec((tm, nkv), lambda i: (i, 0)),
        ],
        out_shape=[
            jax.ShapeDtypeStruct((m, nq), F32),
            jax.ShapeDtypeStruct((m, nkv), F32),
        ],
        compiler_params=_params("parallel"),
        name="qkv_proj",
    )(h2, q_norm_w, kv_norm_w, w_q, w_kv)


def _t5_bucket_map():
    qi = np.arange(SWA_BLOCK)[:, None]
    ki = np.arange(2 * SWA_BLOCK)[None, :]
    dist = qi + SWA_BLOCK - ki
    n = np.maximum(dist, 0)
    max_exact = REL_BUCKETS // 2
    nf = np.maximum(n, 1).astype(np.float32)
    large = max_exact + (np.log(nf / np.float32(max_exact)).astype(np.float32)
                         / np.float32(math.log(REL_MAX_DISTANCE / max_exact))
                         * np.float32(REL_BUCKETS - max_exact)).astype(np.int32)
    large = np.minimum(large, REL_BUCKETS - 1)
    bucket = np.where(n < max_exact, n, large).astype(np.int32)
    in_window = (dist >= 0) & (dist < SWA_WINDOW)
    return np.where(in_window, bucket, -1).astype(np.int32)


def _swa_kernel(table_ref, sinks_ref, q_ref, kvc_ref, kvp_ref, bmap_ref, out_ref, bias_scr):
    n = pl.program_id(1)
    blk = SWA_BLOCK
    hd = SWA_HEAD_DIM
    kvd = SWA_KV_HEADS * hd

    @pl.when((pl.program_id(0) == 0) & (n == 0))
    def _():
        bmap = bmap_ref[...]
        for hq in range(SWA_Q_HEADS):
            bias_scr[0, hq] = jnp.where(bmap < 0, NEG_INF, 0.0)

        def bucket_body(b, carry):
            hit = bmap == b
            for hq in range(SWA_Q_HEADS):
                bias_scr[0, hq] = jnp.where(hit, table_ref[b, hq], bias_scr[0, hq])
            return carry

        lax.fori_loop(0, REL_BUCKETS, bucket_body, 0)
        key_col = lax.broadcasted_iota(jnp.int32, (blk, 2 * blk), 1)
        for hq in range(SWA_Q_HEADS):
            bias_scr[1, hq] = jnp.where(key_col >= blk, bias_scr[0, hq], NEG_INF)
        for hq in range(SWA_Q_HEADS):
            for tbl in range(2):
                bias_scr[tbl, hq] = jnp.where(key_col == 0, sinks_ref[hq], bias_scr[tbl, hq])

    first = (n == 0).astype(jnp.int32)
    kv_band = jnp.concatenate([kvp_ref[...], kvc_ref[...]], axis=0)
    lane = lax.broadcasted_iota(jnp.int32, (1, LANES), 1)
    lo = lane < hd
    ones_blk = jnp.ones((2 * blk, LANES), BF16)
    not_slot0 = lax.broadcasted_iota(jnp.int32, (2 * blk, 1), 0) > 0
    kv_heads = range(SWA_KV_HEADS)
    k_dup, v_aug, q_stack = [], [], []
    for h in kv_heads:
        grp = (h * hd) // LANES
        k_grp = kv_band[:, grp * LANES:(grp + 1) * LANES]
        v_grp = kv_band[:, kvd + grp * LANES:kvd + (grp + 1) * LANES]
        mine = (lo if (h * hd) % LANES == 0 else ~lo) & not_slot0
        k_own = jnp.where(mine, k_grp, 0.0)
        v_own = jnp.where(mine, v_grp, 0.0)
        k_dup.append((k_own + pltpu.roll(k_own, hd, 1)).astype(BF16))
        v_aug.append(jnp.concatenate(
            [(v_own + pltpu.roll(v_own, hd, 1)).astype(BF16), ones_blk], axis=1))
        q_rows = []
        for gidx in range(SWA_GROUP):
            hq = h * SWA_GROUP + gidx
            qg = q_ref[:, (hq * hd) // LANES * LANES:((hq * hd) // LANES + 1) * LANES]
            sel = lo if (hq * hd) % LANES == 0 else ~lo
            q_rows.append(jnp.where(sel, qg * (hd ** -0.5), 0.0))
        q_stack.append(jnp.concatenate(q_rows, axis=0).astype(BF16))
    s = [_dot_nt(q_stack[h], k_dup[h]) for h in kv_heads]
    sb = [s[h] + bias_scr[first, pl.ds(h * SWA_GROUP, SWA_GROUP)].reshape(SWA_GROUP * blk, 2 * blk)
          for h in kv_heads]
    mx = [jnp.max(sb[h], axis=-1, keepdims=True) for h in kv_heads]
    p = [jnp.exp(sb[h] - mx[h]).astype(BF16) for h in kv_heads]
    oa = [jnp.dot(p[h], v_aug[h], preferred_element_type=F32) for h in kv_heads]
    outs = [oa[h][:, :LANES] / oa[h][:, LANES:] for h in kv_heads]
    for h in kv_heads:
        for pair in range(SWA_GROUP // 2):
            hq0 = h * SWA_GROUP + 2 * pair
            col = (hq0 * hd) // LANES * LANES
            even = outs[h][(2 * pair) * blk:(2 * pair + 1) * blk]
            odd = outs[h][(2 * pair + 1) * blk:(2 * pair + 2) * blk]
            out_ref[:, col:col + LANES] = jnp.where(lo, even, odd).astype(out_ref.dtype)


def _swa(q, kv, rel_table, sinks, bsz, seq):
    m = q.shape[0]
    nb = seq // SWA_BLOCK
    bmap = jnp.asarray(_t5_bucket_map())
    grid_spec = pltpu.PrefetchScalarGridSpec(
        num_scalar_prefetch=0,
        grid=(bsz, nb),
        in_specs=[
            pl.BlockSpec(memory_space=pltpu.SMEM),
            pl.BlockSpec(memory_space=pltpu.SMEM),
            pl.BlockSpec((SWA_BLOCK, q.shape[1]), lambda b, n: (b * nb + n, 0)),
            pl.BlockSpec((SWA_BLOCK, kv.shape[1]), lambda b, n: (b * nb + n, 0)),
            pl.BlockSpec((SWA_BLOCK, kv.shape[1]),
                         lambda b, n: (b * nb + jnp.maximum(n - 1, 0), 0)),
            _const_spec((SWA_BLOCK, 2 * SWA_BLOCK)),
        ],
        out_specs=pl.BlockSpec((SWA_BLOCK, q.shape[1]), lambda b, n: (b * nb + n, 0)),
        scratch_shapes=[pltpu.VMEM((2, SWA_Q_HEADS, SWA_BLOCK, 2 * SWA_BLOCK), F32)],
    )
    return pl.pallas_call(
        _swa_kernel,
        grid_spec=grid_spec,
        out_shape=jax.ShapeDtypeStruct((m, q.shape[1]), BF16),
        compiler_params=_params("arbitrary", "arbitrary"),
        name="swa",
    )(rel_table, sinks, q, kv, kv, bmap)


def kernel(x, a_norm_w, a_w_in, a_conv_w, a_a_log, a_dt_bias, a_out_norm_w, a_w_out,
           kv_norm_w, w_kv, b_norm_w, b_w_q, b_sinks, b_w_o, rel_bias_table,
           ffn_norm_w, ffn_w_up, ffn_conv_w, ffn_conv_b, ffn_w_down, final_norm_w):
    bsz, seq, d = x.shape
    n_a = a_w_in.shape[0]
    n_b = b_w_q.shape[0]
    depth = n_a + n_b
    h = x.reshape(bsz * seq, d)
    kv = None
    head_order = np.concatenate([np.arange(0, GDN_V_HEADS, 2), np.arange(1, GDN_V_HEADS, 2)])
    for layer in range(depth):
        if layer < n_a:
            i = layer
            w_in = a_w_in[i]
            w_ba = w_in[:, GDN_MAIN_DIM:].astype(BF16)
            w_at = w_ba[:, np.concatenate([head_order, GDN_V_HEADS + head_order])].T
            qkv, z, ba, bat = _in_proj(h, a_norm_w[i].reshape(1, d),
                                       w_in[:, :GDN_MAIN_DIM].astype(BF16), w_ba, w_at,
                                       a_conv_w[i], bsz, seq)
            o = _gdn_core(qkv, z, ba, bat, a_a_log[i], a_dt_bias[i], a_out_norm_w[i],
                          head_order, bsz, seq)
            h = _mm_res(o, a_w_out[i].astype(BF16), h)
        else:
            j = layer - n_a
            if j == 0:
                q, kv = _qkv_proj(h, b_norm_w[j].reshape(1, d), kv_norm_w.reshape(1, d),
                                  b_w_q[j].astype(BF16), w_kv.astype(BF16))
            else:
                q, _ = _qkv_proj(h, b_norm_w[j].reshape(1, d), kv_norm_w.reshape(1, d),
                                 b_w_q[j].astype(BF16), w_kv.astype(BF16))
            o = _swa(q, kv, rel_bias_table, b_sinks[j], bsz, seq)
            h = _mm_res(o, b_w_o[j].astype(BF16), h)
        last = layer == depth - 1
        h = _ffn(h, ffn_norm_w[layer].reshape(1, d), ffn_w_up[layer].astype(BF16),
                 ffn_conv_w[layer], ffn_conv_b[layer].reshape(1, 2 * D_FF),
                 ffn_w_down[layer].astype(BF16), final_norm_w.reshape(1, d), bsz, seq, last)
    return h.reshape(bsz, seq, d)
```

```python
import functools
import math

import jax
import jax.numpy as jnp
import numpy as np
from jax import lax
from jax.experimental import pallas as pl
from jax.experimental.pallas import tpu as pltpu

F32 = jnp.float32
BF16 = jnp.bfloat16

EPS = 1e-6
NEG_INF = -1e30

D_MODEL = 1024
GDN_QK_HEADS = 8
GDN_V_HEADS = 16
GDN_HEAD_DIM = 128
GDN_QK_DIM = GDN_QK_HEADS * GDN_HEAD_DIM
GDN_V_DIM = GDN_V_HEADS * GDN_HEAD_DIM
GDN_CONV_DIM = 2 * GDN_QK_DIM + GDN_V_DIM
GDN_MAIN_DIM = GDN_CONV_DIM + GDN_V_DIM
GDN_CONV_WIDTH = 4
GDN_CHUNK = 64

SWA_Q_HEADS = 16
SWA_KV_HEADS = 4
SWA_GROUP = SWA_Q_HEADS // SWA_KV_HEADS
SWA_HEAD_DIM = 64
SWA_WINDOW = 128
SWA_BLOCK = 128
REL_BUCKETS = 32
REL_MAX_DISTANCE = 128

D_FF = 2816
FFN_CONV_WIDTH = 3
FFN_CHUNK = 256
CONV_ROWS = 64
PREP_CHUNKS = 4

SUBLANES = 8
LANES = 128
VMEM_LIMIT_BYTES = 56 * 1024 * 1024

TM_PROJ = 256
TM_GDN = 256
TM_FFN = 256
TM_MM = 512


def _params(*sem):
    return pltpu.CompilerParams(dimension_semantics=sem, vmem_limit_bytes=VMEM_LIMIT_BYTES)


def _const_spec(shape):
    zeros = (0,) * len(shape)
    return pl.BlockSpec(shape, lambda *_: zeros)


def _dot(a, b):
    return jnp.dot(a.astype(BF16), b.astype(BF16), preferred_element_type=F32)


def _dot_nt(a, b):
    return lax.dot_general(a.astype(BF16), b.astype(BF16), (((1,), (1,)), ((), ())),
                           preferred_element_type=F32)


def _rms_scale(x):
    return x * lax.rsqrt(jnp.mean(x * x, axis=-1, keepdims=True) + EPS)


def _silu(x):
    half = 0.5 * x
    return half + half * jnp.tanh(half)


def _shift_rows(cur, prev8, j):
    if j == 0:
        return cur
    nrow, ncol = cur.shape
    tiles = jnp.concatenate([prev8, cur], axis=0).reshape(nrow // SUBLANES + 1, SUBLANES, ncol)
    rot = pltpu.roll(tiles, j, 1)
    row = lax.broadcasted_iota(jnp.int32, (1, SUBLANES, ncol), 1)
    return jnp.where(row < j, rot[:-1], rot[1:]).reshape(nrow, ncol)


def _split3(x):
    hi = x.astype(BF16)
    r1 = x - hi.astype(F32)
    mid = r1.astype(BF16)
    lo = (r1 - mid.astype(F32)).astype(BF16)
    return hi, mid, lo


def _in_proj_kernel(x_ref, nw_ref, wm_ref, wba_ref, wbat_ref, cw_ref,
                    qkv_ref, z_ref, ba_ref, bat_ref, carry_scr):
    t = pl.program_id(1)

    @pl.when(t == 0)
    def _():
        carry_scr[...] = jnp.zeros_like(carry_scr)

    tm = x_ref.shape[0]
    hd = GDN_HEAD_DIM
    hn = (_rms_scale(x_ref[...]) * nw_ref[...]).astype(BF16)
    step = 2 * hd
    zevery = GDN_CONV_DIM // GDN_V_DIM
    for c0 in range(0, GDN_CONV_DIM, step):
        if (c0 // step) % zevery == 0:
            z0 = c0 // zevery
            z_ref[:, z0:z0 + step] = jnp.dot(
                hn, wm_ref[(GDN_CONV_DIM + z0) // step], preferred_element_type=F32)
        sl = slice(c0, c0 + step)
        cur = jnp.dot(hn, wm_ref[c0 // step], preferred_element_type=F32)
        prev8 = carry_scr[:, sl]
        carry_scr[:, sl] = cur[tm - SUBLANES:]
        for hc in range(c0, c0 + step, hd):
            lanes = slice(hc - c0, hc - c0 + hd)
            wts = [cw_ref[GDN_CONV_WIDTH - 1 - j:GDN_CONV_WIDTH - j, hc:hc + hd]
                   for j in range(GDN_CONV_WIDTH)]
            for r0 in range(0, tm, CONV_ROWS):
                piece = cur[r0:r0 + CONV_ROWS, lanes]
                before = prev8[:, lanes] if r0 == 0 else cur[r0 - SUBLANES:r0, lanes]
                yh = piece * wts[0]
                for j in range(1, GDN_CONV_WIDTH):
                    yh = yh + _shift_rows(piece, before, j) * wts[j]
                yh = _silu(yh)
                if hc < 2 * GDN_QK_DIM:
                    yh = yh * lax.rsqrt(jnp.sum(yh * yh, axis=-1, keepdims=True) + EPS)
                    if hc < GDN_QK_DIM:
                        yh = yh * (hd ** -0.5)
                qkv_ref[r0:r0 + CONV_ROWS, hc:hc + hd] = yh.astype(qkv_ref.dtype)
    ba_ref[...] = jnp.dot(hn, wba_ref[...], preferred_element_type=F32)
    bat_ref[...] = _dot_nt(wbat_ref[...], hn)


def _in_proj(x2, norm_w, w_main, w_ba, w_bat, conv_w, bsz, seq):
    m, k = x2.shape
    n = w_main.shape[1]
    w_main = w_main.reshape(k, n // (2 * GDN_HEAD_DIM), 2 * GDN_HEAD_DIM).transpose(1, 0, 2)
    nb = w_ba.shape[1]
    nbt = w_bat.shape[0]
    tm = TM_PROJ
    nt = seq // tm

    def row_map(b, t):
        return (b * nt + t, 0)

    return pl.pallas_call(
        _in_proj_kernel,
        grid=(bsz, nt),
        in_specs=[
            pl.BlockSpec((tm, k), row_map),
            _const_spec((1, k)),
            _const_spec(w_main.shape),
            _const_spec((k, nb)),
            _const_spec((nbt, k)),
            _const_spec((GDN_CONV_WIDTH, GDN_CONV_DIM)),
        ],
        out_specs=[
            pl.BlockSpec((tm, GDN_CONV_DIM), row_map),
            pl.BlockSpec((tm, GDN_V_DIM), row_map),
            pl.BlockSpec((tm, nb), row_map),
            pl.BlockSpec((nbt, tm), lambda b, t: (0, b * nt + t)),
        ],
        out_shape=[
            jax.ShapeDtypeStruct((m, GDN_CONV_DIM), BF16),
            jax.ShapeDtypeStruct((m, GDN_V_DIM), F32),
            jax.ShapeDtypeStruct((m, nb), F32),
            jax.ShapeDtypeStruct((nbt, m), F32),
        ],
        scratch_shapes=[pltpu.VMEM((SUBLANES, GDN_CONV_DIM), F32)],
        compiler_params=_params("parallel", "arbitrary"),
        name="gdn_in_proj",
    )(x2, norm_w, w_main, w_ba, w_bat, conv_w)


def _gdn_core_kernel(qkv_ref, z_ref, ba_ref, bat_ref, alog_ref, dtb_ref,
                     alogt_ref, dtbt_ref, onw_ref, out_ref,
                     s_scr, gcx_scr, betap_scr, rows_scr, u_scr, lhs_scr, a_scr, kt_scr):
    t = pl.program_id(1)
    tm = qkv_ref.shape[0]
    c = GDN_CHUNK
    nchunk = tm // c
    hd = GDN_HEAD_DIM
    nh = GDN_V_HEADS
    npair = GDN_QK_HEADS
    heads = range(nh)
    pairs = range(npair)

    @pl.when(t == 0)
    def _():
        s_scr[...] = jnp.zeros_like(s_scr)

    beta = 1.0 / (1.0 + jnp.exp(-ba_ref[:, :nh]))
    a_lin = ba_ref[:, nh:2 * nh] + dtb_ref[...]
    g = -jnp.exp(alog_ref[...]) * (jnp.maximum(a_lin, 0.0) + jnp.log1p(jnp.exp(-jnp.abs(a_lin))))
    row = lax.broadcasted_iota(jnp.int32, (tm, tm), 0)
    col = lax.broadcasted_iota(jnp.int32, (tm, tm), 1)
    same = (row // c) == (col // c)
    lblk = (same & (col <= row)).astype(BF16)
    ublk = (same & (row <= col)).astype(BF16)
    gc = jnp.dot(jnp.concatenate([lblk] * 3, axis=1), jnp.concatenate(_split3(g), axis=0),
                 preferred_element_type=F32)
    e_row = lax.broadcasted_iota(jnp.int32, (3 * nh, npair * 2 * c), 0) % nh
    e_col = lax.broadcasted_iota(jnp.int32, (3 * nh, npair * 2 * c), 1)
    expand = (e_row == e_col // c).astype(BF16)
    betap_scr[...] = jnp.dot(jnp.concatenate(_split3(beta), axis=1), expand,
                             preferred_element_type=F32)
    e_row = lax.broadcasted_iota(jnp.int32, (3 * nh, nh * hd), 0) % nh
    e_col = lax.broadcasted_iota(jnp.int32, (3 * nh, nh * hd), 1)
    expand_hd = (e_row == e_col // hd).astype(BF16)
    gcx_scr[...] = jnp.dot(jnp.concatenate(_split3(gc), axis=1), expand_hd,
                           preferred_element_type=F32)

    beta_t = 1.0 / (1.0 + jnp.exp(-bat_ref[:nh, :]))
    a_lin_t = bat_ref[nh:, :] + dtbt_ref[...]
    g_t = -jnp.exp(alogt_ref[...]) * (jnp.maximum(a_lin_t, 0.0)
                                      + jnp.log1p(jnp.exp(-jnp.abs(a_lin_t))))
    gct = jnp.dot(jnp.concatenate(_split3(g_t), axis=1), jnp.concatenate([ublk] * 3, axis=0),
                  preferred_element_type=F32)
    begct = beta_t * jnp.exp(gct)
    for ci in range(nchunk):
        for slot, arr in enumerate((gct, beta_t, begct)):
            rows_scr[slot, ci] = jnp.concatenate([arr[:npair, ci * c:(ci + 1) * c],
                                                  arr[npair:, ci * c:(ci + 1) * c]], axis=1)

    ri = lax.broadcasted_iota(jnp.int32, (c, 2 * c), 0)
    cj = lax.broadcasted_iota(jnp.int32, (c, 2 * c), 1) % c
    left = lax.broadcasted_iota(jnp.int32, (1, 2 * c), 1) < c
    tril = cj <= ri
    strict = cj < ri
    eye = (ri == cj).astype(F32)
    onw = onw_ref[...]
    zero_hd = jnp.zeros((c, hd), BF16)

    def block_diag(xc):
        xb = xc.astype(BF16)
        zero = jnp.zeros_like(xb)
        return jnp.concatenate([jnp.where(left, xb, zero), jnp.where(left, zero, xb)], axis=0)

    def two_blocks(top, bottom):
        return jnp.concatenate([jnp.concatenate([top, zero_hd], axis=1),
                                jnp.concatenate([zero_hd, bottom], axis=1)], axis=0)

    def prepare(chunk_ids):
        items = [(n, j) for n in range(len(chunk_ids)) for j in pairs]
        rows = [pl.ds(pl.multiple_of(ci * c, c), c) for ci in chunk_ids]
        q = {(n, j): qkv_ref[rows[n], j * hd:(j + 1) * hd] for n, j in items}
        k = {(n, j): qkv_ref[rows[n], GDN_QK_DIM + j * hd:GDN_QK_DIM + (j + 1) * hd]
             for n, j in items}
        gq = {i: _dot_nt(jnp.concatenate([k[i], q[i]], axis=0), jnp.concatenate([k[i], k[i]], axis=0))
              for i in items}
        dec = {}
        for n, j in items:
            gc_pair = jnp.where(left, gcx_scr[rows[n], 2 * j * hd:2 * j * hd + 2 * c],
                                gcx_scr[rows[n], (2 * j + 1) * hd:(2 * j + 1) * hd + 2 * c])
            diff = gc_pair - rows_scr[0, chunk_ids[n], pl.ds(j, 1), :]
            dec[n, j] = jnp.where(tril, jnp.exp(jnp.where(tril, diff, 0.0)), 0.0)
        m = {(n, j): jnp.where(strict, betap_scr[rows[n], j * 2 * c:(j + 1) * 2 * c]
                               * gq[n, j][:c] * dec[n, j], 0.0) for n, j in items}
        for n, j in items:
            a_scr[chunk_ids[n], j] = (gq[n, j][c:] * dec[n, j]).astype(BF16)
        x = {i: eye - m[i] for i in items}
        p = {i: _dot(m[i], block_diag(m[i])) for i in items}
        levels = int(math.log2(c)) - 1
        for lvl in range(levels):
            if lvl + 1 < levels:
                xp = {i: _dot(jnp.concatenate([x[i], p[i]], axis=0), block_diag(p[i])) for i in items}
                x = {i: x[i] + xp[i][:c] for i in items}
                p = {i: xp[i][c:] for i in items}
            else:
                xp = {i: _dot(x[i], block_diag(p[i])) for i in items}
                x = {i: x[i] + xp[i] for i in items}
        u, w = {}, {}
        for n, j in items:
            ci = chunk_ids[n]
            t_u = x[n, j] * rows_scr[1, ci, pl.ds(j, 1), :]
            t_w = x[n, j] * rows_scr[2, ci, pl.ds(j, 1), :]
            v_a = qkv_ref[rows[n], 2 * GDN_QK_DIM + 2 * j * hd:2 * GDN_QK_DIM + (2 * j + 1) * hd]
            v_b = qkv_ref[rows[n], 2 * GDN_QK_DIM + (2 * j + 1) * hd:2 * GDN_QK_DIM + (2 * j + 2) * hd]
            u[n, j] = _dot(t_u, two_blocks(v_a, v_b))
            w[n, j] = _dot(t_w, two_blocks(k[n, j], k[n, j]))
        for n, j in items:
            ci = chunk_ids[n]
            for e in range(2):
                h = 2 * j + e
                u_scr[ci, h] = u[n, j][:, e * hd:(e + 1) * hd]
                lhs_scr[ci, h] = jnp.concatenate(
                    [w[n, j][:, e * hd:(e + 1) * hd].astype(BF16), q[n, j]], axis=0)
            if j % 2 == 0:
                k2 = qkv_ref[rows[n], GDN_QK_DIM + j * hd:GDN_QK_DIM + (j + 2) * hd]
                kt_scr[ci, j // 2] = jnp.concatenate(
                    [k2[:, :hd], k2[:, hd:]], axis=0).astype(F32).T.astype(BF16)

    def prepare_body(i, carry):
        prepare([i * PREP_CHUNKS + n for n in range(PREP_CHUNKS)])
        return carry

    lax.fori_loop(0, nchunk // PREP_CHUNKS, prepare_body, 0)

    def state_body(ci, carry):
        r0 = pl.multiple_of(ci * c, c)
        rows = pl.ds(r0, c)
        gcx = [gcx_scr[rows, h * hd:(h + 1) * hd] for h in heads]
        egc = [jnp.exp(gcx[h]) for h in heads]
        ekd = [jnp.exp(gcx[h][c - 1:c, :] - gcx[h]) for h in heads]
        s_old = [s_scr[j] for j in pairs]
        pq = [_dot(lhs_scr[ci, h], s_old[h // 2][:, (h % 2) * hd:(h % 2 + 1) * hd])
              for h in heads]
        vn = [u_scr[ci, h] - pq[h][:c] for h in heads]
        vnb = [vn[h].astype(BF16) for h in heads]
        avn = [_dot(a_scr[ci, j], two_blocks(vnb[2 * j], vnb[2 * j + 1])) for j in pairs]
        vs = [jnp.concatenate([vn[2 * j] * ekd[2 * j], vn[2 * j + 1] * ekd[2 * j + 1]],
                              axis=1).astype(BF16) for j in pairs]
        zero_vs = jnp.zeros_like(vs[0])
        ds = [_dot(kt_scr[ci, j // 2], jnp.concatenate(
            [vs[j], zero_vs] if j % 2 == 0 else [zero_vs, vs[j]], axis=0)) for j in pairs]
        for j in pairs:
            egl = jnp.exp(jnp.concatenate([gcx[2 * j][c - 1:c, :], gcx[2 * j + 1][c - 1:c, :]], axis=1))
            s_scr[j] = s_old[j] * egl + ds[j]
        for h in heads:
            o_h = (pq[h][c:] * egc[h]
                   + avn[h // 2][:, (h % 2) * hd:(h % 2 + 1) * hd])
            z_h = z_ref[rows, h * hd:(h + 1) * hd]
            out_ref[rows, h * hd:(h + 1) * hd] = (
                _rms_scale(o_h) * onw * _silu(z_h)).astype(out_ref.dtype)
        return carry

    lax.fori_loop(0, nchunk, state_body, 0)


def _gdn_core(qkv, z, ba, bat, a_log, dt_bias, out_norm_w, head_order, bsz, seq):
    m = qkv.shape[0]
    tm = TM_GDN
    nt = seq // tm
    nh = GDN_V_HEADS
    nb = ba.shape[1]
    nchunk = tm // GDN_CHUNK

    def cur_map(b, t):
        return (b * nt + t, 0)

    return pl.pallas_call(
        _gdn_core_kernel,
        grid=(bsz, nt),
        in_specs=[
            pl.BlockSpec((tm, GDN_CONV_DIM), cur_map),
            pl.BlockSpec((tm, GDN_V_DIM), cur_map),
            pl.BlockSpec((tm, nb), cur_map),
            pl.BlockSpec((2 * nh, tm), lambda b, t: (0, b * nt + t)),
            _const_spec((1, nh)),
            _const_spec((1, nh)),
            _const_spec((nh, 1)),
            _const_spec((nh, 1)),
            _const_spec((1, GDN_HEAD_DIM)),
        ],
        out_specs=pl.BlockSpec((tm, GDN_V_DIM), cur_map),
        out_shape=jax.ShapeDtypeStruct((m, GDN_V_DIM), BF16),
        scratch_shapes=[
            pltpu.VMEM((GDN_QK_HEADS, GDN_HEAD_DIM, 2 * GDN_HEAD_DIM), F32),
            pltpu.VMEM((tm, nh * GDN_HEAD_DIM), F32),
            pltpu.VMEM((tm, nh * GDN_CHUNK), F32),
            pltpu.VMEM((3, nchunk, GDN_QK_HEADS, 2 * GDN_CHUNK), F32),
            pltpu.VMEM((nchunk, nh, GDN_CHUNK, GDN_HEAD_DIM), F32),
            pltpu.VMEM((nchunk, nh, 2 * GDN_CHUNK, GDN_HEAD_DIM), BF16),
            pltpu.VMEM((nchunk, GDN_QK_HEADS, GDN_CHUNK, 2 * GDN_CHUNK), BF16),
            pltpu.VMEM((nchunk, GDN_QK_HEADS // 2, GDN_HEAD_DIM, 2 * GDN_CHUNK), BF16),
        ],
        compiler_params=_params("parallel", "arbitrary"),
        name="gdn_core",
    )(qkv, z, ba, bat, a_log.reshape(1, nh), dt_bias.reshape(1, nh),
      a_log[head_order].reshape(nh, 1), dt_bias[head_order].reshape(nh, 1),
      out_norm_w.reshape(1, GDN_HEAD_DIM))


def _mm_res_kernel(x_ref, w_ref, res_ref, out_ref):
    out_ref[...] = res_ref[...] + jnp.dot(x_ref[...].astype(BF16), w_ref[...],
                                          preferred_element_type=F32)


def _mm_res(x2, w, res):
    m, k = x2.shape
    n = w.shape[1]
    tm = TM_MM
    return pl.pallas_call(
        _mm_res_kernel,
        grid=(m // tm,),
        in_specs=[
            pl.BlockSpec((tm, k), lambda i: (i, 0)),
            _const_spec((k, n)),
            pl.BlockSpec((tm, n), lambda i: (i, 0)),
        ],
        out_specs=pl.BlockSpec((tm, n), lambda i: (i, 0)),
        out_shape=jax.ShapeDtypeStruct((m, n), F32),
        compiler_params=_params("parallel"),
        name="mm_res",
    )(x2, w, res)


def _ffn_kernel(h_ref, nw_ref, wup_ref, cw_ref, cb_ref, wdn_ref, fnw_ref, out_ref,
                carry_scr, act_scr, *, final_norm):
    t = pl.program_id(1)

    @pl.when(t == 0)
    def _():
        carry_scr[...] = jnp.zeros_like(carry_scr)

    h = h_ref[...]
    tm = h.shape[0]
    hn = (_rms_scale(h) * nw_ref[...]).astype(BF16)
    for c0 in range(0, D_FF, FFN_CHUNK):
        halves = []
        for off in (0, D_FF):
            sl = slice(off + c0, off + c0 + FFN_CHUNK)
            cur = jnp.dot(hn, wup_ref[(off + c0) // FFN_CHUNK], preferred_element_type=F32)
            prev8 = carry_scr[:, sl]
            carry_scr[:, sl] = cur[tm - SUBLANES:]
            y = cur * cw_ref[FFN_CONV_WIDTH - 1:FFN_CONV_WIDTH, sl] + cb_ref[:, sl]
            for j in range(1, FFN_CONV_WIDTH):
                y = y + (_shift_rows(cur, prev8, j)
                         * cw_ref[FFN_CONV_WIDTH - 1 - j:FFN_CONV_WIDTH - j, sl])
            halves.append(y)
        act_scr[:, c0:c0 + FFN_CHUNK] = (_silu(halves[0]) * halves[1]).astype(BF16)
    act = act_scr[...]
    res = h + jnp.concatenate([jnp.dot(act, wdn_ref[n], preferred_element_type=F32)
                               for n in range(wdn_ref.shape[0])], axis=1)
    if final_norm:
        res = _rms_scale(res) * fnw_ref[...]
    out_ref[...] = res


def _ffn(h2, norm_w, w_up, conv_w, conv_b, w_down, final_w, bsz, seq, final_norm):
    m, k = h2.shape
    tm = TM_FFN
    nt = seq // tm
    w_up = w_up.reshape(k, 2 * D_FF // FFN_CHUNK, FFN_CHUNK).transpose(1, 0, 2)
    w_down = w_down.reshape(D_FF, k // FFN_CHUNK, FFN_CHUNK).transpose(1, 0, 2)

    def row_map(b, t):
        return (b * nt + t, 0)

    return pl.pallas_call(
        functools.partial(_ffn_kernel, final_norm=final_norm),
        grid=(bsz, nt),
        in_specs=[
            pl.BlockSpec((tm, k), row_map),
            _const_spec((1, k)),
            _const_spec(w_up.shape),
            _const_spec((FFN_CONV_WIDTH, 2 * D_FF)),
            _const_spec((1, 2 * D_FF)),
            _const_spec(w_down.shape),
            _const_spec((1, k)),
        ],
        out_specs=pl.BlockSpec((tm, k), row_map),
        out_shape=jax.ShapeDtypeStruct((m, k), F32),
        scratch_shapes=[
            pltpu.VMEM((SUBLANES, 2 * D_FF), F32),
            pltpu.VMEM((tm, D_FF), BF16),
        ],
        compiler_params=_params("parallel", "arbitrary"),
        name="ffn_final" if final_norm else "ffn",
    )(h2, norm_w, w_up, conv_w, conv_b, w_down, final_w)


def _qkv_proj_kernel(h_ref, qnw_ref, kvnw_ref, wq_ref, wkv_ref, q_ref, kv_ref):
    y = _rms_scale(h_ref[...])
    q = jnp.dot((y * qnw_ref[...]).astype(BF16), wq_ref[...], preferred_element_type=F32)
    q_ref[...] = (q * (SWA_HEAD_DIM ** -0.5)).astype(q_ref.dtype)
    kv_ref[...] = jnp.dot((y * kvnw_ref[...]).astype(BF16), wkv_ref[...],
                          preferred_element_type=F32).astype(kv_ref.dtype)


def _qkv_proj(h2, q_norm_w, kv_norm_w, w_q, w_kv):
    m, k = h2.shape
    nq = w_q.shape[1]
    nkv = w_kv.shape[1]
    tm = TM_MM
    return pl.pallas_call(
        _qkv_proj_kernel,
        grid=(m // tm,),
        in_specs=[
            pl.BlockSpec((tm, k), lambda i: (i, 0)),
            _const_spec((1, k)),
            _const_spec((1, k)),
            _const_spec((k, nq)),
            _const_spec((k, nkv)),
        ],
        out_specs=[
            pl.BlockSpec((tm, nq), lambda i: (i, 0)),
            pl.BlockSpec((tm, nkv), lambda i: (i, 0)),
        ],
        out_shape=[
            jax.ShapeDtypeStruct((m, nq), BF16),
            jax.ShapeDtypeStruct((m, nkv), BF16),
        ],
        compiler_params=_params("parallel"),
        name="qkv_proj",
    )(h2, q_norm_w, kv_norm_w, w_q, w_kv)


def _t5_bucket_map():
    qi = np.arange(SWA_BLOCK)[:, None]
    ki = np.arange(2 * SWA_BLOCK)[None, :]
    dist = qi + SWA_BLOCK - ki
    n = np.maximum(dist, 0)
    max_exact = REL_BUCKETS // 2
    nf = np.maximum(n, 1).astype(np.float32)
    large = max_exact + (np.log(nf / np.float32(max_exact)).astype(np.float32)
                         / np.float32(math.log(REL_MAX_DISTANCE / max_exact))
                         * np.float32(REL_BUCKETS - max_exact)).astype(np.int32)
    large = np.minimum(large, REL_BUCKETS - 1)
    bucket = np.where(n < max_exact, n, large).astype(np.int32)
    in_window = (dist >= 0) & (dist < SWA_WINDOW)
    return np.where(in_window, bucket, -1).astype(np.int32)


def _swa_kernel(table_ref, sinks_ref, q_ref, kvc_ref, kvp_ref, bmap_ref, out_ref, bias_scr):
    n = pl.program_id(1)
    blk = SWA_BLOCK
    hd = SWA_HEAD_DIM
    kvd = SWA_KV_HEADS * hd

    @pl.when((pl.program_id(0) == 0) & (n == 0))
    def _():
        bmap = bmap_ref[...]
        for hq in range(SWA_Q_HEADS):
            bias_scr[0, hq] = jnp.where(bmap < 0, NEG_INF, 0.0)

        def bucket_body(b, carry):
            hit = bmap == b
            for hq in range(SWA_Q_HEADS):
                bias_scr[0, hq] = jnp.where(hit, table_ref[b, hq], bias_scr[0, hq])
            return carry

        lax.fori_loop(0, REL_BUCKETS, bucket_body, 0)
        key_col = lax.broadcasted_iota(jnp.int32, (blk, 2 * blk), 1)
        for hq in range(SWA_Q_HEADS):
            bias_scr[1, hq] = jnp.where(key_col >= blk, bias_scr[0, hq], NEG_INF)
        for hq in range(SWA_Q_HEADS):
            for tbl in range(2):
                bias_scr[tbl, hq] = jnp.where(key_col == 0, sinks_ref[hq], bias_scr[tbl, hq])

    first = (n == 0).astype(jnp.int32)
    kv_band = jnp.concatenate([kvp_ref[...], kvc_ref[...]], axis=0)
    lane = lax.broadcasted_iota(jnp.int32, (1, LANES), 1)
    lo = lane < hd
    ones_blk = jnp.ones((2 * blk, LANES), BF16)
    not_slot0 = lax.broadcasted_iota(jnp.int32, (2 * blk, 1), 0) > 0
    kv_heads = range(SWA_KV_HEADS)
    k_dup, v_aug, q_stack = [], [], []
    for h in kv_heads:
        grp = (h * hd) // LANES
        k_grp = kv_band[:, grp * LANES:(grp + 1) * LANES]
        v_grp = kv_band[:, kvd + grp * LANES:kvd + (grp + 1) * LANES]
        mine = (lo if (h * hd) % LANES == 0 else ~lo) & not_slot0
        k_own = jnp.where(mine, k_grp, 0.0).astype(F32)
        v_own = jnp.where(mine, v_grp, 0.0).astype(F32)
        k_dup.append((k_own + pltpu.roll(k_own, hd, 1)).astype(BF16))
        v_aug.append(jnp.concatenate(
            [(v_own + pltpu.roll(v_own, hd, 1)).astype(BF16), ones_blk], axis=1))
        q_rows = []
        for gidx in range(SWA_GROUP):
            hq = h * SWA_GROUP + gidx
            qg = q_ref[:, (hq * hd) // LANES * LANES:((hq * hd) // LANES + 1) * LANES]
            sel = lo if (hq * hd) % LANES == 0 else ~lo
            q_rows.append(jnp.where(sel, qg, jnp.zeros_like(qg)))
        q_stack.append(jnp.concatenate(q_rows, axis=0))
    s = [_dot_nt(q_stack[h], k_dup[h]) for h in kv_heads]
    sb = [s[h] + bias_scr[first, pl.ds(h * SWA_GROUP, SWA_GROUP)].reshape(SWA_GROUP * blk, 2 * blk)
          for h in kv_heads]
    mx = [jnp.max(sb[h], axis=-1, keepdims=True) for h in kv_heads]
    p = [jnp.exp(sb[h] - mx[h]).astype(BF16) for h in kv_heads]
    oa = [jnp.dot(p[h], v_aug[h], preferred_element_type=F32) for h in kv_heads]
    outs = [oa[h][:, :LANES] / oa[h][:, LANES:] for h in kv_heads]
    for h in kv_heads:
        for pair in range(SWA_GROUP // 2):
            hq0 = h * SWA_GROUP + 2 * pair
            col = (hq0 * hd) // LANES * LANES
            even = outs[h][(2 * pair) * blk:(2 * pair + 1) * blk]
            odd = outs[h][(2 * pair + 1) * blk:(2 * pair + 2) * blk]
            out_ref[:, col:col + LANES] = jnp.where(lo, even, odd).astype(out_ref.dtype)


def _swa(q, kv, rel_table, sinks, bsz, seq):
    m = q.shape[0]
    nb = seq // SWA_BLOCK
    bmap = jnp.asarray(_t5_bucket_map())
    grid_spec = pltpu.PrefetchScalarGridSpec(
        num_scalar_prefetch=0,
        grid=(bsz, nb),
        in_specs=[
            pl.BlockSpec(memory_space=pltpu.SMEM),
            pl.BlockSpec(memory_space=pltpu.SMEM),
            pl.BlockSpec((SWA_BLOCK, q.shape[1]), lambda b, n: (b * nb + n, 0)),
            pl.BlockSpec((SWA_BLOCK, kv.shape[1]), lambda b, n: (b * nb + n, 0)),
            pl.BlockSpec((SWA_BLOCK, kv.shape[1]),
                         lambda b, n: (b * nb + jnp.maximum(n - 1, 0), 0)),
            _const_spec((SWA_BLOCK, 2 * SWA_BLOCK)),
        ],
        out_specs=pl.BlockSpec((SWA_BLOCK, q.shape[1]), lambda b, n: (b * nb + n, 0)),
        scratch_shapes=[pltpu.VMEM((2, SWA_Q_HEADS, SWA_BLOCK, 2 * SWA_BLOCK), F32)],
    )
    return pl.pallas_call(
        _swa_kernel,
        grid_spec=grid_spec,
        out_shape=jax.ShapeDtypeStruct((m, q.shape[1]), BF16),
        compiler_params=_params("arbitrary", "arbitrary"),
        name="swa",
    )(rel_table, sinks, q, kv, kv, bmap)


def kernel(x, a_norm_w, a_w_in, a_conv_w, a_a_log, a_dt_bias, a_out_norm_w, a_w_out,
           kv_norm_w, w_kv, b_norm_w, b_w_q, b_sinks, b_w_o, rel_bias_table,
           ffn_norm_w, ffn_w_up, ffn_conv_w, ffn_conv_b, ffn_w_down, final_norm_w):
    bsz, seq, d = x.shape
    n_a = a_w_in.shape[0]
    n_b = b_w_q.shape[0]
    depth = n_a + n_b
    h = x.reshape(bsz * seq, d)
    kv = None
    head_order = np.concatenate([np.arange(0, GDN_V_HEADS, 2), np.arange(1, GDN_V_HEADS, 2)])
    for layer in range(depth):
        if layer < n_a:
            i = layer
            w_in = a_w_in[i]
            w_ba = w_in[:, GDN_MAIN_DIM:].astype(BF16)
            w_at = w_ba[:, np.concatenate([head_order, GDN_V_HEADS + head_order])].T
            qkv, z, ba, bat = _in_proj(h, a_norm_w[i].reshape(1, d),
                                       w_in[:, :GDN_MAIN_DIM].astype(BF16), w_ba, w_at,
                                       a_conv_w[i], bsz, seq)
            o = _gdn_core(qkv, z, ba, bat, a_a_log[i], a_dt_bias[i], a_out_norm_w[i],
                          head_order, bsz, seq)
            h = _mm_res(o, a_w_out[i].astype(BF16), h)
        else:
            j = layer - n_a
            if j == 0:
                q, kv = _qkv_proj(h, b_norm_w[j].reshape(1, d), kv_norm_w.reshape(1, d),
                                  b_w_q[j].astype(BF16), w_kv.astype(BF16))
            else:
                q, _ = _qkv_proj(h, b_norm_w[j].reshape(1, d), kv_norm_w.reshape(1, d),
                                 b_w_q[j].astype(BF16), w_kv.astype(BF16))
            o = _swa(q, kv, rel_bias_table, b_sinks[j], bsz, seq)
            h = _mm_res(o, b_w_o[j].astype(BF16), h)
        last = layer == depth - 1
        h = _ffn(h, ffn_norm_w[layer].reshape(1, d), ffn_w_up[layer].astype(BF16),
                 ffn_conv_w[layer], ffn_conv_b[layer].reshape(1, 2 * D_FF),
                 ffn_w_down[layer].astype(BF16), final_norm_w.reshape(1, d), bsz, seq, last)
    return h.reshape(bsz, seq, d)
```

```python
import functools
import math

import jax
import jax.numpy as jnp
import numpy as np
from jax import lax
from jax.experimental import pallas as pl
from jax.experimental.pallas import tpu as pltpu

F32 = jnp.float32
BF16 = jnp.bfloat16

EPS = 1e-6
NEG_INF = -1e30

D_MODEL = 1024
GDN_QK_HEADS = 8
GDN_V_HEADS = 16
GDN_HEAD_DIM = 128
GDN_QK_DIM = GDN_QK_HEADS * GDN_HEAD_DIM
GDN_V_DIM = GDN_V_HEADS * GDN_HEAD_DIM
GDN_CONV_DIM = 2 * GDN_QK_DIM + GDN_V_DIM
GDN_MAIN_DIM = GDN_CONV_DIM + GDN_V_DIM
GDN_CONV_WIDTH = 4
GDN_CHUNK = 64

SWA_Q_HEADS = 16
SWA_KV_HEADS = 4
SWA_GROUP = SWA_Q_HEADS // SWA_KV_HEADS
SWA_HEAD_DIM = 64
SWA_WINDOW = 128
SWA_BLOCK = 128
REL_BUCKETS = 32
REL_MAX_DISTANCE = 128

D_FF = 2816
FFN_CONV_WIDTH = 3
FFN_CHUNK = 256
CONV_ROWS = 64
PREP_CHUNKS = 4

SUBLANES = 8
LANES = 128
VMEM_LIMIT_BYTES = 56 * 1024 * 1024

TM_PROJ = 256
TM_GDN = 256
TM_FFN = 512
FFN_SUB = 256
TM_MM = 512


def _params(*sem):
    return pltpu.CompilerParams(dimension_semantics=sem, vmem_limit_bytes=VMEM_LIMIT_BYTES)


def _const_spec(shape):
    zeros = (0,) * len(shape)
    return pl.BlockSpec(shape, lambda *_: zeros)


def _dot(a, b):
    return jnp.dot(a.astype(BF16), b.astype(BF16), preferred_element_type=F32)


def _dot_nt(a, b):
    return lax.dot_general(a.astype(BF16), b.astype(BF16), (((1,), (1,)), ((), ())),
                           preferred_element_type=F32)


def _rms_scale(x):
    return x * lax.rsqrt(jnp.mean(x * x, axis=-1, keepdims=True) + EPS)


def _silu(x):
    half = 0.5 * x
    return half + half * jnp.tanh(half)


def _shift_rows(cur, prev8, j):
    if j == 0:
        return cur
    nrow, ncol = cur.shape
    tiles = jnp.concatenate([prev8, cur], axis=0).reshape(nrow // SUBLANES + 1, SUBLANES, ncol)
    rot = pltpu.roll(tiles, j, 1)
    row = lax.broadcasted_iota(jnp.int32, (1, SUBLANES, ncol), 1)
    return jnp.where(row < j, rot[:-1], rot[1:]).reshape(nrow, ncol)


def _split3(x):
    hi = x.astype(BF16)
    r1 = x - hi.astype(F32)
    mid = r1.astype(BF16)
    lo = (r1 - mid.astype(F32)).astype(BF16)
    return hi, mid, lo


def _in_proj_kernel(x_ref, nw_ref, wm_ref, wba_ref, wbat_ref, cw_ref,
                    qkv_ref, z_ref, ba_ref, bat_ref, carry_scr):
    t = pl.program_id(1)

    @pl.when(t == 0)
    def _():
        carry_scr[...] = jnp.zeros_like(carry_scr)

    tm = x_ref.shape[0]
    hd = GDN_HEAD_DIM
    hn = (_rms_scale(x_ref[...]) * nw_ref[...]).astype(BF16)
    step = 2 * hd
    zevery = GDN_CONV_DIM // GDN_V_DIM
    for c0 in range(0, GDN_CONV_DIM, step):
        if (c0 // step) % zevery == 0:
            z0 = c0 // zevery
            z_ref[:, z0:z0 + step] = jnp.dot(
                hn, wm_ref[(GDN_CONV_DIM + z0) // step], preferred_element_type=F32)
        sl = slice(c0, c0 + step)
        cur = jnp.dot(hn, wm_ref[c0 // step], preferred_element_type=F32)
        prev8 = carry_scr[:, sl]
        carry_scr[:, sl] = cur[tm - SUBLANES:]
        for hc in range(c0, c0 + step, hd):
            lanes = slice(hc - c0, hc - c0 + hd)
            wts = [cw_ref[GDN_CONV_WIDTH - 1 - j:GDN_CONV_WIDTH - j, hc:hc + hd]
                   for j in range(GDN_CONV_WIDTH)]
            for r0 in range(0, tm, CONV_ROWS):
                piece = cur[r0:r0 + CONV_ROWS, lanes]
                before = prev8[:, lanes] if r0 == 0 else cur[r0 - SUBLANES:r0, lanes]
                yh = piece * wts[0]
                for j in range(1, GDN_CONV_WIDTH):
                    yh = yh + _shift_rows(piece, before, j) * wts[j]
                yh = _silu(yh)
                if hc < 2 * GDN_QK_DIM:
                    yh = yh * lax.rsqrt(jnp.sum(yh * yh, axis=-1, keepdims=True) + EPS)
                    if hc < GDN_QK_DIM:
                        yh = yh * (hd ** -0.5)
                qkv_ref[r0:r0 + CONV_ROWS, hc:hc + hd] = yh.astype(qkv_ref.dtype)
    ba_ref[...] = jnp.dot(hn, wba_ref[...], preferred_element_type=F32)
    bat_ref[...] = _dot_nt(wbat_ref[...], hn)


def _in_proj(x2, norm_w, w_main, w_ba, w_bat, conv_w, bsz, seq):
    m, k = x2.shape
    n = w_main.shape[1]
    w_main = w_main.reshape(k, n // (2 * GDN_HEAD_DIM), 2 * GDN_HEAD_DIM).transpose(1, 0, 2)
    nb = w_ba.shape[1]
    nbt = w_bat.shape[0]
    tm = TM_PROJ
    nt = seq // tm

    def row_map(b, t):
        return (b * nt + t, 0)

    return pl.pallas_call(
        _in_proj_kernel,
        grid=(bsz, nt),
        in_specs=[
            pl.BlockSpec((tm, k), row_map),
            _const_spec((1, k)),
            _const_spec(w_main.shape),
            _const_spec((k, nb)),
            _const_spec((nbt, k)),
            _const_spec((GDN_CONV_WIDTH, GDN_CONV_DIM)),
        ],
        out_specs=[
            pl.BlockSpec((tm, GDN_CONV_DIM), row_map),
            pl.BlockSpec((tm, GDN_V_DIM), row_map),
            pl.BlockSpec((tm, nb), row_map),
            pl.BlockSpec((nbt, tm), lambda b, t: (0, b * nt + t)),
        ],
        out_shape=[
            jax.ShapeDtypeStruct((m, GDN_CONV_DIM), BF16),
            jax.ShapeDtypeStruct((m, GDN_V_DIM), F32),
            jax.ShapeDtypeStruct((m, nb), F32),
            jax.ShapeDtypeStruct((nbt, m), F32),
        ],
        scratch_shapes=[pltpu.VMEM((SUBLANES, GDN_CONV_DIM), F32)],
        compiler_params=_params("parallel", "arbitrary"),
        name="gdn_in_proj",
    )(x2, norm_w, w_main, w_ba, w_bat, conv_w)


def _gdn_core_kernel(qkv_ref, z_ref, ba_ref, bat_ref, alog_ref, dtb_ref,
                     alogt_ref, dtbt_ref, onw_ref, out_ref,
                     s_scr, gcx_scr, betap_scr, rows_scr, u_scr, lhs_scr, a_scr, kt_scr):
    t = pl.program_id(1)
    tm = qkv_ref.shape[0]
    c = GDN_CHUNK
    nchunk = tm // c
    hd = GDN_HEAD_DIM
    nh = GDN_V_HEADS
    npair = GDN_QK_HEADS
    heads = range(nh)
    pairs = range(npair)

    @pl.when(t == 0)
    def _():
        s_scr[...] = jnp.zeros_like(s_scr)

    beta = 1.0 / (1.0 + jnp.exp(-ba_ref[:, :nh]))
    a_lin = ba_ref[:, nh:2 * nh] + dtb_ref[...]
    g = -jnp.exp(alog_ref[...]) * (jnp.maximum(a_lin, 0.0) + jnp.log1p(jnp.exp(-jnp.abs(a_lin))))
    row = lax.broadcasted_iota(jnp.int32, (tm, tm), 0)
    col = lax.broadcasted_iota(jnp.int32, (tm, tm), 1)
    same = (row // c) == (col // c)
    lblk = (same & (col <= row)).astype(BF16)
    ublk = (same & (row <= col)).astype(BF16)
    gc = jnp.dot(jnp.concatenate([lblk] * 3, axis=1), jnp.concatenate(_split3(g), axis=0),
                 preferred_element_type=F32)
    e_row = lax.broadcasted_iota(jnp.int32, (3 * nh, npair * 2 * c), 0) % nh
    e_col = lax.broadcasted_iota(jnp.int32, (3 * nh, npair * 2 * c), 1)
    expand = (e_row == e_col // c).astype(BF16)
    betap_scr[...] = jnp.dot(jnp.concatenate(_split3(beta), axis=1), expand,
                             preferred_element_type=F32)
    e_row = lax.broadcasted_iota(jnp.int32, (3 * nh, nh * hd), 0) % nh
    e_col = lax.broadcasted_iota(jnp.int32, (3 * nh, nh * hd), 1)
    expand_hd = (e_row == e_col // hd).astype(BF16)
    gcx_scr[...] = jnp.dot(jnp.concatenate(_split3(gc), axis=1), expand_hd,
                           preferred_element_type=F32)

    beta_t = 1.0 / (1.0 + jnp.exp(-bat_ref[:nh, :]))
    a_lin_t = bat_ref[nh:, :] + dtbt_ref[...]
    g_t = -jnp.exp(alogt_ref[...]) * (jnp.maximum(a_lin_t, 0.0)
                                      + jnp.log1p(jnp.exp(-jnp.abs(a_lin_t))))
    gct = jnp.dot(jnp.concatenate(_split3(g_t), axis=1), jnp.concatenate([ublk] * 3, axis=0),
                  preferred_element_type=F32)
    begct = beta_t * jnp.exp(gct)
    for ci in range(nchunk):
        for slot, arr in enumerate((gct, beta_t, begct)):
            rows_scr[slot, ci] = jnp.concatenate([arr[:npair, ci * c:(ci + 1) * c],
                                                  arr[npair:, ci * c:(ci + 1) * c]], axis=1)

    ri = lax.broadcasted_iota(jnp.int32, (c, 2 * c), 0)
    cj = lax.broadcasted_iota(jnp.int32, (c, 2 * c), 1) % c
    left = lax.broadcasted_iota(jnp.int32, (1, 2 * c), 1) < c
    tril = cj <= ri
    strict = cj < ri
    eye = (ri == cj).astype(F32)
    onw = onw_ref[...]
    zero_hd = jnp.zeros((c, hd), BF16)

    def block_diag(xc):
        xb = xc.astype(BF16)
        zero = jnp.zeros_like(xb)
        return jnp.concatenate([jnp.where(left, xb, zero), jnp.where(left, zero, xb)], axis=0)

    def two_blocks(top, bottom):
        return jnp.concatenate([jnp.concatenate([top, zero_hd], axis=1),
                                jnp.concatenate([zero_hd, bottom], axis=1)], axis=0)

    def prepare(chunk_ids):
        items = [(n, j) for n in range(len(chunk_ids)) for j in pairs]
        rows = [pl.ds(pl.multiple_of(ci * c, c), c) for ci in chunk_ids]
        q = {(n, j): qkv_ref[rows[n], j * hd:(j + 1) * hd] for n, j in items}
        k = {(n, j): qkv_ref[rows[n], GDN_QK_DIM + j * hd:GDN_QK_DIM + (j + 1) * hd]
             for n, j in items}
        gq = {i: _dot_nt(jnp.concatenate([k[i], q[i]], axis=0), jnp.concatenate([k[i], k[i]], axis=0))
              for i in items}
        dec = {}
        for n, j in items:
            gc_pair = jnp.where(left, gcx_scr[rows[n], 2 * j * hd:2 * j * hd + 2 * c],
                                gcx_scr[rows[n], (2 * j + 1) * hd:(2 * j + 1) * hd + 2 * c])
            diff = gc_pair - rows_scr[0, chunk_ids[n], pl.ds(j, 1), :]
            dec[n, j] = jnp.where(tril, jnp.exp(jnp.where(tril, diff, 0.0)), 0.0)
        m = {(n, j): jnp.where(strict, betap_scr[rows[n], j * 2 * c:(j + 1) * 2 * c]
                               * gq[n, j][:c] * dec[n, j], 0.0) for n, j in items}
        for n, j in items:
            a_scr[chunk_ids[n], j] = (gq[n, j][c:] * dec[n, j]).astype(BF16)
        x = {i: eye - m[i] for i in items}
        p = {i: _dot(m[i], block_diag(m[i])) for i in items}
        levels = int(math.log2(c)) - 1
        for lvl in range(levels):
            if lvl + 1 < levels:
                xp = {i: _dot(jnp.concatenate([x[i], p[i]], axis=0), block_diag(p[i])) for i in items}
                x = {i: x[i] + xp[i][:c] for i in items}
                p = {i: xp[i][c:] for i in items}
            else:
                xp = {i: _dot(x[i], block_diag(p[i])) for i in items}
                x = {i: x[i] + xp[i] for i in items}
        u, w = {}, {}
        for n, j in items:
            ci = chunk_ids[n]
            t_u = x[n, j] * rows_scr[1, ci, pl.ds(j, 1), :]
            t_w = x[n, j] * rows_scr[2, ci, pl.ds(j, 1), :]
            v_a = qkv_ref[rows[n], 2 * GDN_QK_DIM + 2 * j * hd:2 * GDN_QK_DIM + (2 * j + 1) * hd]
            v_b = qkv_ref[rows[n], 2 * GDN_QK_DIM + (2 * j + 1) * hd:2 * GDN_QK_DIM + (2 * j + 2) * hd]
            u[n, j] = _dot(t_u, two_blocks(v_a, v_b))
            w[n, j] = _dot(t_w, two_blocks(k[n, j], k[n, j]))
        for n, j in items:
            ci = chunk_ids[n]
            for e in range(2):
                h = 2 * j + e
                u_scr[ci, h] = u[n, j][:, e * hd:(e + 1) * hd]
                lhs_scr[ci, h] = jnp.concatenate(
                    [w[n, j][:, e * hd:(e + 1) * hd].astype(BF16), q[n, j]], axis=0)
            if j % 2 == 0:
                k2 = qkv_ref[rows[n], GDN_QK_DIM + j * hd:GDN_QK_DIM + (j + 2) * hd]
                kt_scr[ci, j // 2] = jnp.concatenate(
                    [k2[:, :hd], k2[:, hd:]], axis=0).astype(F32).T.astype(BF16)

    def prepare_body(i, carry):
        prepare([i * PREP_CHUNKS + n for n in range(PREP_CHUNKS)])
        return carry

    lax.fori_loop(0, nchunk // PREP_CHUNKS, prepare_body, 0)

    def state_body(ci, carry):
        r0 = pl.multiple_of(ci * c, c)
        rows = pl.ds(r0, c)
        gcx = [gcx_scr[rows, h * hd:(h + 1) * hd] for h in heads]
        egc = [jnp.exp(gcx[h]) for h in heads]
        ekd = [jnp.exp(gcx[h][c - 1:c, :] - gcx[h]) for h in heads]
        s_old = [s_scr[j] for j in pairs]
        pq = [_dot(lhs_scr[ci, h], s_old[h // 2][:, (h % 2) * hd:(h % 2 + 1) * hd])
              for h in heads]
        vn = [u_scr[ci, h] - pq[h][:c] for h in heads]
        vnb = [vn[h].astype(BF16) for h in heads]
        avn = [_dot(a_scr[ci, j], two_blocks(vnb[2 * j], vnb[2 * j + 1])) for j in pairs]
        vs = [jnp.concatenate([vn[2 * j] * ekd[2 * j], vn[2 * j + 1] * ekd[2 * j + 1]],
                              axis=1).astype(BF16) for j in pairs]
        zero_vs = jnp.zeros_like(vs[0])
        ds = [_dot(kt_scr[ci, j // 2], jnp.concatenate(
            [vs[j], zero_vs] if j % 2 == 0 else [zero_vs, vs[j]], axis=0)) for j in pairs]
        for j in pairs:
            egl = jnp.exp(jnp.concatenate([gcx[2 * j][c - 1:c, :], gcx[2 * j + 1][c - 1:c, :]], axis=1))
            s_scr[j] = s_old[j] * egl + ds[j]
        for h in heads:
            o_h = (pq[h][c:] * egc[h]
                   + avn[h // 2][:, (h % 2) * hd:(h % 2 + 1) * hd])
            z_h = z_ref[rows, h * hd:(h + 1) * hd]
            out_ref[rows, h * hd:(h + 1) * hd] = (
                _rms_scale(o_h) * onw * _silu(z_h)).astype(out_ref.dtype)
        return carry

    lax.fori_loop(0, nchunk, state_body, 0)


def _gdn_core(qkv, z, ba, bat, a_log, dt_bias, out_norm_w, head_order, bsz, seq):
    m = qkv.shape[0]
    tm = TM_GDN
    nt = seq // tm
    nh = GDN_V_HEADS
    nb = ba.shape[1]
    nchunk = tm // GDN_CHUNK

    def cur_map(b, t):
        return (b * nt + t, 0)

    return pl.pallas_call(
        _gdn_core_kernel,
        grid=(bsz, nt),
        in_specs=[
            pl.BlockSpec((tm, GDN_CONV_DIM), cur_map),
            pl.BlockSpec((tm, GDN_V_DIM), cur_map),
            pl.BlockSpec((tm, nb), cur_map),
            pl.BlockSpec((2 * nh, tm), lambda b, t: (0, b * nt + t)),
            _const_spec((1, nh)),
            _const_spec((1, nh)),
            _const_spec((nh, 1)),
            _const_spec((nh, 1)),
            _const_spec((1, GDN_HEAD_DIM)),
        ],
        out_specs=pl.BlockSpec((tm, GDN_V_DIM), cur_map),
        out_shape=jax.ShapeDtypeStruct((m, GDN_V_DIM), BF16),
        scratch_shapes=[
            pltpu.VMEM((GDN_QK_HEADS, GDN_HEAD_DIM, 2 * GDN_HEAD_DIM), F32),
            pltpu.VMEM((tm, nh * GDN_HEAD_DIM), F32),
            pltpu.VMEM((tm, nh * GDN_CHUNK), F32),
            pltpu.VMEM((3, nchunk, GDN_QK_HEADS, 2 * GDN_CHUNK), F32),
            pltpu.VMEM((nchunk, nh, GDN_CHUNK, GDN_HEAD_DIM), F32),
            pltpu.VMEM((nchunk, nh, 2 * GDN_CHUNK, GDN_HEAD_DIM), BF16),
            pltpu.VMEM((nchunk, GDN_QK_HEADS, GDN_CHUNK, 2 * GDN_CHUNK), BF16),
            pltpu.VMEM((nchunk, GDN_QK_HEADS // 2, GDN_HEAD_DIM, 2 * GDN_CHUNK), BF16),
        ],
        compiler_params=_params("parallel", "arbitrary"),
        name="gdn_core",
    )(qkv, z, ba, bat, a_log.reshape(1, nh), dt_bias.reshape(1, nh),
      a_log[head_order].reshape(nh, 1), dt_bias[head_order].reshape(nh, 1),
      out_norm_w.reshape(1, GDN_HEAD_DIM))


def _mm_res_kernel(x_ref, w_ref, res_ref, out_ref):
    out_ref[...] = res_ref[...] + jnp.dot(x_ref[...].astype(BF16), w_ref[...],
                                          preferred_element_type=F32)


def _mm_res(x2, w, res):
    m, k = x2.shape
    n = w.shape[1]
    tm = TM_MM
    return pl.pallas_call(
        _mm_res_kernel,
        grid=(m // tm,),
        in_specs=[
            pl.BlockSpec((tm, k), lambda i: (i, 0)),
            _const_spec((k, n)),
            pl.BlockSpec((tm, n), lambda i: (i, 0)),
        ],
        out_specs=pl.BlockSpec((tm, n), lambda i: (i, 0)),
        out_shape=jax.ShapeDtypeStruct((m, n), F32),
        compiler_params=_params("parallel"),
        name="mm_res",
    )(x2, w, res)


def _ffn_kernel(h_ref, nw_ref, wup_ref, cw_ref, cb_ref, wdn_ref, fnw_ref, out_ref,
                carry_scr, act_scr, *, final_norm):
    t = pl.program_id(1)

    @pl.when(t == 0)
    def _():
        carry_scr[...] = jnp.zeros_like(carry_scr)

    nsub = h_ref.shape[0] // FFN_SUB

    def up_phase(si):
        rows = slice(si * FFN_SUB, (si + 1) * FFN_SUB)
        h = h_ref[rows, :]
        hn = (_rms_scale(h) * nw_ref[...]).astype(BF16)
        for c0 in range(0, D_FF, FFN_CHUNK):
            halves = []
            for off in (0, D_FF):
                sl = slice(off + c0, off + c0 + FFN_CHUNK)
                cur = jnp.dot(hn, wup_ref[:, sl], preferred_element_type=F32)
                prev8 = carry_scr[:, sl]
                carry_scr[:, sl] = cur[FFN_SUB - SUBLANES:]
                y = cur * cw_ref[FFN_CONV_WIDTH - 1:FFN_CONV_WIDTH, sl] + cb_ref[:, sl]
                for j in range(1, FFN_CONV_WIDTH):
                    y = y + (_shift_rows(cur, prev8, j)
                             * cw_ref[FFN_CONV_WIDTH - 1 - j:FFN_CONV_WIDTH - j, sl])
                halves.append(y)
            act_scr[si, :, c0:c0 + FFN_CHUNK] = (_silu(halves[0]) * halves[1]).astype(BF16)

    def down_phase(si):
        rows = slice(si * FFN_SUB, (si + 1) * FFN_SUB)
        act = act_scr[si]
        res = h_ref[rows, :] + jnp.dot(act, wdn_ref[...], preferred_element_type=F32)
        if final_norm:
            res = _rms_scale(res) * fnw_ref[...]
        out_ref[rows, :] = res

    up_phase(0)
    for si in range(1, nsub):
        up_phase(si)
        down_phase(si - 1)
    down_phase(nsub - 1)


def _ffn(h2, norm_w, w_up, conv_w, conv_b, w_down, final_w, bsz, seq, final_norm):
    m, k = h2.shape
    tm = TM_FFN
    nt = seq // tm

    def row_map(b, t):
        return (b * nt + t, 0)

    return pl.pallas_call(
        functools.partial(_ffn_kernel, final_norm=final_norm),
        grid=(bsz, nt),
        in_specs=[
            pl.BlockSpec((tm, k), row_map),
            _const_spec((1, k)),
            _const_spec(w_up.shape),
            _const_spec((FFN_CONV_WIDTH, 2 * D_FF)),
            _const_spec((1, 2 * D_FF)),
            _const_spec(w_down.shape),
            _const_spec((1, k)),
        ],
        out_specs=pl.BlockSpec((tm, k), row_map),
        out_shape=jax.ShapeDtypeStruct((m, k), F32),
        scratch_shapes=[
            pltpu.VMEM((SUBLANES, 2 * D_FF), F32),
            pltpu.VMEM((tm // FFN_SUB, FFN_SUB, D_FF), BF16),
        ],
        compiler_params=_params("parallel", "arbitrary"),
        name="ffn_final" if final_norm else "ffn",
    )(h2, norm_w, w_up, conv_w, conv_b, w_down, final_w)


def _qkv_proj_kernel(h_ref, qnw_ref, kvnw_ref, wq_ref, wkv_ref, q_ref, kv_ref):
    y = _rms_scale(h_ref[...])
    q = jnp.dot((y * qnw_ref[...]).astype(BF16), wq_ref[...], preferred_element_type=F32)
    q_ref[...] = (q * (SWA_HEAD_DIM ** -0.5)).astype(q_ref.dtype)
    kv_ref[...] = jnp.dot((y * kvnw_ref[...]).astype(BF16), wkv_ref[...],
                          preferred_element_type=F32).astype(kv_ref.dtype)


def _qkv_proj(h2, q_norm_w, kv_norm_w, w_q, w_kv):
    m, k = h2.shape
    nq = w_q.shape[1]
    nkv = w_kv.shape[1]
    tm = TM_MM
    return pl.pallas_call(
        _qkv_proj_kernel,
        grid=(m // tm,),
        in_specs=[
            pl.BlockSpec((tm, k), lambda i: (i, 0)),
            _const_spec((1, k)),
            _const_spec((1, k)),
            _const_spec((k, nq)),
            _const_spec((k, nkv)),
        ],
        out_specs=[
            pl.BlockSpec((tm, nq), lambda i: (i, 0)),
            pl.BlockSpec((tm, nkv), lambda i: (i, 0)),
        ],
        out_shape=[
            jax.ShapeDtypeStruct((m, nq), BF16),
            jax.ShapeDtypeStruct((m, nkv), BF16),
        ],
        compiler_params=_params("parallel"),
        name="qkv_proj",
    )(h2, q_norm_w, kv_norm_w, w_q, w_kv)


def _t5_bucket_map():
    qi = np.arange(SWA_BLOCK)[:, None]
    ki = np.arange(2 * SWA_BLOCK)[None, :]
    dist = qi + SWA_BLOCK - ki
    n = np.maximum(dist, 0)
    max_exact = REL_BUCKETS // 2
    nf = np.maximum(n, 1).astype(np.float32)
    large = max_exact + (np.log(nf / np.float32(max_exact)).astype(np.float32)
                         / np.float32(math.log(REL_MAX_DISTANCE / max_exact))
                         * np.float32(REL_BUCKETS - max_exact)).astype(np.int32)
    large = np.minimum(large, REL_BUCKETS - 1)
    bucket = np.where(n < max_exact, n, large).astype(np.int32)
    in_window = (dist >= 0) & (dist < SWA_WINDOW)
    return np.where(in_window, bucket, -1).astype(np.int32)


def _swa_kernel(table_ref, sinks_ref, q_ref, kvc_ref, kvp_ref, bmap_ref, out_ref, bias_scr):
    n = pl.program_id(1)
    blk = SWA_BLOCK
    hd = SWA_HEAD_DIM
    kvd = SWA_KV_HEADS * hd

    @pl.when((pl.program_id(0) == 0) & (n == 0))
    def _():
        bmap = bmap_ref[...]
        for hq in range(SWA_Q_HEADS):
            bias_scr[0, hq] = jnp.where(bmap < 0, NEG_INF, 0.0)

        def bucket_body(b, carry):
            hit = bmap == b
            for hq in range(SWA_Q_HEADS):
                bias_scr[0, hq] = jnp.where(hit, table_ref[b, hq], bias_scr[0, hq])
            return carry

        lax.fori_loop(0, REL_BUCKETS, bucket_body, 0)
        key_col = lax.broadcasted_iota(jnp.int32, (blk, 2 * blk), 1)
        for hq in range(SWA_Q_HEADS):
            bias_scr[1, hq] = jnp.where(key_col >= blk, bias_scr[0, hq], NEG_INF)
        for hq in range(SWA_Q_HEADS):
            for tbl in range(2):
                bias_scr[tbl, hq] = jnp.where(key_col == 0, sinks_ref[hq], bias_scr[tbl, hq])

    first = (n == 0).astype(jnp.int32)
    kv_band = jnp.concatenate([kvp_ref[...], kvc_ref[...]], axis=0)
    lane = lax.broadcasted_iota(jnp.int32, (1, LANES), 1)
    lo = lane < hd
    ones_blk = jnp.ones((2 * blk, LANES), BF16)
    not_slot0 = lax.broadcasted_iota(jnp.int32, (2 * blk, 1), 0) > 0
    kv_heads = range(SWA_KV_HEADS)
    k_dup, v_aug, q_stack = [], [], []
    for h in kv_heads:
        grp = (h * hd) // LANES
        k_grp = kv_band[:, grp * LANES:(grp + 1) * LANES]
        v_grp = kv_band[:, kvd + grp * LANES:kvd + (grp + 1) * LANES]
        mine = (lo if (h * hd) % LANES == 0 else ~lo) & not_slot0
        k_own = jnp.where(mine, k_grp, 0.0).astype(F32)
        v_own = jnp.where(mine, v_grp, 0.0).astype(F32)
        k_dup.append((k_own + pltpu.roll(k_own, hd, 1)).astype(BF16))
        v_aug.append(jnp.concatenate(
            [(v_own + pltpu.roll(v_own, hd, 1)).astype(BF16), ones_blk], axis=1))
        q_rows = []
        for gidx in range(SWA_GROUP):
            hq = h * SWA_GROUP + gidx
            qg = q_ref[:, (hq * hd) // LANES * LANES:((hq * hd) // LANES + 1) * LANES]
            sel = lo if (hq * hd) % LANES == 0 else ~lo
            q_rows.append(jnp.where(sel, qg, jnp.zeros_like(qg)))
        q_stack.append(jnp.concatenate(q_rows, axis=0))
    s = [_dot_nt(q_stack[h], k_dup[h]) for h in kv_heads]
    sb = [s[h] + bias_scr[first, pl.ds(h * SWA_GROUP, SWA_GROUP)].reshape(SWA_GROUP * blk, 2 * blk)
          for h in kv_heads]
    mx = [jnp.max(sb[h], axis=-1, keepdims=True) for h in kv_heads]
    p = [jnp.exp(sb[h] - mx[h]).astype(BF16) for h in kv_heads]
    oa = [jnp.dot(p[h], v_aug[h], preferred_element_type=F32) for h in kv_heads]
    outs = [oa[h][:, :LANES] / oa[h][:, LANES:] for h in kv_heads]
    for h in kv_heads:
        for pair in range(SWA_GROUP // 2):
            hq0 = h * SWA_GROUP + 2 * pair
            col = (hq0 * hd) // LANES * LANES
            even = outs[h][(2 * pair) * blk:(2 * pair + 1) * blk]
            odd = outs[h][(2 * pair + 1) * blk:(2 * pair + 2) * blk]
            out_ref[:, col:col + LANES] = jnp.where(lo, even, odd).astype(out_ref.dtype)


def _swa(q, kv, rel_table, sinks, bsz, seq):
    m = q.shape[0]
    nb = seq // SWA_BLOCK
    bmap = jnp.asarray(_t5_bucket_map())
    grid_spec = pltpu.PrefetchScalarGridSpec(
        num_scalar_prefetch=0,
        grid=(bsz, nb),
        in_specs=[
            pl.BlockSpec(memory_space=pltpu.SMEM),
            pl.BlockSpec(memory_space=pltpu.SMEM),
            pl.BlockSpec((SWA_BLOCK, q.shape[1]), lambda b, n: (b * nb + n, 0)),
            pl.BlockSpec((SWA_BLOCK, kv.shape[1]), lambda b, n: (b * nb + n, 0)),
            pl.BlockSpec((SWA_BLOCK, kv.shape[1]),
                         lambda b, n: (b * nb + jnp.maximum(n - 1, 0), 0)),
            _const_spec((SWA_BLOCK, 2 * SWA_BLOCK)),
        ],
        out_specs=pl.BlockSpec((SWA_BLOCK, q.shape[1]), lambda b, n: (b * nb + n, 0)),
        scratch_shapes=[pltpu.VMEM((2, SWA_Q_HEADS, SWA_BLOCK, 2 * SWA_BLOCK), F32)],
    )
    return pl.pallas_call(
        _swa_kernel,
        grid_spec=grid_spec,
        out_shape=jax.ShapeDtypeStruct((m, q.shape[1]), BF16),
        compiler_params=_params("arbitrary", "arbitrary"),
        name="swa",
    )(rel_table, sinks, q, kv, kv, bmap)


def kernel(x, a_norm_w, a_w_in, a_conv_w, a_a_log, a_dt_bias, a_out_norm_w, a_w_out,
           kv_norm_w, w_kv, b_norm_w, b_w_q, b_sinks, b_w_o, rel_bias_table,
           ffn_norm_w, ffn_w_up, ffn_conv_w, ffn_conv_b, ffn_w_down, final_norm_w):
    bsz, seq, d = x.shape
    n_a = a_w_in.shape[0]
    n_b = b_w_q.shape[0]
    depth = n_a + n_b
    h = x.reshape(bsz * seq, d)
    kv = None
    head_order = np.concatenate([np.arange(0, GDN_V_HEADS, 2), np.arange(1, GDN_V_HEADS, 2)])
    for layer in range(depth):
        if layer < n_a:
            i = layer
            w_in = a_w_in[i]
            w_ba = w_in[:, GDN_MAIN_DIM:].astype(BF16)
            w_at = w_ba[:, np.concatenate([head_order, GDN_V_HEADS + head_order])].T
            qkv, z, ba, bat = _in_proj(h, a_norm_w[i].reshape(1, d),
                                       w_in[:, :GDN_MAIN_DIM].astype(BF16), w_ba, w_at,
                                       a_conv_w[i], bsz, seq)
            o = _gdn_core(qkv, z, ba, bat, a_a_log[i], a_dt_bias[i], a_out_norm_w[i],
                          head_order, bsz, seq)
            h = _mm_res(o, a_w_out[i].astype(BF16), h)
        else:
            j = layer - n_a
            if j == 0:
                q, kv = _qkv_proj(h, b_norm_w[j].reshape(1, d), kv_norm_w.reshape(1, d),
                                  b_w_q[j].astype(BF16), w_kv.astype(BF16))
            else:
                q, _ = _qkv_proj(h, b_norm_w[j].reshape(1, d), kv_norm_w.reshape(1, d),
                                 b_w_q[j].astype(BF16), w_kv.astype(BF16))
            o = _swa(q, kv, rel_bias_table, b_sinks[j], bsz, seq)
            h = _mm_res(o, b_w_o[j].astype(BF16), h)
        last = layer == depth - 1
        h = _ffn(h, ffn_norm_w[layer].reshape(1, d), ffn_w_up[layer].astype(BF16),
                 ffn_conv_w[layer], ffn_conv_b[layer].reshape(1, 2 * D_FF),
                 ffn_w_down[layer].astype(BF16), final_norm_w.reshape(1, d), bsz, seq, last)
    return h.reshape(bsz, seq, d)
```

```python
import functools
import math

import jax
import jax.numpy as jnp
import numpy as np
from jax import lax
from jax.experimental import pallas as pl
from jax.experimental.pallas import tpu as pltpu

F32 = jnp.float32
BF16 = jnp.bfloat16

EPS = 1e-6
NEG_INF = -1e30

D_MODEL = 1024
GDN_QK_HEADS = 8
GDN_V_HEADS = 16
GDN_HEAD_DIM = 128
GDN_QK_DIM = GDN_QK_HEADS * GDN_HEAD_DIM
GDN_V_DIM = GDN_V_HEADS * GDN_HEAD_DIM
GDN_CONV_DIM = 2 * GDN_QK_DIM + GDN_V_DIM
GDN_MAIN_DIM = GDN_CONV_DIM + GDN_V_DIM
GDN_CONV_WIDTH = 4
GDN_CHUNK = 64

SWA_Q_HEADS = 16
SWA_KV_HEADS = 4
SWA_GROUP = SWA_Q_HEADS // SWA_KV_HEADS
SWA_HEAD_DIM = 64
SWA_WINDOW = 128
SWA_BLOCK = 128
REL_BUCKETS = 32
REL_MAX_DISTANCE = 128

D_FF = 2816
FFN_CONV_WIDTH = 3
FFN_CHUNK = 256
CONV_ROWS = 64
PREP_CHUNKS = 4

SUBLANES = 8
LANES = 128
VMEM_LIMIT_BYTES = 56 * 1024 * 1024

TM_PROJ = 256
TM_GDN = 256
TM_FFN = 512
FFN_SUB = 256
TM_MM = 512


def _params(*sem):
    return pltpu.CompilerParams(dimension_semantics=sem, vmem_limit_bytes=VMEM_LIMIT_BYTES)


def _const_spec(shape):
    zeros = (0,) * len(shape)
    return pl.BlockSpec(shape, lambda *_: zeros)


def _resident_spec(shape):
    zeros = (0,) * len(shape)
    return pl.BlockSpec(shape, lambda *_: zeros, pipeline_mode=pl.Buffered(1))


def _dot(a, b):
    return jnp.dot(a.astype(BF16), b.astype(BF16), preferred_element_type=F32)


def _dot_nt(a, b):
    return lax.dot_general(a.astype(BF16), b.astype(BF16), (((1,), (1,)), ((), ())),
                           preferred_element_type=F32)


def _rms_scale(x):
    return x * lax.rsqrt(jnp.mean(x * x, axis=-1, keepdims=True) + EPS)


def _silu(x):
    half = 0.5 * x
    return half + half * jnp.tanh(half)


def _shift_rows(cur, prev8, j):
    if j == 0:
        return cur
    nrow, ncol = cur.shape
    tiles = jnp.concatenate([prev8, cur], axis=0).reshape(nrow // SUBLANES + 1, SUBLANES, ncol)
    rot = pltpu.roll(tiles, j, 1)
    row = lax.broadcasted_iota(jnp.int32, (1, SUBLANES, ncol), 1)
    return jnp.where(row < j, rot[:-1], rot[1:]).reshape(nrow, ncol)


def _split3(x):
    hi = x.astype(BF16)
    r1 = x - hi.astype(F32)
    mid = r1.astype(BF16)
    lo = (r1 - mid.astype(F32)).astype(BF16)
    return hi, mid, lo


def _in_proj_kernel(x_ref, nw_ref, wm_ref, wba_ref, wbat_ref, cw_ref,
                    qkv_ref, z_ref, ba_ref, bat_ref, carry_scr):
    t = pl.program_id(1)

    @pl.when(t == 0)
    def _():
        carry_scr[...] = jnp.zeros_like(carry_scr)

    tm = x_ref.shape[0]
    hd = GDN_HEAD_DIM
    hn = (_rms_scale(x_ref[...]) * nw_ref[...]).astype(BF16)
    step = 2 * hd
    zevery = GDN_CONV_DIM // GDN_V_DIM
    for c0 in range(0, GDN_CONV_DIM, step):
        if (c0 // step) % zevery == 0:
            z0 = c0 // zevery
            z_ref[:, z0:z0 + step] = jnp.dot(
                hn, wm_ref[(GDN_CONV_DIM + z0) // step], preferred_element_type=F32)
        sl = slice(c0, c0 + step)
        cur = jnp.dot(hn, wm_ref[c0 // step], preferred_element_type=F32)
        prev8 = carry_scr[:, sl]
        carry_scr[:, sl] = cur[tm - SUBLANES:]
        for hc in range(c0, c0 + step, hd):
            lanes = slice(hc - c0, hc - c0 + hd)
            wts = [cw_ref[GDN_CONV_WIDTH - 1 - j:GDN_CONV_WIDTH - j, hc:hc + hd]
                   for j in range(GDN_CONV_WIDTH)]
            for r0 in range(0, tm, CONV_ROWS):
                piece = cur[r0:r0 + CONV_ROWS, lanes]
                before = prev8[:, lanes] if r0 == 0 else cur[r0 - SUBLANES:r0, lanes]
                yh = piece * wts[0]
                for j in range(1, GDN_CONV_WIDTH):
                    yh = yh + _shift_rows(piece, before, j) * wts[j]
                yh = _silu(yh)
                if hc < 2 * GDN_QK_DIM:
                    yh = yh * lax.rsqrt(jnp.sum(yh * yh, axis=-1, keepdims=True) + EPS)
                    if hc < GDN_QK_DIM:
                        yh = yh * (hd ** -0.5)
                qkv_ref[r0:r0 + CONV_ROWS, hc:hc + hd] = yh.astype(qkv_ref.dtype)
    ba_ref[...] = jnp.dot(hn, wba_ref[...], preferred_element_type=F32)
    bat_ref[...] = _dot_nt(wbat_ref[...], hn)


def _in_proj(x2, norm_w, w_main, w_ba, w_bat, conv_w, bsz, seq):
    m, k = x2.shape
    n = w_main.shape[1]
    w_main = w_main.reshape(k, n // (2 * GDN_HEAD_DIM), 2 * GDN_HEAD_DIM).transpose(1, 0, 2)
    nb = w_ba.shape[1]
    nbt = w_bat.shape[0]
    tm = TM_PROJ
    nt = seq // tm

    def row_map(b, t):
        return (b * nt + t, 0)

    return pl.pallas_call(
        _in_proj_kernel,
        grid=(bsz, nt),
        in_specs=[
            pl.BlockSpec((tm, k), row_map),
            _const_spec((1, k)),
            _const_spec(w_main.shape),
            _const_spec((k, nb)),
            _const_spec((nbt, k)),
            _const_spec((GDN_CONV_WIDTH, GDN_CONV_DIM)),
        ],
        out_specs=[
            pl.BlockSpec((tm, GDN_CONV_DIM), row_map),
            pl.BlockSpec((tm, GDN_V_DIM), row_map),
            pl.BlockSpec((tm, nb), row_map),
            pl.BlockSpec((nbt, tm), lambda b, t: (0, b * nt + t)),
        ],
        out_shape=[
            jax.ShapeDtypeStruct((m, GDN_CONV_DIM), BF16),
            jax.ShapeDtypeStruct((m, GDN_V_DIM), F32),
            jax.ShapeDtypeStruct((m, nb), F32),
            jax.ShapeDtypeStruct((nbt, m), F32),
        ],
        scratch_shapes=[pltpu.VMEM((SUBLANES, GDN_CONV_DIM), F32)],
        compiler_params=_params("parallel", "arbitrary"),
        name="gdn_in_proj",
    )(x2, norm_w, w_main, w_ba, w_bat, conv_w)


def _gdn_core_kernel(qkv_ref, z_ref, ba_ref, bat_ref, alog_ref, dtb_ref,
                     alogt_ref, dtbt_ref, onw_ref, out_ref,
                     s_scr, gcx_scr, betap_scr, rows_scr, u_scr, lhs_scr, a_scr, kt_scr):
    t = pl.program_id(1)
    tm = qkv_ref.shape[0]
    c = GDN_CHUNK
    nchunk = tm // c
    hd = GDN_HEAD_DIM
    nh = GDN_V_HEADS
    npair = GDN_QK_HEADS
    heads = range(nh)
    pairs = range(npair)

    @pl.when(t == 0)
    def _():
        s_scr[...] = jnp.zeros_like(s_scr)

    beta = 1.0 / (1.0 + jnp.exp(-ba_ref[:, :nh]))
    a_lin = ba_ref[:, nh:2 * nh] + dtb_ref[...]
    g = -jnp.exp(alog_ref[...]) * (jnp.maximum(a_lin, 0.0) + jnp.log1p(jnp.exp(-jnp.abs(a_lin))))
    row = lax.broadcasted_iota(jnp.int32, (tm, tm), 0)
    col = lax.broadcasted_iota(jnp.int32, (tm, tm), 1)
    same = (row // c) == (col // c)
    lblk = (same & (col <= row)).astype(BF16)
    ublk = (same & (row <= col)).astype(BF16)
    gc = jnp.dot(jnp.concatenate([lblk] * 3, axis=1), jnp.concatenate(_split3(g), axis=0),
                 preferred_element_type=F32)
    e_row = lax.broadcasted_iota(jnp.int32, (3 * nh, npair * 2 * c), 0) % nh
    e_col = lax.broadcasted_iota(jnp.int32, (3 * nh, npair * 2 * c), 1)
    expand = (e_row == e_col // c).astype(BF16)
    betap_scr[...] = jnp.dot(jnp.concatenate(_split3(beta), axis=1), expand,
                             preferred_element_type=F32)
    e_row = lax.broadcasted_iota(jnp.int32, (3 * nh, nh * hd), 0) % nh
    e_col = lax.broadcasted_iota(jnp.int32, (3 * nh, nh * hd), 1)
    expand_hd = (e_row == e_col // hd).astype(BF16)
    gcx_scr[...] = jnp.dot(jnp.concatenate(_split3(gc), axis=1), expand_hd,
                           preferred_element_type=F32)

    beta_t = 1.0 / (1.0 + jnp.exp(-bat_ref[:nh, :]))
    a_lin_t = bat_ref[nh:, :] + dtbt_ref[...]
    g_t = -jnp.exp(alogt_ref[...]) * (jnp.maximum(a_lin_t, 0.0)
                                      + jnp.log1p(jnp.exp(-jnp.abs(a_lin_t))))
    gct = jnp.dot(jnp.concatenate(_split3(g_t), axis=1), jnp.concatenate([ublk] * 3, axis=0),
                  preferred_element_type=F32)
    begct = beta_t * jnp.exp(gct)
    for ci in range(nchunk):
        for slot, arr in enumerate((gct, beta_t, begct)):
            rows_scr[slot, ci] = jnp.concatenate([arr[:npair, ci * c:(ci + 1) * c],
                                                  arr[npair:, ci * c:(ci + 1) * c]], axis=1)

    ri = lax.broadcasted_iota(jnp.int32, (c, 2 * c), 0)
    cj = lax.broadcasted_iota(jnp.int32, (c, 2 * c), 1) % c
    left = lax.broadcasted_iota(jnp.int32, (1, 2 * c), 1) < c
    tril = cj <= ri
    strict = cj < ri
    eye = (ri == cj).astype(F32)
    onw = onw_ref[...]
    zero_hd = jnp.zeros((c, hd), BF16)

    def block_diag(xc):
        xb = xc.astype(BF16)
        zero = jnp.zeros_like(xb)
        return jnp.concatenate([jnp.where(left, xb, zero), jnp.where(left, zero, xb)], axis=0)

    def two_blocks(top, bottom):
        return jnp.concatenate([jnp.concatenate([top, zero_hd], axis=1),
                                jnp.concatenate([zero_hd, bottom], axis=1)], axis=0)

    def prepare(chunk_ids):
        items = [(n, j) for n in range(len(chunk_ids)) for j in pairs]
        rows = [pl.ds(pl.multiple_of(ci * c, c), c) for ci in chunk_ids]
        q = {(n, j): qkv_ref[rows[n], j * hd:(j + 1) * hd] for n, j in items}
        k = {(n, j): qkv_ref[rows[n], GDN_QK_DIM + j * hd:GDN_QK_DIM + (j + 1) * hd]
             for n, j in items}
        gq = {i: _dot_nt(jnp.concatenate([k[i], q[i]], axis=0), jnp.concatenate([k[i], k[i]], axis=0))
              for i in items}
        dec = {}
        for n, j in items:
            gc_pair = jnp.where(left, gcx_scr[rows[n], 2 * j * hd:2 * j * hd + 2 * c],
                                gcx_scr[rows[n], (2 * j + 1) * hd:(2 * j + 1) * hd + 2 * c])
            diff = gc_pair - rows_scr[0, chunk_ids[n], pl.ds(j, 1), :]
            dec[n, j] = jnp.where(tril, jnp.exp(jnp.where(tril, diff, 0.0)), 0.0)
        m = {(n, j): jnp.where(strict, betap_scr[rows[n], j * 2 * c:(j + 1) * 2 * c]
                               * gq[n, j][:c] * dec[n, j], 0.0) for n, j in items}
        for n, j in items:
            a_scr[chunk_ids[n], j] = (gq[n, j][c:] * dec[n, j]).astype(BF16)
        x = {i: eye - m[i] for i in items}
        p = {i: _dot(m[i], block_diag(m[i])) for i in items}
        levels = int(math.log2(c)) - 1
        for lvl in range(levels):
            if lvl + 1 < levels:
                xp = {i: _dot(jnp.concatenate([x[i], p[i]], axis=0), block_diag(p[i])) for i in items}
                x = {i: x[i] + xp[i][:c] for i in items}
                p = {i: xp[i][c:] for i in items}
            else:
                xp = {i: _dot(x[i], block_diag(p[i])) for i in items}
                x = {i: x[i] + xp[i] for i in items}
        u, w = {}, {}
        for n, j in items:
            ci = chunk_ids[n]
            t_u = x[n, j] * rows_scr[1, ci, pl.ds(j, 1), :]
            t_w = x[n, j] * rows_scr[2, ci, pl.ds(j, 1), :]
            v_a = qkv_ref[rows[n], 2 * GDN_QK_DIM + 2 * j * hd:2 * GDN_QK_DIM + (2 * j + 1) * hd]
            v_b = qkv_ref[rows[n], 2 * GDN_QK_DIM + (2 * j + 1) * hd:2 * GDN_QK_DIM + (2 * j + 2) * hd]
            u[n, j] = _dot(t_u, two_blocks(v_a, v_b))
            w[n, j] = _dot(t_w, two_blocks(k[n, j], k[n, j]))
        for n, j in items:
            ci = chunk_ids[n]
            for e in range(2):
                h = 2 * j + e
                u_scr[ci, h] = u[n, j][:, e * hd:(e + 1) * hd]
                lhs_scr[ci, h] = jnp.concatenate(
                    [w[n, j][:, e * hd:(e + 1) * hd].astype(BF16), q[n, j]], axis=0)
            if j % 2 == 0:
                k2 = qkv_ref[rows[n], GDN_QK_DIM + j * hd:GDN_QK_DIM + (j + 2) * hd]
                kt_scr[ci, j // 2] = jnp.concatenate(
                    [k2[:, :hd], k2[:, hd:]], axis=0).astype(F32).T.astype(BF16)

    def prepare_body(i, carry):
        prepare([i * PREP_CHUNKS + n for n in range(PREP_CHUNKS)])
        return carry

    lax.fori_loop(0, nchunk // PREP_CHUNKS, prepare_body, 0)

    def state_body(ci, carry):
        r0 = pl.multiple_of(ci * c, c)
        rows = pl.ds(r0, c)
        gcx = [gcx_scr[rows, h * hd:(h + 1) * hd] for h in heads]
        egc = [jnp.exp(gcx[h]) for h in heads]
        ekd = [jnp.exp(gcx[h][c - 1:c, :] - gcx[h]) for h in heads]
        s_old = [s_scr[j] for j in pairs]
        pq = [_dot(lhs_scr[ci, h], s_old[h // 2][:, (h % 2) * hd:(h % 2 + 1) * hd])
              for h in heads]
        vn = [u_scr[ci, h] - pq[h][:c] for h in heads]
        vnb = [vn[h].astype(BF16) for h in heads]
        avn = [_dot(a_scr[ci, j], two_blocks(vnb[2 * j], vnb[2 * j + 1])) for j in pairs]
        vs = [jnp.concatenate([vn[2 * j] * ekd[2 * j], vn[2 * j + 1] * ekd[2 * j + 1]],
                              axis=1).astype(BF16) for j in pairs]
        zero_vs = jnp.zeros_like(vs[0])
        ds = [_dot(kt_scr[ci, j // 2], jnp.concatenate(
            [vs[j], zero_vs] if j % 2 == 0 else [zero_vs, vs[j]], axis=0)) for j in pairs]
        for j in pairs:
            egl = jnp.exp(jnp.concatenate([gcx[2 * j][c - 1:c, :], gcx[2 * j + 1][c - 1:c, :]], axis=1))
            s_scr[j] = s_old[j] * egl + ds[j]
        for h in heads:
            o_h = (pq[h][c:] * egc[h]
                   + avn[h // 2][:, (h % 2) * hd:(h % 2 + 1) * hd])
            z_h = z_ref[rows, h * hd:(h + 1) * hd]
            out_ref[rows, h * hd:(h + 1) * hd] = (
                _rms_scale(o_h) * onw * _silu(z_h)).astype(out_ref.dtype)
        return carry

    lax.fori_loop(0, nchunk, state_body, 0)


def _gdn_core(qkv, z, ba, bat, a_log, dt_bias, out_norm_w, head_order, bsz, seq):
    m = qkv.shape[0]
    tm = TM_GDN
    nt = seq // tm
    nh = GDN_V_HEADS
    nb = ba.shape[1]
    nchunk = tm // GDN_CHUNK

    def cur_map(b, t):
        return (b * nt + t, 0)

    return pl.pallas_call(
        _gdn_core_kernel,
        grid=(bsz, nt),
        in_specs=[
            pl.BlockSpec((tm, GDN_CONV_DIM), cur_map),
            pl.BlockSpec((tm, GDN_V_DIM), cur_map),
            pl.BlockSpec((tm, nb), cur_map),
            pl.BlockSpec((2 * nh, tm), lambda b, t: (0, b * nt + t)),
            _const_spec((1, nh)),
            _const_spec((1, nh)),
            _const_spec((nh, 1)),
            _const_spec((nh, 1)),
            _const_spec((1, GDN_HEAD_DIM)),
        ],
        out_specs=pl.BlockSpec((tm, GDN_V_DIM), cur_map),
        out_shape=jax.ShapeDtypeStruct((m, GDN_V_DIM), BF16),
        scratch_shapes=[
            pltpu.VMEM((GDN_QK_HEADS, GDN_HEAD_DIM, 2 * GDN_HEAD_DIM), F32),
            pltpu.VMEM((tm, nh * GDN_HEAD_DIM), F32),
            pltpu.VMEM((tm, nh * GDN_CHUNK), F32),
            pltpu.VMEM((3, nchunk, GDN_QK_HEADS, 2 * GDN_CHUNK), F32),
            pltpu.VMEM((nchunk, nh, GDN_CHUNK, GDN_HEAD_DIM), F32),
            pltpu.VMEM((nchunk, nh, 2 * GDN_CHUNK, GDN_HEAD_DIM), BF16),
            pltpu.VMEM((nchunk, GDN_QK_HEADS, GDN_CHUNK, 2 * GDN_CHUNK), BF16),
            pltpu.VMEM((nchunk, GDN_QK_HEADS // 2, GDN_HEAD_DIM, 2 * GDN_CHUNK), BF16),
        ],
        compiler_params=_params("parallel", "arbitrary"),
        name="gdn_core",
    )(qkv, z, ba, bat, a_log.reshape(1, nh), dt_bias.reshape(1, nh),
      a_log[head_order].reshape(nh, 1), dt_bias[head_order].reshape(nh, 1),
      out_norm_w.reshape(1, GDN_HEAD_DIM))


def _ffn_kernel(res_ref, x_ref, wpre_ref, nw_ref, wup_ref, cw_ref, cb_ref, wdn_ref, fnw_ref, out_ref,
                carry_scr, act_scr, h_scr, *, final_norm):
    t = pl.program_id(1)

    @pl.when(t == 0)
    def _():
        carry_scr[...] = jnp.zeros_like(carry_scr)

    nsub = res_ref.shape[0] // FFN_SUB

    def up_phase(si):
        rows = slice(si * FFN_SUB, (si + 1) * FFN_SUB)
        h = res_ref[rows, :] + jnp.dot(x_ref[rows, :], wpre_ref[...], preferred_element_type=F32)
        h_scr[si] = h
        hn = (_rms_scale(h) * nw_ref[...]).astype(BF16)
        for c0 in range(0, D_FF, FFN_CHUNK):
            halves = []
            for off in (0, D_FF):
                sl = slice(off + c0, off + c0 + FFN_CHUNK)
                cur = jnp.dot(hn, wup_ref[:, sl], preferred_element_type=F32)
                prev8 = carry_scr[:, sl]
                carry_scr[:, sl] = cur[FFN_SUB - SUBLANES:]
                y = cur * cw_ref[FFN_CONV_WIDTH - 1:FFN_CONV_WIDTH, sl] + cb_ref[:, sl]
                for j in range(1, FFN_CONV_WIDTH):
                    y = y + (_shift_rows(cur, prev8, j)
                             * cw_ref[FFN_CONV_WIDTH - 1 - j:FFN_CONV_WIDTH - j, sl])
                halves.append(y)
            act_scr[si, :, c0:c0 + FFN_CHUNK] = (_silu(halves[0]) * halves[1]).astype(BF16)

    def down_phase(si):
        rows = slice(si * FFN_SUB, (si + 1) * FFN_SUB)
        act = act_scr[si]
        res = h_scr[si] + jnp.dot(act, wdn_ref[...], preferred_element_type=F32)
        if final_norm:
            res = _rms_scale(res) * fnw_ref[...]
        out_ref[rows, :] = res

    up_phase(0)
    for si in range(1, nsub):
        up_phase(si)
        down_phase(si - 1)
    down_phase(nsub - 1)


def _ffn(res, x2, w_pre, norm_w, w_up, conv_w, conv_b, w_down, final_w, bsz, seq, final_norm):
    m, k = res.shape
    tm = TM_FFN
    nt = seq // tm

    def row_map(b, t):
        return (b * nt + t, 0)

    return pl.pallas_call(
        functools.partial(_ffn_kernel, final_norm=final_norm),
        grid=(bsz, nt),
        in_specs=[
            pl.BlockSpec((tm, k), row_map),
            pl.BlockSpec((tm, x2.shape[1]), row_map),
            _resident_spec(w_pre.shape),
            _const_spec((1, k)),
            _resident_spec(w_up.shape),
            _const_spec((FFN_CONV_WIDTH, 2 * D_FF)),
            _const_spec((1, 2 * D_FF)),
            _resident_spec(w_down.shape),
            _const_spec((1, k)),
        ],
        out_specs=pl.BlockSpec((tm, k), row_map),
        out_shape=jax.ShapeDtypeStruct((m, k), F32),
        scratch_shapes=[
            pltpu.VMEM((SUBLANES, 2 * D_FF), F32),
            pltpu.VMEM((tm // FFN_SUB, FFN_SUB, D_FF), BF16),
            pltpu.VMEM((tm // FFN_SUB, FFN_SUB, k), F32),
        ],
        compiler_params=_params("parallel", "arbitrary"),
        name="ffn_final" if final_norm else "ffn",
    )(res, x2, w_pre, norm_w, w_up, conv_w, conv_b, w_down, final_w)


def _qkv_proj_kernel(h_ref, qnw_ref, kvnw_ref, wq_ref, wkv_ref, q_ref, kv_ref):
    y = _rms_scale(h_ref[...])
    q = jnp.dot((y * qnw_ref[...]).astype(BF16), wq_ref[...], preferred_element_type=F32)
    q_ref[...] = (q * (SWA_HEAD_DIM ** -0.5)).astype(q_ref.dtype)
    kv_ref[...] = jnp.dot((y * kvnw_ref[...]).astype(BF16), wkv_ref[...],
                          preferred_element_type=F32).astype(kv_ref.dtype)


def _qkv_proj(h2, q_norm_w, kv_norm_w, w_q, w_kv):
    m, k = h2.shape
    nq = w_q.shape[1]
    nkv = w_kv.shape[1]
    tm = TM_MM
    return pl.pallas_call(
        _qkv_proj_kernel,
        grid=(m // tm,),
        in_specs=[
            pl.BlockSpec((tm, k), lambda i: (i, 0)),
            _const_spec((1, k)),
            _const_spec((1, k)),
            _const_spec((k, nq)),
            _const_spec((k, nkv)),
        ],
        out_specs=[
            pl.BlockSpec((tm, nq), lambda i: (i, 0)),
            pl.BlockSpec((tm, nkv), lambda i: (i, 0)),
        ],
        out_shape=[
            jax.ShapeDtypeStruct((m, nq), BF16),
            jax.ShapeDtypeStruct((m, nkv), BF16),
        ],
        compiler_params=_params("parallel"),
        name="qkv_proj",
    )(h2, q_norm_w, kv_norm_w, w_q, w_kv)


def _t5_bucket_map():
    qi = np.arange(SWA_BLOCK)[:, None]
    ki = np.arange(2 * SWA_BLOCK)[None, :]
    dist = qi + SWA_BLOCK - ki
    n = np.maximum(dist, 0)
    max_exact = REL_BUCKETS // 2
    nf = np.maximum(n, 1).astype(np.float32)
    large = max_exact + (np.log(nf / np.float32(max_exact)).astype(np.float32)
                         / np.float32(math.log(REL_MAX_DISTANCE / max_exact))
                         * np.float32(REL_BUCKETS - max_exact)).astype(np.int32)
    large = np.minimum(large, REL_BUCKETS - 1)
    bucket = np.where(n < max_exact, n, large).astype(np.int32)
    in_window = (dist >= 0) & (dist < SWA_WINDOW)
    return np.where(in_window, bucket, -1).astype(np.int32)


def _swa_kernel(table_ref, sinks_ref, q_ref, kvc_ref, kvp_ref, bmap_ref, out_ref, bias_scr):
    n = pl.program_id(1)
    blk = SWA_BLOCK
    hd = SWA_HEAD_DIM
    kvd = SWA_KV_HEADS * hd

    @pl.when((pl.program_id(0) == 0) & (n == 0))
    def _():
        bmap = bmap_ref[...]
        for hq in range(SWA_Q_HEADS):
            bias_scr[0, hq] = jnp.where(bmap < 0, NEG_INF, 0.0)

        def bucket_body(b, carry):
            hit = bmap == b
            for hq in range(SWA_Q_HEADS):
                bias_scr[0, hq] = jnp.where(hit, table_ref[b, hq], bias_scr[0, hq])
            return carry

        lax.fori_loop(0, REL_BUCKETS, bucket_body, 0)
        key_col = lax.broadcasted_iota(jnp.int32, (blk, 2 * blk), 1)
        for hq in range(SWA_Q_HEADS):
            bias_scr[1, hq] = jnp.where(key_col >= blk, bias_scr[0, hq], NEG_INF)
        for hq in range(SWA_Q_HEADS):
            for tbl in range(2):
                bias_scr[tbl, hq] = jnp.where(key_col == 0, sinks_ref[hq], bias_scr[tbl, hq])

    first = (n == 0).astype(jnp.int32)
    kv_band = jnp.concatenate([kvp_ref[...], kvc_ref[...]], axis=0)
    lane = lax.broadcasted_iota(jnp.int32, (1, LANES), 1)
    lo = lane < hd
    ones_blk = jnp.ones((2 * blk, LANES), BF16)
    not_slot0 = lax.broadcasted_iota(jnp.int32, (2 * blk, 1), 0) > 0
    kv_heads = range(SWA_KV_HEADS)
    k_dup, v_aug, q_stack = [], [], []
    for h in kv_heads:
        grp = (h * hd) // LANES
        k_grp = kv_band[:, grp * LANES:(grp + 1) * LANES]
        v_grp = kv_band[:, kvd + grp * LANES:kvd + (grp + 1) * LANES]
        mine = (lo if (h * hd) % LANES == 0 else ~lo) & not_slot0
        k_own = jnp.where(mine, k_grp, 0.0).astype(F32)
        v_own = jnp.where(mine, v_grp, 0.0).astype(F32)
        k_dup.append((k_own + pltpu.roll(k_own, hd, 1)).astype(BF16))
        v_aug.append(jnp.concatenate(
            [(v_own + pltpu.roll(v_own, hd, 1)).astype(BF16), ones_blk], axis=1))
        q_rows = []
        for gidx in range(SWA_GROUP):
            hq = h * SWA_GROUP + gidx
            qg = q_ref[:, (hq * hd) // LANES * LANES:((hq * hd) // LANES + 1) * LANES]
            sel = lo if (hq * hd) % LANES == 0 else ~lo
            q_rows.append(jnp.where(sel, qg, jnp.zeros_like(qg)))
        q_stack.append(jnp.concatenate(q_rows, axis=0))
    s = [_dot_nt(q_stack[h], k_dup[h]) for h in kv_heads]
    sb = [s[h] + bias_scr[first, pl.ds(h * SWA_GROUP, SWA_GROUP)].reshape(SWA_GROUP * blk, 2 * blk)
          for h in kv_heads]
    mx = [jnp.max(sb[h], axis=-1, keepdims=True) for h in kv_heads]
    p = [jnp.exp(sb[h] - mx[h]).astype(BF16) for h in kv_heads]
    oa = [jnp.dot(p[h], v_aug[h], preferred_element_type=F32) for h in kv_heads]
    outs = [oa[h][:, :LANES] / oa[h][:, LANES:] for h in kv_heads]
    for h in kv_heads:
        for pair in range(SWA_GROUP // 2):
            hq0 = h * SWA_GROUP + 2 * pair
            col = (hq0 * hd) // LANES * LANES
            even = outs[h][(2 * pair) * blk:(2 * pair + 1) * blk]
            odd = outs[h][(2 * pair + 1) * blk:(2 * pair + 2) * blk]
            out_ref[:, col:col + LANES] = jnp.where(lo, even, odd).astype(out_ref.dtype)


def _swa(q, kv, rel_table, sinks, bsz, seq):
    m = q.shape[0]
    nb = seq // SWA_BLOCK
    bmap = jnp.asarray(_t5_bucket_map())
    grid_spec = pltpu.PrefetchScalarGridSpec(
        num_scalar_prefetch=0,
        grid=(bsz, nb),
        in_specs=[
            pl.BlockSpec(memory_space=pltpu.SMEM),
            pl.BlockSpec(memory_space=pltpu.SMEM),
            pl.BlockSpec((SWA_BLOCK, q.shape[1]), lambda b, n: (b * nb + n, 0)),
            pl.BlockSpec((SWA_BLOCK, kv.shape[1]), lambda b, n: (b * nb + n, 0)),
            pl.BlockSpec((SWA_BLOCK, kv.shape[1]),
                         lambda b, n: (b * nb + jnp.maximum(n - 1, 0), 0)),
            _const_spec((SWA_BLOCK, 2 * SWA_BLOCK)),
        ],
        out_specs=pl.BlockSpec((SWA_BLOCK, q.shape[1]), lambda b, n: (b * nb + n, 0)),
        scratch_shapes=[pltpu.VMEM((2, SWA_Q_HEADS, SWA_BLOCK, 2 * SWA_BLOCK), F32)],
    )
    return pl.pallas_call(
        _swa_kernel,
        grid_spec=grid_spec,
        out_shape=jax.ShapeDtypeStruct((m, q.shape[1]), BF16),
        compiler_params=_params("arbitrary", "arbitrary"),
        name="swa",
    )(rel_table, sinks, q, kv, kv, bmap)


def kernel(x, a_norm_w, a_w_in, a_conv_w, a_a_log, a_dt_bias, a_out_norm_w, a_w_out,
           kv_norm_w, w_kv, b_norm_w, b_w_q, b_sinks, b_w_o, rel_bias_table,
           ffn_norm_w, ffn_w_up, ffn_conv_w, ffn_conv_b, ffn_w_down, final_norm_w):
    bsz, seq, d = x.shape
    n_a = a_w_in.shape[0]
    n_b = b_w_q.shape[0]
    depth = n_a + n_b
    h = x.reshape(bsz * seq, d)
    kv = None
    head_order = np.concatenate([np.arange(0, GDN_V_HEADS, 2), np.arange(1, GDN_V_HEADS, 2)])
    for layer in range(depth):
        if layer < n_a:
            i = layer
            w_in = a_w_in[i]
            w_ba = w_in[:, GDN_MAIN_DIM:].astype(BF16)
            w_at = w_ba[:, np.concatenate([head_order, GDN_V_HEADS + head_order])].T
            qkv, z, ba, bat = _in_proj(h, a_norm_w[i].reshape(1, d),
                                       w_in[:, :GDN_MAIN_DIM].astype(BF16), w_ba, w_at,
                                       a_conv_w[i], bsz, seq)
            o = _gdn_core(qkv, z, ba, bat, a_a_log[i], a_dt_bias[i], a_out_norm_w[i],
                          head_order, bsz, seq)
            w_mix = a_w_out[i].astype(BF16)
        else:
            j = layer - n_a
            if j == 0:
                q, kv = _qkv_proj(h, b_norm_w[j].reshape(1, d), kv_norm_w.reshape(1, d),
                                  b_w_q[j].astype(BF16), w_kv.astype(BF16))
            else:
                q, _ = _qkv_proj(h, b_norm_w[j].reshape(1, d), kv_norm_w.reshape(1, d),
                                 b_w_q[j].astype(BF16), w_kv.astype(BF16))
            o = _swa(q, kv, rel_bias_table, b_sinks[j], bsz, seq)
            w_mix = b_w_o[j].astype(BF16)
        last = layer == depth - 1
        h = _ffn(h, o, w_mix, ffn_norm_w[layer].reshape(1, d), ffn_w_up[layer].astype(BF16),
                 ffn_conv_w[layer], ffn_conv_b[layer].reshape(1, 2 * D_FF),
                 ffn_w_down[layer].astype(BF16), final_norm_w.reshape(1, d), bsz, seq, last)
    return h.reshape(bsz, seq, d)
```

```python
import functools
import math

import jax
import jax.numpy as jnp
import numpy as np
from jax import lax
from jax.experimental import pallas as pl
from jax.experimental.pallas import tpu as pltpu

F32 = jnp.float32
BF16 = jnp.bfloat16

EPS = 1e-6
NEG_INF = -1e30

D_MODEL = 1024
GDN_QK_HEADS = 8
GDN_V_HEADS = 16
GDN_HEAD_DIM = 128
GDN_QK_DIM = GDN_QK_HEADS * GDN_HEAD_DIM
GDN_V_DIM = GDN_V_HEADS * GDN_HEAD_DIM
GDN_CONV_DIM = 2 * GDN_QK_DIM + GDN_V_DIM
GDN_MAIN_DIM = GDN_CONV_DIM + GDN_V_DIM
GDN_CONV_WIDTH = 4
GDN_CHUNK = 64

SWA_Q_HEADS = 16
SWA_KV_HEADS = 4
SWA_GROUP = SWA_Q_HEADS // SWA_KV_HEADS
SWA_HEAD_DIM = 64
SWA_WINDOW = 128
SWA_BLOCK = 128
SWA_STEP = 512
REL_BUCKETS = 32
REL_MAX_DISTANCE = 128

D_FF = 2816
FFN_CONV_WIDTH = 3
FFN_CHUNK = 256
CONV_ROWS = 64
PREP_CHUNKS = 4

SUBLANES = 8
LANES = 128
VMEM_LIMIT_BYTES = 56 * 1024 * 1024

TM_PROJ = 256
TM_GDN = 256
TM_FFN = 512
FFN_SUB = 256
TM_MM = 512


def _params(*sem):
    return pltpu.CompilerParams(dimension_semantics=sem, vmem_limit_bytes=VMEM_LIMIT_BYTES)


def _const_spec(shape):
    zeros = (0,) * len(shape)
    return pl.BlockSpec(shape, lambda *_: zeros)


def _resident_spec(shape):
    zeros = (0,) * len(shape)
    return pl.BlockSpec(shape, lambda *_: zeros, pipeline_mode=pl.Buffered(1))


def _dot(a, b):
    return jnp.dot(a.astype(BF16), b.astype(BF16), preferred_element_type=F32)


def _dot_nt(a, b):
    return lax.dot_general(a.astype(BF16), b.astype(BF16), (((1,), (1,)), ((), ())),
                           preferred_element_type=F32)


def _rms_scale(x):
    return x * lax.rsqrt(jnp.mean(x * x, axis=-1, keepdims=True) + EPS)


def _silu(x):
    half = 0.5 * x
    return half + half * jnp.tanh(half)


def _shift_rows(cur, prev8, j):
    if j == 0:
        return cur
    nrow, ncol = cur.shape
    tiles = jnp.concatenate([prev8, cur], axis=0).reshape(nrow // SUBLANES + 1, SUBLANES, ncol)
    rot = pltpu.roll(tiles, j, 1)
    row = lax.broadcasted_iota(jnp.int32, (1, SUBLANES, ncol), 1)
    return jnp.where(row < j, rot[:-1], rot[1:]).reshape(nrow, ncol)


def _split3(x):
    hi = x.astype(BF16)
    r1 = x - hi.astype(F32)
    mid = r1.astype(BF16)
    lo = (r1 - mid.astype(F32)).astype(BF16)
    return hi, mid, lo


def _in_proj_kernel(x_ref, nw_ref, wm_ref, wba_ref, wbat_ref, cw_ref,
                    qkv_ref, z_ref, ba_ref, bat_ref, carry_scr):
    t = pl.program_id(1)

    @pl.when(t == 0)
    def _():
        carry_scr[...] = jnp.zeros_like(carry_scr)

    tm = x_ref.shape[0]
    hd = GDN_HEAD_DIM
    hn = (_rms_scale(x_ref[...]) * nw_ref[...]).astype(BF16)
    step = 2 * hd
    zevery = GDN_CONV_DIM // GDN_V_DIM
    for c0 in range(0, GDN_CONV_DIM, step):
        if (c0 // step) % zevery == 0:
            z0 = c0 // zevery
            z_ref[:, z0:z0 + step] = jnp.dot(
                hn, wm_ref[(GDN_CONV_DIM + z0) // step], preferred_element_type=F32)
        sl = slice(c0, c0 + step)
        cur = jnp.dot(hn, wm_ref[c0 // step], preferred_element_type=F32)
        prev8 = carry_scr[:, sl]
        carry_scr[:, sl] = cur[tm - SUBLANES:]
        for hc in range(c0, c0 + step, hd):
            lanes = slice(hc - c0, hc - c0 + hd)
            wts = [cw_ref[GDN_CONV_WIDTH - 1 - j:GDN_CONV_WIDTH - j, hc:hc + hd]
                   for j in range(GDN_CONV_WIDTH)]
            for r0 in range(0, tm, CONV_ROWS):
                piece = cur[r0:r0 + CONV_ROWS, lanes]
                before = prev8[:, lanes] if r0 == 0 else cur[r0 - SUBLANES:r0, lanes]
                yh = piece * wts[0]
                for j in range(1, GDN_CONV_WIDTH):
                    yh = yh + _shift_rows(piece, before, j) * wts[j]
                yh = _silu(yh)
                if hc < 2 * GDN_QK_DIM:
                    yh = yh * lax.rsqrt(jnp.sum(yh * yh, axis=-1, keepdims=True) + EPS)
                    if hc < GDN_QK_DIM:
                        yh = yh * (hd ** -0.5)
                qkv_ref[r0:r0 + CONV_ROWS, hc:hc + hd] = yh.astype(qkv_ref.dtype)
    ba_ref[...] = jnp.dot(hn, wba_ref[...], preferred_element_type=F32)
    bat_ref[...] = _dot_nt(wbat_ref[...], hn)


def _in_proj(x2, norm_w, w_main, w_ba, w_bat, conv_w, bsz, seq):
    m, k = x2.shape
    n = w_main.shape[1]
    w_main = w_main.reshape(k, n // (2 * GDN_HEAD_DIM), 2 * GDN_HEAD_DIM).transpose(1, 0, 2)
    nb = w_ba.shape[1]
    nbt = w_bat.shape[0]
    tm = TM_PROJ
    nt = seq // tm

    def row_map(b, t):
        return (b * nt + t, 0)

    return pl.pallas_call(
        _in_proj_kernel,
        grid=(bsz, nt),
        in_specs=[
            pl.BlockSpec((tm, k), row_map),
            _const_spec((1, k)),
            _const_spec(w_main.shape),
            _const_spec((k, nb)),
            _const_spec((nbt, k)),
            _const_spec((GDN_CONV_WIDTH, GDN_CONV_DIM)),
        ],
        out_specs=[
            pl.BlockSpec((tm, GDN_CONV_DIM), row_map),
            pl.BlockSpec((tm, GDN_V_DIM), row_map),
            pl.BlockSpec((tm, nb), row_map),
            pl.BlockSpec((nbt, tm), lambda b, t: (0, b * nt + t)),
        ],
        out_shape=[
            jax.ShapeDtypeStruct((m, GDN_CONV_DIM), BF16),
            jax.ShapeDtypeStruct((m, GDN_V_DIM), F32),
            jax.ShapeDtypeStruct((m, nb), F32),
            jax.ShapeDtypeStruct((nbt, m), F32),
        ],
        scratch_shapes=[pltpu.VMEM((SUBLANES, GDN_CONV_DIM), F32)],
        compiler_params=_params("parallel", "arbitrary"),
        name="gdn_in_proj",
    )(x2, norm_w, w_main, w_ba, w_bat, conv_w)


def _gdn_core_kernel(qkv_ref, z_ref, ba_ref, bat_ref, alog_ref, dtb_ref,
                     alogt_ref, dtbt_ref, onw_ref, out_ref,
                     s_scr, gcx_scr, betap_scr, rows_scr, u_scr, lhs_scr, a_scr, kt_scr):
    t = pl.program_id(1)
    tm = qkv_ref.shape[0]
    c = GDN_CHUNK
    nchunk = tm // c
    hd = GDN_HEAD_DIM
    nh = GDN_V_HEADS
    npair = GDN_QK_HEADS
    heads = range(nh)
    pairs = range(npair)

    @pl.when(t == 0)
    def _():
        s_scr[...] = jnp.zeros_like(s_scr)

    beta = 1.0 / (1.0 + jnp.exp(-ba_ref[:, :nh]))
    a_lin = ba_ref[:, nh:2 * nh] + dtb_ref[...]
    g = -jnp.exp(alog_ref[...]) * (jnp.maximum(a_lin, 0.0) + jnp.log1p(jnp.exp(-jnp.abs(a_lin))))
    row = lax.broadcasted_iota(jnp.int32, (tm, tm), 0)
    col = lax.broadcasted_iota(jnp.int32, (tm, tm), 1)
    same = (row // c) == (col // c)
    lblk = (same & (col <= row)).astype(BF16)
    ublk = (same & (row <= col)).astype(BF16)
    gc = jnp.dot(jnp.concatenate([lblk] * 3, axis=1), jnp.concatenate(_split3(g), axis=0),
                 preferred_element_type=F32)
    e_row = lax.broadcasted_iota(jnp.int32, (3 * nh, npair * 2 * c), 0) % nh
    e_col = lax.broadcasted_iota(jnp.int32, (3 * nh, npair * 2 * c), 1)
    expand = (e_row == e_col // c).astype(BF16)
    betap_scr[...] = jnp.dot(jnp.concatenate(_split3(beta), axis=1), expand,
                             preferred_element_type=F32)
    e_row = lax.broadcasted_iota(jnp.int32, (3 * nh, nh * hd), 0) % nh
    e_col = lax.broadcasted_iota(jnp.int32, (3 * nh, nh * hd), 1)
    expand_hd = (e_row == e_col // hd).astype(BF16)
    gcx_scr[...] = jnp.dot(jnp.concatenate(_split3(gc), axis=1), expand_hd,
                           preferred_element_type=F32)

    beta_t = 1.0 / (1.0 + jnp.exp(-bat_ref[:nh, :]))
    a_lin_t = bat_ref[nh:, :] + dtbt_ref[...]
    g_t = -jnp.exp(alogt_ref[...]) * (jnp.maximum(a_lin_t, 0.0)
                                      + jnp.log1p(jnp.exp(-jnp.abs(a_lin_t))))
    gct = jnp.dot(jnp.concatenate(_split3(g_t), axis=1), jnp.concatenate([ublk] * 3, axis=0),
                  preferred_element_type=F32)
    begct = beta_t * jnp.exp(gct)
    for ci in range(nchunk):
        for slot, arr in enumerate((gct, beta_t, begct)):
            rows_scr[slot, ci] = jnp.concatenate([arr[:npair, ci * c:(ci + 1) * c],
                                                  arr[npair:, ci * c:(ci + 1) * c]], axis=1)

    ri = lax.broadcasted_iota(jnp.int32, (c, 2 * c), 0)
    cj = lax.broadcasted_iota(jnp.int32, (c, 2 * c), 1) % c
    left = lax.broadcasted_iota(jnp.int32, (1, 2 * c), 1) < c
    tril = cj <= ri
    strict = cj < ri
    eye = (ri == cj).astype(F32)
    onw = onw_ref[...]
    zero_hd = jnp.zeros((c, hd), BF16)

    def block_diag(xc):
        xb = xc.astype(BF16)
        zero = jnp.zeros_like(xb)
        return jnp.concatenate([jnp.where(left, xb, zero), jnp.where(left, zero, xb)], axis=0)

    def two_blocks(top, bottom):
        return jnp.concatenate([jnp.concatenate([top, zero_hd], axis=1),
                                jnp.concatenate([zero_hd, bottom], axis=1)], axis=0)

    def prepare(chunk_ids):
        items = [(n, j) for n in range(len(chunk_ids)) for j in pairs]
        rows = [pl.ds(pl.multiple_of(ci * c, c), c) for ci in chunk_ids]
        q = {(n, j): qkv_ref[rows[n], j * hd:(j + 1) * hd] for n, j in items}
        k = {(n, j): qkv_ref[rows[n], GDN_QK_DIM + j * hd:GDN_QK_DIM + (j + 1) * hd]
             for n, j in items}
        gq = {i: _dot_nt(jnp.concatenate([k[i], q[i]], axis=0), jnp.concatenate([k[i], k[i]], axis=0))
              for i in items}
        dec = {}
        for n, j in items:
            gc_pair = jnp.where(left, gcx_scr[rows[n], 2 * j * hd:2 * j * hd + 2 * c],
                                gcx_scr[rows[n], (2 * j + 1) * hd:(2 * j + 1) * hd + 2 * c])
            diff = gc_pair - rows_scr[0, chunk_ids[n], pl.ds(j, 1), :]
            dec[n, j] = jnp.where(tril, jnp.exp(jnp.where(tril, diff, 0.0)), 0.0)
        m = {(n, j): jnp.where(strict, betap_scr[rows[n], j * 2 * c:(j + 1) * 2 * c]
                               * gq[n, j][:c] * dec[n, j], 0.0) for n, j in items}
        for n, j in items:
            a_scr[chunk_ids[n], j] = (gq[n, j][c:] * dec[n, j]).astype(BF16)
        x = {i: eye - m[i] for i in items}
        p = {i: _dot(m[i], block_diag(m[i])) for i in items}
        levels = int(math.log2(c)) - 1
        for lvl in range(levels):
            if lvl + 1 < levels:
                xp = {i: _dot(jnp.concatenate([x[i], p[i]], axis=0), block_diag(p[i])) for i in items}
                x = {i: x[i] + xp[i][:c] for i in items}
                p = {i: xp[i][c:] for i in items}
            else:
                xp = {i: _dot(x[i], block_diag(p[i])) for i in items}
                x = {i: x[i] + xp[i] for i in items}
        u, w = {}, {}
        for n, j in items:
            ci = chunk_ids[n]
            t_u = x[n, j] * rows_scr[1, ci, pl.ds(j, 1), :]
            t_w = x[n, j] * rows_scr[2, ci, pl.ds(j, 1), :]
            v_a = qkv_ref[rows[n], 2 * GDN_QK_DIM + 2 * j * hd:2 * GDN_QK_DIM + (2 * j + 1) * hd]
            v_b = qkv_ref[rows[n], 2 * GDN_QK_DIM + (2 * j + 1) * hd:2 * GDN_QK_DIM + (2 * j + 2) * hd]
            u[n, j] = _dot(t_u, two_blocks(v_a, v_b))
            w[n, j] = _dot(t_w, two_blocks(k[n, j], k[n, j]))
        for n, j in items:
            ci = chunk_ids[n]
            for e in range(2):
                h = 2 * j + e
                u_scr[ci, h] = u[n, j][:, e * hd:(e + 1) * hd]
                lhs_scr[ci, h] = jnp.concatenate(
                    [w[n, j][:, e * hd:(e + 1) * hd].astype(BF16), q[n, j]], axis=0)
            if j % 2 == 0:
                k2 = qkv_ref[rows[n], GDN_QK_DIM + j * hd:GDN_QK_DIM + (j + 2) * hd]
                kt_scr[ci, j // 2] = jnp.concatenate(
                    [k2[:, :hd], k2[:, hd:]], axis=0).astype(F32).T.astype(BF16)

    def prepare_body(i, carry):
        prepare([i * PREP_CHUNKS + n for n in range(PREP_CHUNKS)])
        return carry

    lax.fori_loop(0, nchunk // PREP_CHUNKS, prepare_body, 0)

    def state_body(ci, carry):
        r0 = pl.multiple_of(ci * c, c)
        rows = pl.ds(r0, c)
        gcx = [gcx_scr[rows, h * hd:(h + 1) * hd] for h in heads]
        egc = [jnp.exp(gcx[h]) for h in heads]
        ekd = [jnp.exp(gcx[h][c - 1:c, :] - gcx[h]) for h in heads]
        s_old = [s_scr[j] for j in pairs]
        pq = [_dot(lhs_scr[ci, h], s_old[h // 2][:, (h % 2) * hd:(h % 2 + 1) * hd])
              for h in heads]
        vn = [u_scr[ci, h] - pq[h][:c] for h in heads]
        vnb = [vn[h].astype(BF16) for h in heads]
        avn = [_dot(a_scr[ci, j], two_blocks(vnb[2 * j], vnb[2 * j + 1])) for j in pairs]
        vs = [jnp.concatenate([vn[2 * j] * ekd[2 * j], vn[2 * j + 1] * ekd[2 * j + 1]],
                              axis=1).astype(BF16) for j in pairs]
        zero_vs = jnp.zeros_like(vs[0])
        ds = [_dot(kt_scr[ci, j // 2], jnp.concatenate(
            [vs[j], zero_vs] if j % 2 == 0 else [zero_vs, vs[j]], axis=0)) for j in pairs]
        for j in pairs:
            egl = jnp.exp(jnp.concatenate([gcx[2 * j][c - 1:c, :], gcx[2 * j + 1][c - 1:c, :]], axis=1))
            s_scr[j] = s_old[j] * egl + ds[j]
        for h in heads:
            o_h = (pq[h][c:] * egc[h]
                   + avn[h // 2][:, (h % 2) * hd:(h % 2 + 1) * hd])
            z_h = z_ref[rows, h * hd:(h + 1) * hd]
            out_ref[rows, h * hd:(h + 1) * hd] = (
                _rms_scale(o_h) * onw * _silu(z_h)).astype(out_ref.dtype)
        return carry

    lax.fori_loop(0, nchunk, state_body, 0)


def _gdn_core(qkv, z, ba, bat, a_log, dt_bias, out_norm_w, head_order, bsz, seq):
    m = qkv.shape[0]
    tm = TM_GDN
    nt = seq // tm
    nh = GDN_V_HEADS
    nb = ba.shape[1]
    nchunk = tm // GDN_CHUNK

    def cur_map(b, t):
        return (b * nt + t, 0)

    return pl.pallas_call(
        _gdn_core_kernel,
        grid=(bsz, nt),
        in_specs=[
            pl.BlockSpec((tm, GDN_CONV_DIM), cur_map),
            pl.BlockSpec((tm, GDN_V_DIM), cur_map),
            pl.BlockSpec((tm, nb), cur_map),
            pl.BlockSpec((2 * nh, tm), lambda b, t: (0, b * nt + t)),
            _const_spec((1, nh)),
            _const_spec((1, nh)),
            _const_spec((nh, 1)),
            _const_spec((nh, 1)),
            _const_spec((1, GDN_HEAD_DIM)),
        ],
        out_specs=pl.BlockSpec((tm, GDN_V_DIM), cur_map),
        out_shape=jax.ShapeDtypeStruct((m, GDN_V_DIM), BF16),
        scratch_shapes=[
            pltpu.VMEM((GDN_QK_HEADS, GDN_HEAD_DIM, 2 * GDN_HEAD_DIM), F32),
            pltpu.VMEM((tm, nh * GDN_HEAD_DIM), F32),
            pltpu.VMEM((tm, nh * GDN_CHUNK), F32),
            pltpu.VMEM((3, nchunk, GDN_QK_HEADS, 2 * GDN_CHUNK), F32),
            pltpu.VMEM((nchunk, nh, GDN_CHUNK, GDN_HEAD_DIM), F32),
            pltpu.VMEM((nchunk, nh, 2 * GDN_CHUNK, GDN_HEAD_DIM), BF16),
            pltpu.VMEM((nchunk, GDN_QK_HEADS, GDN_CHUNK, 2 * GDN_CHUNK), BF16),
            pltpu.VMEM((nchunk, GDN_QK_HEADS // 2, GDN_HEAD_DIM, 2 * GDN_CHUNK), BF16),
        ],
        compiler_params=_params("parallel", "arbitrary"),
        name="gdn_core",
    )(qkv, z, ba, bat, a_log.reshape(1, nh), dt_bias.reshape(1, nh),
      a_log[head_order].reshape(nh, 1), dt_bias[head_order].reshape(nh, 1),
      out_norm_w.reshape(1, GDN_HEAD_DIM))


def _ffn_kernel(res_ref, x_ref, wpre_ref, nw_ref, wup_ref, cw_ref, cb_ref, wdn_ref, fnw_ref, out_ref,
                carry_scr, act_scr, h_scr, *, final_norm):
    t = pl.program_id(1)

    @pl.when(t == 0)
    def _():
        carry_scr[...] = jnp.zeros_like(carry_scr)

    nsub = res_ref.shape[0] // FFN_SUB

    def up_phase(si):
        rows = slice(si * FFN_SUB, (si + 1) * FFN_SUB)
        h = res_ref[rows, :] + jnp.dot(x_ref[rows, :], wpre_ref[...], preferred_element_type=F32)
        h_scr[si] = h
        hn = (_rms_scale(h) * nw_ref[...]).astype(BF16)
        for c0 in range(0, D_FF, FFN_CHUNK):
            halves = []
            for off in (0, D_FF):
                sl = slice(off + c0, off + c0 + FFN_CHUNK)
                cur = jnp.dot(hn, wup_ref[:, sl], preferred_element_type=F32)
                prev8 = carry_scr[:, sl]
                carry_scr[:, sl] = cur[FFN_SUB - SUBLANES:]
                y = cur * cw_ref[FFN_CONV_WIDTH - 1:FFN_CONV_WIDTH, sl] + cb_ref[:, sl]
                for j in range(1, FFN_CONV_WIDTH):
                    y = y + (_shift_rows(cur, prev8, j)
                             * cw_ref[FFN_CONV_WIDTH - 1 - j:FFN_CONV_WIDTH - j, sl])
                halves.append(y)
            act_scr[si, :, c0:c0 + FFN_CHUNK] = (_silu(halves[0]) * halves[1]).astype(BF16)

    def down_phase(si):
        rows = slice(si * FFN_SUB, (si + 1) * FFN_SUB)
        act = act_scr[si]
        res = h_scr[si] + jnp.dot(act, wdn_ref[...], preferred_element_type=F32)
        if final_norm:
            res = _rms_scale(res) * fnw_ref[...]
        out_ref[rows, :] = res

    up_phase(0)
    for si in range(1, nsub):
        up_phase(si)
        down_phase(si - 1)
    down_phase(nsub - 1)


def _ffn(res, x2, w_pre, norm_w, w_up, conv_w, conv_b, w_down, final_w, bsz, seq, final_norm):
    m, k = res.shape
    tm = TM_FFN
    nt = seq // tm

    def row_map(b, t):
        return (b * nt + t, 0)

    return pl.pallas_call(
        functools.partial(_ffn_kernel, final_norm=final_norm),
        grid=(bsz, nt),
        in_specs=[
            pl.BlockSpec((tm, k), row_map),
            pl.BlockSpec((tm, x2.shape[1]), row_map),
            _resident_spec(w_pre.shape),
            _const_spec((1, k)),
            _resident_spec(w_up.shape),
            _const_spec((FFN_CONV_WIDTH, 2 * D_FF)),
            _const_spec((1, 2 * D_FF)),
            _resident_spec(w_down.shape),
            _const_spec((1, k)),
        ],
        out_specs=pl.BlockSpec((tm, k), row_map),
        out_shape=jax.ShapeDtypeStruct((m, k), F32),
        scratch_shapes=[
            pltpu.VMEM((SUBLANES, 2 * D_FF), F32),
            pltpu.VMEM((tm // FFN_SUB, FFN_SUB, D_FF), BF16),
            pltpu.VMEM((tm // FFN_SUB, FFN_SUB, k), F32),
        ],
        compiler_params=_params("parallel", "arbitrary"),
        name="ffn_final" if final_norm else "ffn",
    )(res, x2, w_pre, norm_w, w_up, conv_w, conv_b, w_down, final_w)


def _qkv_proj_kernel(h_ref, qnw_ref, kvnw_ref, wq_ref, wkv_ref, q_ref, kv_ref):
    y = _rms_scale(h_ref[...])
    q = jnp.dot((y * qnw_ref[...]).astype(BF16), wq_ref[...], preferred_element_type=F32)
    q_ref[...] = (q * (SWA_HEAD_DIM ** -0.5)).astype(q_ref.dtype)
    kv_ref[...] = jnp.dot((y * kvnw_ref[...]).astype(BF16), wkv_ref[...],
                          preferred_element_type=F32).astype(kv_ref.dtype)


def _qkv_proj(h2, q_norm_w, kv_norm_w, w_q, w_kv):
    m, k = h2.shape
    nq = w_q.shape[1]
    nkv = w_kv.shape[1]
    tm = TM_MM
    return pl.pallas_call(
        _qkv_proj_kernel,
        grid=(m // tm,),
        in_specs=[
            pl.BlockSpec((tm, k), lambda i: (i, 0)),
            _const_spec((1, k)),
            _const_spec((1, k)),
            _const_spec((k, nq)),
            _const_spec((k, nkv)),
        ],
        out_specs=[
            pl.BlockSpec((tm, nq), lambda i: (i, 0)),
            pl.BlockSpec((tm, nkv), lambda i: (i, 0)),
        ],
        out_shape=[
            jax.ShapeDtypeStruct((m, nq), BF16),
            jax.ShapeDtypeStruct((m, nkv), BF16),
        ],
        compiler_params=_params("parallel"),
        name="qkv_proj",
    )(h2, q_norm_w, kv_norm_w, w_q, w_kv)


def _t5_bucket_map():
    qi = np.arange(SWA_BLOCK)[:, None]
    ki = np.arange(2 * SWA_BLOCK)[None, :]
    dist = qi + SWA_BLOCK - ki
    n = np.maximum(dist, 0)
    max_exact = REL_BUCKETS // 2
    nf = np.maximum(n, 1).astype(np.float32)
    large = max_exact + (np.log(nf / np.float32(max_exact)).astype(np.float32)
                         / np.float32(math.log(REL_MAX_DISTANCE / max_exact))
                         * np.float32(REL_BUCKETS - max_exact)).astype(np.int32)
    large = np.minimum(large, REL_BUCKETS - 1)
    bucket = np.where(n < max_exact, n, large).astype(np.int32)
    in_window = (dist >= 0) & (dist < SWA_WINDOW)
    return np.where(in_window, bucket, -1).astype(np.int32)


def _swa_kernel(table_ref, sinks_ref, q_ref, kvc_ref, bmap_ref, out_ref,
                bias_scr, kprev_scr, vprev_scr):
    n = pl.program_id(1)
    blk = SWA_BLOCK
    hd = SWA_HEAD_DIM
    kvd = SWA_KV_HEADS * hd

    @pl.when(n == 0)
    def _():
        kprev_scr[...] = jnp.zeros_like(kprev_scr)
        vprev_scr[...] = jnp.zeros_like(vprev_scr)

    @pl.when((pl.program_id(0) == 0) & (n == 0))
    def _():
        bmap = bmap_ref[...]
        for hq in range(SWA_Q_HEADS):
            bias_scr[0, hq] = jnp.where(bmap < 0, NEG_INF, 0.0)

        def bucket_body(b, carry):
            hit = bmap == b
            for hq in range(SWA_Q_HEADS):
                bias_scr[0, hq] = jnp.where(hit, table_ref[b, hq], bias_scr[0, hq])
            return carry

        lax.fori_loop(0, REL_BUCKETS, bucket_body, 0)
        key_col = lax.broadcasted_iota(jnp.int32, (blk, 2 * blk), 1)
        for hq in range(SWA_Q_HEADS):
            bias_scr[1, hq] = jnp.where(key_col >= blk, bias_scr[0, hq], NEG_INF)
        for hq in range(SWA_Q_HEADS):
            for tbl in range(2):
                bias_scr[tbl, hq] = jnp.where(key_col == 0, sinks_ref[hq], bias_scr[tbl, hq])

    lane = lax.broadcasted_iota(jnp.int32, (1, LANES), 1)
    lo = lane < hd
    ones_blk = jnp.ones((2 * blk, LANES), BF16)
    not_row0 = lax.broadcasted_iota(jnp.int32, (blk, 1), 0) > 0
    kv_heads = range(SWA_KV_HEADS)
    subs = range(q_ref.shape[0] // blk)
    items = [(sb, h) for sb in subs for h in kv_heads]
    d_row = lax.broadcasted_iota(jnp.int32, (LANES, LANES), 0)
    d_col = lax.broadcasted_iota(jnp.int32, (LANES, LANES), 1) % hd
    k_blk, v_blk = {}, {}
    for h in kv_heads:
        grp = (h * hd) // LANES
        dup = (d_row == d_col + (h * hd) % LANES).astype(BF16)
        k_all = jnp.dot(kvc_ref[:, grp * LANES:(grp + 1) * LANES], dup,
                        preferred_element_type=F32).astype(BF16)
        v_all = jnp.dot(kvc_ref[:, kvd + grp * LANES:kvd + (grp + 1) * LANES], dup,
                        preferred_element_type=F32).astype(BF16)
        k_blk[-1, h] = kprev_scr[h]
        v_blk[-1, h] = vprev_scr[h]
        for sb in subs:
            k_blk[sb, h] = k_all[sb * blk:(sb + 1) * blk]
            v_blk[sb, h] = v_all[sb * blk:(sb + 1) * blk]
        for sb in subs:
            if sb + 1 < len(subs):
                k_blk[sb, h, "prev"] = jnp.where(not_row0, k_blk[sb, h], jnp.zeros_like(k_blk[sb, h]))
                v_blk[sb, h, "prev"] = jnp.where(not_row0, v_blk[sb, h], jnp.zeros_like(v_blk[sb, h]))
        last = len(subs) - 1
        kprev_scr[h] = jnp.where(not_row0, k_blk[last, h], jnp.zeros_like(k_blk[last, h]))
        vprev_scr[h] = jnp.where(not_row0, v_blk[last, h], jnp.zeros_like(v_blk[last, h]))

    def prev_of(blocks, sb, h):
        return blocks[-1, h] if sb == 0 else blocks[sb - 1, h, "prev"]

    k_dup = {(sb, h): jnp.concatenate([prev_of(k_blk, sb, h), k_blk[sb, h]], axis=0) for sb, h in items}
    v_aug = {(sb, h): jnp.concatenate(
        [jnp.concatenate([prev_of(v_blk, sb, h), v_blk[sb, h]], axis=0), ones_blk], axis=1)
        for sb, h in items}
    q_stack = {}
    for sb, h in items:
        q_rows = []
        for gidx in range(SWA_GROUP):
            hq = h * SWA_GROUP + gidx
            qg = q_ref[sb * blk:(sb + 1) * blk, (hq * hd) // LANES * LANES:((hq * hd) // LANES + 1) * LANES]
            sel = lo if (hq * hd) % LANES == 0 else ~lo
            q_rows.append(jnp.where(sel, qg, jnp.zeros_like(qg)))
        q_stack[sb, h] = jnp.concatenate(q_rows, axis=0)
    s = {i: _dot_nt(q_stack[i], k_dup[i]) for i in items}
    first = [(n == 0).astype(jnp.int32) if sb == 0 else 0 for sb in subs]
    sbias = {(sb, h): s[sb, h] + bias_scr[first[sb], pl.ds(h * SWA_GROUP, SWA_GROUP)].reshape(
        SWA_GROUP * blk, 2 * blk) for sb, h in items}
    mx = {i: jnp.max(sbias[i], axis=-1, keepdims=True) for i in items}
    p = {i: jnp.exp(sbias[i] - mx[i]).astype(BF16) for i in items}
    oa = {i: jnp.dot(p[i], v_aug[i], preferred_element_type=F32) for i in items}
    outs = {i: oa[i][:, :LANES] / oa[i][:, LANES:] for i in items}
    for sb, h in items:
        for pair in range(SWA_GROUP // 2):
            hq0 = h * SWA_GROUP + 2 * pair
            col = (hq0 * hd) // LANES * LANES
            even = outs[sb, h][(2 * pair) * blk:(2 * pair + 1) * blk]
            odd = outs[sb, h][(2 * pair + 1) * blk:(2 * pair + 2) * blk]
            out_ref[sb * blk:(sb + 1) * blk, col:col + LANES] = jnp.where(lo, even, odd).astype(
                out_ref.dtype)


def _swa(q, kv, rel_table, sinks, bsz, seq):
    m = q.shape[0]
    nb = seq // SWA_STEP
    bmap = jnp.asarray(_t5_bucket_map())
    grid_spec = pltpu.PrefetchScalarGridSpec(
        num_scalar_prefetch=0,
        grid=(bsz, nb),
        in_specs=[
            pl.BlockSpec(memory_space=pltpu.SMEM),
            pl.BlockSpec(memory_space=pltpu.SMEM),
            pl.BlockSpec((SWA_STEP, q.shape[1]), lambda b, n: (b * nb + n, 0)),
            pl.BlockSpec((SWA_STEP, kv.shape[1]), lambda b, n: (b * nb + n, 0)),
            _const_spec((SWA_BLOCK, 2 * SWA_BLOCK)),
        ],
        out_specs=pl.BlockSpec((SWA_STEP, q.shape[1]), lambda b, n: (b * nb + n, 0)),
        scratch_shapes=[pltpu.VMEM((2, SWA_Q_HEADS, SWA_BLOCK, 2 * SWA_BLOCK), F32),
                        pltpu.VMEM((SWA_KV_HEADS, SWA_BLOCK, LANES), BF16),
                        pltpu.VMEM((SWA_KV_HEADS, SWA_BLOCK, LANES), BF16)],
    )
    return pl.pallas_call(
        _swa_kernel,
        grid_spec=grid_spec,
        out_shape=jax.ShapeDtypeStruct((m, q.shape[1]), BF16),
        compiler_params=_params("arbitrary", "arbitrary"),
        name="swa",
    )(rel_table, sinks, q, kv, bmap)


def kernel(x, a_norm_w, a_w_in, a_conv_w, a_a_log, a_dt_bias, a_out_norm_w, a_w_out,
           kv_norm_w, w_kv, b_norm_w, b_w_q, b_sinks, b_w_o, rel_bias_table,
           ffn_norm_w, ffn_w_up, ffn_conv_w, ffn_conv_b, ffn_w_down, final_norm_w):
    bsz, seq, d = x.shape
    n_a = a_w_in.shape[0]
    n_b = b_w_q.shape[0]
    depth = n_a + n_b
    h = x.reshape(bsz * seq, d)
    kv = None
    head_order = np.concatenate([np.arange(0, GDN_V_HEADS, 2), np.arange(1, GDN_V_HEADS, 2)])
    for layer in range(depth):
        if layer < n_a:
            i = layer
            w_in = a_w_in[i]
            w_ba = w_in[:, GDN_MAIN_DIM:].astype(BF16)
            w_at = w_ba[:, np.concatenate([head_order, GDN_V_HEADS + head_order])].T
            qkv, z, ba, bat = _in_proj(h, a_norm_w[i].reshape(1, d),
                                       w_in[:, :GDN_MAIN_DIM].astype(BF16), w_ba, w_at,
                                       a_conv_w[i], bsz, seq)
            o = _gdn_core(qkv, z, ba, bat, a_a_log[i], a_dt_bias[i], a_out_norm_w[i],
                          head_order, bsz, seq)
            w_mix = a_w_out[i].astype(BF16)
        else:
            j = layer - n_a
            if j == 0:
                q, kv = _qkv_proj(h, b_norm_w[j].reshape(1, d), kv_norm_w.reshape(1, d),
                                  b_w_q[j].astype(BF16), w_kv.astype(BF16))
            else:
                q, _ = _qkv_proj(h, b_norm_w[j].reshape(1, d), kv_norm_w.reshape(1, d),
                                 b_w_q[j].astype(BF16), w_kv.astype(BF16))
            o = _swa(q, kv, rel_bias_table, b_sinks[j], bsz, seq)
            w_mix = b_w_o[j].astype(BF16)
        last = layer == depth - 1
        h = _ffn(h, o, w_mix, ffn_norm_w[layer].reshape(1, d), ffn_w_up[layer].astype(BF16),
                 ffn_conv_w[layer], ffn_conv_b[layer].reshape(1, 2 * D_FF),
                 ffn_w_down[layer].astype(BF16), final_norm_w.reshape(1, d), bsz, seq, last)
    return h.reshape(bsz, seq, d)
```

```python
import functools
import math

import jax
import jax.numpy as jnp
import numpy as np
from jax import lax
from jax.experimental import pallas as pl
from jax.experimental.pallas import tpu as pltpu

F32 = jnp.float32
BF16 = jnp.bfloat16

EPS = 1e-6
NEG_INF = -1e30

D_MODEL = 1024
GDN_QK_HEADS = 8
GDN_V_HEADS = 16
GDN_HEAD_DIM = 128
GDN_QK_DIM = GDN_QK_HEADS * GDN_HEAD_DIM
GDN_V_DIM = GDN_V_HEADS * GDN_HEAD_DIM
GDN_CONV_DIM = 2 * GDN_QK_DIM + GDN_V_DIM
GDN_MAIN_DIM = GDN_CONV_DIM + GDN_V_DIM
GDN_CONV_WIDTH = 4
GDN_CHUNK = 64

SWA_Q_HEADS = 16
SWA_KV_HEADS = 4
SWA_GROUP = SWA_Q_HEADS // SWA_KV_HEADS
SWA_HEAD_DIM = 64
SWA_WINDOW = 128
SWA_BLOCK = 128
SWA_STEP = 512
REL_BUCKETS = 32
REL_MAX_DISTANCE = 128

D_FF = 2816
FFN_CONV_WIDTH = 3
FFN_CHUNK = 256
CONV_ROWS = 64
PREP_CHUNKS = 4

SUBLANES = 8
LANES = 128
VMEM_LIMIT_BYTES = 56 * 1024 * 1024

TM_PROJ = 256
TM_GDN = 256
TM_FFN = 512
FFN_SUB = 256


def _params(*sem):
    return pltpu.CompilerParams(dimension_semantics=sem, vmem_limit_bytes=VMEM_LIMIT_BYTES)


def _const_spec(shape):
    zeros = (0,) * len(shape)
    return pl.BlockSpec(shape, lambda *_: zeros)


def _resident_spec(shape):
    zeros = (0,) * len(shape)
    return pl.BlockSpec(shape, lambda *_: zeros, pipeline_mode=pl.Buffered(1))


def _dot(a, b):
    return jnp.dot(a.astype(BF16), b.astype(BF16), preferred_element_type=F32)


def _dot_nt(a, b):
    return lax.dot_general(a.astype(BF16), b.astype(BF16), (((1,), (1,)), ((), ())),
                           preferred_element_type=F32)


def _rms_scale(x):
    return x * lax.rsqrt(jnp.mean(x * x, axis=-1, keepdims=True) + EPS)


def _silu(x):
    return _silu_of_half(0.5 * x)


def _silu_of_half(half):
    return half + half * jnp.tanh(half)


def _shift_rows(cur, prev8, j):
    if j == 0:
        return cur
    nrow, ncol = cur.shape
    tiles = jnp.concatenate([prev8, cur], axis=0).reshape(nrow // SUBLANES + 1, SUBLANES, ncol)
    rot = pltpu.roll(tiles, j, 1)
    row = lax.broadcasted_iota(jnp.int32, (1, SUBLANES, ncol), 1)
    return jnp.where(row < j, rot[:-1], rot[1:]).reshape(nrow, ncol)


def _split3(x):
    hi = x.astype(BF16)
    r1 = x - hi.astype(F32)
    mid = r1.astype(BF16)
    lo = (r1 - mid.astype(F32)).astype(BF16)
    return hi, mid, lo


def _in_proj_kernel(x_ref, nw_ref, wm_ref, wba_ref, wbat_ref, cw_ref,
                    qkv_ref, z_ref, ba_ref, bat_ref, carry_scr):
    t = pl.program_id(1)

    @pl.when(t == 0)
    def _():
        carry_scr[...] = jnp.zeros_like(carry_scr)

    tm = x_ref.shape[0]
    hd = GDN_HEAD_DIM
    hn = (_rms_scale(x_ref[...]) * nw_ref[...]).astype(BF16)
    step = 2 * hd
    zevery = GDN_CONV_DIM // GDN_V_DIM
    for c0 in range(0, GDN_CONV_DIM, step):
        if (c0 // step) % zevery == 0:
            z0 = c0 // zevery
            z_ref[:, z0:z0 + step] = jnp.dot(
                hn, wm_ref[(GDN_CONV_DIM + z0) // step], preferred_element_type=F32)
        sl = slice(c0, c0 + step)
        cur = jnp.dot(hn, wm_ref[c0 // step], preferred_element_type=F32)
        prev8 = carry_scr[:, sl]
        carry_scr[:, sl] = cur[tm - SUBLANES:]
        for hc in range(c0, c0 + step, hd):
            lanes = slice(hc - c0, hc - c0 + hd)
            wts = [cw_ref[GDN_CONV_WIDTH - 1 - j:GDN_CONV_WIDTH - j, hc:hc + hd]
                   for j in range(GDN_CONV_WIDTH)]
            for r0 in range(0, tm, CONV_ROWS):
                piece = cur[r0:r0 + CONV_ROWS, lanes]
                before = prev8[:, lanes] if r0 == 0 else cur[r0 - SUBLANES:r0, lanes]
                yh = piece * wts[0]
                for j in range(1, GDN_CONV_WIDTH):
                    yh = yh + _shift_rows(piece, before, j) * wts[j]
                yh = _silu_of_half(yh)
                if hc < 2 * GDN_QK_DIM:
                    yh = yh * lax.rsqrt(jnp.sum(yh * yh, axis=-1, keepdims=True) + EPS)
                    if hc < GDN_QK_DIM:
                        yh = yh * (hd ** -0.5)
                qkv_ref[r0:r0 + CONV_ROWS, hc:hc + hd] = yh.astype(qkv_ref.dtype)
    ba_ref[...] = jnp.dot(hn, wba_ref[...], preferred_element_type=F32)
    bat_ref[...] = _dot_nt(wbat_ref[...], hn)


def _in_proj(x2, norm_w, w_main, w_ba, w_bat, conv_w, bsz, seq):
    m, k = x2.shape
    n = w_main.shape[1]
    w_main = w_main.reshape(k, n // (2 * GDN_HEAD_DIM), 2 * GDN_HEAD_DIM).transpose(1, 0, 2)
    nb = w_ba.shape[1]
    nbt = w_bat.shape[0]
    tm = TM_PROJ
    nt = seq // tm

    def row_map(b, t):
        return (b * nt + t, 0)

    return pl.pallas_call(
        _in_proj_kernel,
        grid=(bsz, nt),
        in_specs=[
            pl.BlockSpec((tm, k), row_map),
            _const_spec((1, k)),
            _const_spec(w_main.shape),
            _const_spec((k, nb)),
            _const_spec((nbt, k)),
            _const_spec((GDN_CONV_WIDTH, GDN_CONV_DIM)),
        ],
        out_specs=[
            pl.BlockSpec((tm, GDN_CONV_DIM), row_map),
            pl.BlockSpec((tm, GDN_V_DIM), row_map),
            pl.BlockSpec((tm, nb), row_map),
            pl.BlockSpec((nbt, tm), lambda b, t: (0, b * nt + t)),
        ],
        out_shape=[
            jax.ShapeDtypeStruct((m, GDN_CONV_DIM), BF16),
            jax.ShapeDtypeStruct((m, GDN_V_DIM), F32),
            jax.ShapeDtypeStruct((m, nb), F32),
            jax.ShapeDtypeStruct((nbt, m), F32),
        ],
        scratch_shapes=[pltpu.VMEM((SUBLANES, GDN_CONV_DIM), F32)],
        compiler_params=_params("parallel", "arbitrary"),
        name="gdn_in_proj",
    )(x2, norm_w, w_main, w_ba, w_bat, conv_w)


def _gdn_core_kernel(qkv_ref, z_ref, ba_ref, bat_ref, alog_ref, dtb_ref,
                     alogt_ref, dtbt_ref, onw_ref, out_ref,
                     s_scr, gcx_scr, betap_scr, rows_scr, u_scr, lhs_scr, a_scr, kt_scr):
    t = pl.program_id(1)
    tm = qkv_ref.shape[0]
    c = GDN_CHUNK
    nchunk = tm // c
    hd = GDN_HEAD_DIM
    nh = GDN_V_HEADS
    npair = GDN_QK_HEADS
    heads = range(nh)
    pairs = range(npair)

    @pl.when(t == 0)
    def _():
        s_scr[...] = jnp.zeros_like(s_scr)

    beta = 1.0 / (1.0 + jnp.exp(-ba_ref[:, :nh]))
    a_lin = ba_ref[:, nh:2 * nh] + dtb_ref[...]
    g = -jnp.exp(alog_ref[...]) * (jnp.maximum(a_lin, 0.0) + jnp.log1p(jnp.exp(-jnp.abs(a_lin))))
    row = lax.broadcasted_iota(jnp.int32, (tm, tm), 0)
    col = lax.broadcasted_iota(jnp.int32, (tm, tm), 1)
    same = (row // c) == (col // c)
    lblk = (same & (col <= row)).astype(BF16)
    ublk = (same & (row <= col)).astype(BF16)
    gc = jnp.dot(jnp.concatenate([lblk] * 3, axis=1), jnp.concatenate(_split3(g), axis=0),
                 preferred_element_type=F32)
    e_row = lax.broadcasted_iota(jnp.int32, (3 * nh, npair * 2 * c), 0) % nh
    e_col = lax.broadcasted_iota(jnp.int32, (3 * nh, npair * 2 * c), 1)
    expand = (e_row == e_col // c).astype(BF16)
    betap_scr[...] = jnp.dot(jnp.concatenate(_split3(beta), axis=1), expand,
                             preferred_element_type=F32)
    e_row = lax.broadcasted_iota(jnp.int32, (3 * nh, nh * hd), 0) % nh
    e_col = lax.broadcasted_iota(jnp.int32, (3 * nh, nh * hd), 1)
    expand_hd = (e_row == e_col // hd).astype(BF16)
    gcx_scr[...] = jnp.dot(jnp.concatenate(_split3(gc), axis=1), expand_hd,
                           preferred_element_type=F32)

    beta_t = 1.0 / (1.0 + jnp.exp(-bat_ref[:nh, :]))
    a_lin_t = bat_ref[nh:, :] + dtbt_ref[...]
    g_t = -jnp.exp(alogt_ref[...]) * (jnp.maximum(a_lin_t, 0.0)
                                      + jnp.log1p(jnp.exp(-jnp.abs(a_lin_t))))
    gct = jnp.dot(jnp.concatenate(_split3(g_t), axis=1), jnp.concatenate([ublk] * 3, axis=0),
                  preferred_element_type=F32)
    begct = beta_t * jnp.exp(gct)
    for ci in range(nchunk):
        for slot, arr in enumerate((gct, beta_t, begct)):
            rows_scr[slot, ci] = jnp.concatenate([arr[:npair, ci * c:(ci + 1) * c],
                                                  arr[npair:, ci * c:(ci + 1) * c]], axis=1)

    ri = lax.broadcasted_iota(jnp.int32, (c, 2 * c), 0)
    cj = lax.broadcasted_iota(jnp.int32, (c, 2 * c), 1) % c
    left = lax.broadcasted_iota(jnp.int32, (1, 2 * c), 1) < c
    tril = cj <= ri
    strict = cj < ri
    eye = (ri == cj).astype(F32)
    onw = onw_ref[...]
    zero_hd = jnp.zeros((c, hd), BF16)

    def block_diag(xc):
        xb = xc.astype(BF16)
        zero = jnp.zeros_like(xb)
        return jnp.concatenate([jnp.where(left, xb, zero), jnp.where(left, zero, xb)], axis=0)

    def two_blocks(top, bottom):
        return jnp.concatenate([jnp.concatenate([top, zero_hd], axis=1),
                                jnp.concatenate([zero_hd, bottom], axis=1)], axis=0)

    def prepare(chunk_ids):
        items = [(n, j) for n in range(len(chunk_ids)) for j in pairs]
        rows = [pl.ds(pl.multiple_of(ci * c, c), c) for ci in chunk_ids]
        q = {(n, j): qkv_ref[rows[n], j * hd:(j + 1) * hd] for n, j in items}
        k = {(n, j): qkv_ref[rows[n], GDN_QK_DIM + j * hd:GDN_QK_DIM + (j + 1) * hd]
             for n, j in items}
        gq = {i: _dot_nt(jnp.concatenate([k[i], q[i]], axis=0), jnp.concatenate([k[i], k[i]], axis=0))
              for i in items}
        dec = {}
        for n, j in items:
            gc_pair = jnp.where(left, gcx_scr[rows[n], 2 * j * hd:2 * j * hd + 2 * c],
                                gcx_scr[rows[n], (2 * j + 1) * hd:(2 * j + 1) * hd + 2 * c])
            diff = gc_pair - rows_scr[0, chunk_ids[n], pl.ds(j, 1), :]
            dec[n, j] = jnp.where(tril, jnp.exp(jnp.where(tril, diff, 0.0)), 0.0)
        m = {(n, j): jnp.where(strict, betap_scr[rows[n], j * 2 * c:(j + 1) * 2 * c]
                               * gq[n, j][:c] * dec[n, j], 0.0) for n, j in items}
        for n, j in items:
            a_scr[chunk_ids[n], j] = (gq[n, j][c:] * dec[n, j]).astype(BF16)
        x = {i: eye - m[i] for i in items}
        p = {i: _dot(m[i], block_diag(m[i])) for i in items}
        levels = int(math.log2(c)) - 1
        for lvl in range(levels):
            if lvl + 1 < levels:
                xp = {i: _dot(jnp.concatenate([x[i], p[i]], axis=0), block_diag(p[i])) for i in items}
                x = {i: x[i] + xp[i][:c] for i in items}
                p = {i: xp[i][c:] for i in items}
            else:
                xp = {i: _dot(x[i], block_diag(p[i])) for i in items}
                x = {i: x[i] + xp[i] for i in items}
        u, w = {}, {}
        for n, j in items:
            ci = chunk_ids[n]
            t_u = x[n, j] * rows_scr[1, ci, pl.ds(j, 1), :]
            t_w = x[n, j] * rows_scr[2, ci, pl.ds(j, 1), :]
            v_a = qkv_ref[rows[n], 2 * GDN_QK_DIM + 2 * j * hd:2 * GDN_QK_DIM + (2 * j + 1) * hd]
            v_b = qkv_ref[rows[n], 2 * GDN_QK_DIM + (2 * j + 1) * hd:2 * GDN_QK_DIM + (2 * j + 2) * hd]
            u[n, j] = _dot(t_u, two_blocks(v_a, v_b))
            w[n, j] = _dot(t_w, two_blocks(k[n, j], k[n, j]))
        for n, j in items:
            ci = chunk_ids[n]
            for e in range(2):
                h = 2 * j + e
                u_scr[ci, h] = u[n, j][:, e * hd:(e + 1) * hd]
                lhs_scr[ci, h] = jnp.concatenate(
                    [w[n, j][:, e * hd:(e + 1) * hd].astype(BF16), q[n, j]], axis=0)
            if j % 2 == 0:
                k2 = qkv_ref[rows[n], GDN_QK_DIM + j * hd:GDN_QK_DIM + (j + 2) * hd]
                kt_scr[ci, j // 2] = jnp.concatenate(
                    [k2[:, :hd], k2[:, hd:]], axis=0).astype(F32).T.astype(BF16)

    def prepare_body(i, carry):
        prepare([i * PREP_CHUNKS + n for n in range(PREP_CHUNKS)])
        return carry

    lax.fori_loop(0, nchunk // PREP_CHUNKS, prepare_body, 0)

    def state_body(ci, carry):
        r0 = pl.multiple_of(ci * c, c)
        rows = pl.ds(r0, c)
        gcx = [gcx_scr[rows, h * hd:(h + 1) * hd] for h in heads]
        egc = [jnp.exp(gcx[h]) for h in heads]
        ekd = [jnp.exp(gcx[h][c - 1:c, :] - gcx[h]) for h in heads]
        s_old = [s_scr[j] for j in pairs]
        pq = [_dot(lhs_scr[ci, h], s_old[h // 2][:, (h % 2) * hd:(h % 2 + 1) * hd])
              for h in heads]
        vn = [u_scr[ci, h] - pq[h][:c] for h in heads]
        vnb = [vn[h].astype(BF16) for h in heads]
        avn = [_dot(a_scr[ci, j], two_blocks(vnb[2 * j], vnb[2 * j + 1])) for j in pairs]
        vs = [jnp.concatenate([vn[2 * j] * ekd[2 * j], vn[2 * j + 1] * ekd[2 * j + 1]],
                              axis=1).astype(BF16) for j in pairs]
        zero_vs = jnp.zeros_like(vs[0])
        ds = [_dot(kt_scr[ci, j // 2], jnp.concatenate(
            [vs[j], zero_vs] if j % 2 == 0 else [zero_vs, vs[j]], axis=0)) for j in pairs]
        for j in pairs:
            egl = jnp.exp(jnp.concatenate([gcx[2 * j][c - 1:c, :], gcx[2 * j + 1][c - 1:c, :]], axis=1))
            s_scr[j] = s_old[j] * egl + ds[j]
        for h in heads:
            o_h = (pq[h][c:] * egc[h]
                   + avn[h // 2][:, (h % 2) * hd:(h % 2 + 1) * hd])
            z_h = z_ref[rows, h * hd:(h + 1) * hd]
            out_ref[rows, h * hd:(h + 1) * hd] = (
                _rms_scale(o_h) * onw * _silu(z_h)).astype(out_ref.dtype)
        return carry

    lax.fori_loop(0, nchunk, state_body, 0)


def _gdn_core(qkv, z, ba, bat, a_log, dt_bias, out_norm_w, head_order, bsz, seq):
    m = qkv.shape[0]
    tm = TM_GDN
    nt = seq // tm
    nh = GDN_V_HEADS
    nb = ba.shape[1]
    nchunk = tm // GDN_CHUNK

    def cur_map(b, t):
        return (b * nt + t, 0)

    return pl.pallas_call(
        _gdn_core_kernel,
        grid=(bsz, nt),
        in_specs=[
            pl.BlockSpec((tm, GDN_CONV_DIM), cur_map),
            pl.BlockSpec((tm, GDN_V_DIM), cur_map),
            pl.BlockSpec((tm, nb), cur_map),
            pl.BlockSpec((2 * nh, tm), lambda b, t: (0, b * nt + t)),
            _const_spec((1, nh)),
            _const_spec((1, nh)),
            _const_spec((nh, 1)),
            _const_spec((nh, 1)),
            _const_spec((1, GDN_HEAD_DIM)),
        ],
        out_specs=pl.BlockSpec((tm, GDN_V_DIM), cur_map),
        out_shape=jax.ShapeDtypeStruct((m, GDN_V_DIM), BF16),
        scratch_shapes=[
            pltpu.VMEM((GDN_QK_HEADS, GDN_HEAD_DIM, 2 * GDN_HEAD_DIM), F32),
            pltpu.VMEM((tm, nh * GDN_HEAD_DIM), F32),
            pltpu.VMEM((tm, nh * GDN_CHUNK), F32),
            pltpu.VMEM((3, nchunk, GDN_QK_HEADS, 2 * GDN_CHUNK), F32),
            pltpu.VMEM((nchunk, nh, GDN_CHUNK, GDN_HEAD_DIM), F32),
            pltpu.VMEM((nchunk, nh, 2 * GDN_CHUNK, GDN_HEAD_DIM), BF16),
            pltpu.VMEM((nchunk, GDN_QK_HEADS, GDN_CHUNK, 2 * GDN_CHUNK), BF16),
            pltpu.VMEM((nchunk, GDN_QK_HEADS // 2, GDN_HEAD_DIM, 2 * GDN_CHUNK), BF16),
        ],
        compiler_params=_params("parallel", "arbitrary"),
        name="gdn_core",
    )(qkv, z, ba, bat, a_log.reshape(1, nh), dt_bias.reshape(1, nh),
      a_log[head_order].reshape(nh, 1), dt_bias[head_order].reshape(nh, 1),
      out_norm_w.reshape(1, GDN_HEAD_DIM))


def _ffn_kernel(res_ref, x_ref, wpre_ref, nw_ref, wup_ref, cw_ref, cb_ref, wdn_ref, fnw_ref, out_ref,
                carry_scr, act_scr, h_scr, *, final_norm):
    t = pl.program_id(1)

    @pl.when(t == 0)
    def _():
        carry_scr[...] = jnp.zeros_like(carry_scr)

    nsub = res_ref.shape[0] // FFN_SUB

    def up_phase(si):
        rows = slice(si * FFN_SUB, (si + 1) * FFN_SUB)
        h = res_ref[rows, :] + jnp.dot(x_ref[rows, :], wpre_ref[...], preferred_element_type=F32)
        h_scr[si] = h
        hn = (_rms_scale(h) * nw_ref[...]).astype(BF16)
        for c0 in range(0, D_FF, FFN_CHUNK):
            halves = []
            for off in (0, D_FF):
                sl = slice(off + c0, off + c0 + FFN_CHUNK)
                cur = jnp.dot(hn, wup_ref[:, sl], preferred_element_type=F32)
                prev8 = carry_scr[:, sl]
                carry_scr[:, sl] = cur[FFN_SUB - SUBLANES:]
                y = cur * cw_ref[FFN_CONV_WIDTH - 1:FFN_CONV_WIDTH, sl] + cb_ref[:, sl]
                for j in range(1, FFN_CONV_WIDTH):
                    y = y + (_shift_rows(cur, prev8, j)
                             * cw_ref[FFN_CONV_WIDTH - 1 - j:FFN_CONV_WIDTH - j, sl])
                halves.append(y)
            act_scr[si, :, c0:c0 + FFN_CHUNK] = (_silu_of_half(halves[0]) * halves[1]).astype(BF16)

    def down_phase(si):
        rows = slice(si * FFN_SUB, (si + 1) * FFN_SUB)
        act = act_scr[si]
        res = h_scr[si] + jnp.dot(act, wdn_ref[...], preferred_element_type=F32)
        if final_norm:
            res = _rms_scale(res) * fnw_ref[...]
        out_ref[rows, :] = res

    up_phase(0)
    for si in range(1, nsub):
        up_phase(si)
        down_phase(si - 1)
    down_phase(nsub - 1)


def _ffn(res, x2, w_pre, norm_w, w_up, conv_w, conv_b, w_down, final_w, bsz, seq, final_norm):
    m, k = res.shape
    tm = TM_FFN
    nt = seq // tm

    def row_map(b, t):
        return (b * nt + t, 0)

    return pl.pallas_call(
        functools.partial(_ffn_kernel, final_norm=final_norm),
        grid=(bsz, nt),
        in_specs=[
            pl.BlockSpec((tm, k), row_map),
            pl.BlockSpec((tm, x2.shape[1]), row_map),
            _resident_spec(w_pre.shape),
            _const_spec((1, k)),
            _resident_spec(w_up.shape),
            _const_spec((FFN_CONV_WIDTH, 2 * D_FF)),
            _const_spec((1, 2 * D_FF)),
            _resident_spec(w_down.shape),
            _const_spec((1, k)),
        ],
        out_specs=pl.BlockSpec((tm, k), row_map),
        out_shape=jax.ShapeDtypeStruct((m, k), F32),
        scratch_shapes=[
            pltpu.VMEM((SUBLANES, 2 * D_FF), F32),
            pltpu.VMEM((tm // FFN_SUB, FFN_SUB, D_FF), BF16),
            pltpu.VMEM((tm // FFN_SUB, FFN_SUB, k), F32),
        ],
        compiler_params=_params("parallel", "arbitrary"),
        name="ffn_final" if final_norm else "ffn",
    )(res, x2, w_pre, norm_w, w_up, conv_w, conv_b, w_down, final_w)


def _t5_bucket_map():
    qi = np.arange(SWA_BLOCK)[:, None]
    ki = np.arange(2 * SWA_BLOCK)[None, :]
    dist = qi + SWA_BLOCK - ki
    n = np.maximum(dist, 0)
    max_exact = REL_BUCKETS // 2
    nf = np.maximum(n, 1).astype(np.float32)
    large = max_exact + (np.log(nf / np.float32(max_exact)).astype(np.float32)
                         / np.float32(math.log(REL_MAX_DISTANCE / max_exact))
                         * np.float32(REL_BUCKETS - max_exact)).astype(np.int32)
    large = np.minimum(large, REL_BUCKETS - 1)
    bucket = np.where(n < max_exact, n, large).astype(np.int32)
    in_window = (dist >= 0) & (dist < SWA_WINDOW)
    return np.where(in_window, bucket, -1).astype(np.int32)


def _swa_kernel(table_ref, sinks_ref, h_ref, qnw_ref, kvnw_ref, wq_ref, wkv_ref, bmap_ref, out_ref,
                bias_scr, kprev_scr, vprev_scr):
    n = pl.program_id(1)
    blk = SWA_BLOCK
    hd = SWA_HEAD_DIM
    kvd = SWA_KV_HEADS * hd

    @pl.when(n == 0)
    def _():
        kprev_scr[...] = jnp.zeros_like(kprev_scr)
        vprev_scr[...] = jnp.zeros_like(vprev_scr)

    @pl.when((pl.program_id(0) == 0) & (n == 0))
    def _():
        bmap = bmap_ref[...]
        for hq in range(SWA_Q_HEADS):
            bias_scr[0, hq] = jnp.where(bmap < 0, NEG_INF, 0.0)

        def bucket_body(b, carry):
            hit = bmap == b
            for hq in range(SWA_Q_HEADS):
                bias_scr[0, hq] = jnp.where(hit, table_ref[b, hq], bias_scr[0, hq])
            return carry

        lax.fori_loop(0, REL_BUCKETS, bucket_body, 0)
        key_col = lax.broadcasted_iota(jnp.int32, (blk, 2 * blk), 1)
        for hq in range(SWA_Q_HEADS):
            bias_scr[1, hq] = jnp.where(key_col >= blk, bias_scr[0, hq], NEG_INF)
        for hq in range(SWA_Q_HEADS):
            for tbl in range(2):
                bias_scr[tbl, hq] = jnp.where(key_col == 0, sinks_ref[hq], bias_scr[tbl, hq])

    y = _rms_scale(h_ref[...])
    q_all = (jnp.dot((y * qnw_ref[...]).astype(BF16), wq_ref[...], preferred_element_type=F32)
             * (hd ** -0.5)).astype(BF16)
    kv_all = jnp.dot((y * kvnw_ref[...]).astype(BF16), wkv_ref[...],
                     preferred_element_type=F32).astype(BF16)

    lane = lax.broadcasted_iota(jnp.int32, (1, LANES), 1)
    lo = lane < hd
    ones_blk = jnp.ones((2 * blk, LANES), BF16)
    not_row0 = lax.broadcasted_iota(jnp.int32, (blk, 1), 0) > 0
    kv_heads = range(SWA_KV_HEADS)
    subs = range(h_ref.shape[0] // blk)
    items = [(sb, h) for sb in subs for h in kv_heads]
    d_row = lax.broadcasted_iota(jnp.int32, (LANES, LANES), 0)
    d_col = lax.broadcasted_iota(jnp.int32, (LANES, LANES), 1) % hd
    k_blk, v_blk = {}, {}
    for h in kv_heads:
        grp = (h * hd) // LANES
        dup = (d_row == d_col + (h * hd) % LANES).astype(BF16)
        k_all = jnp.dot(kv_all[:, grp * LANES:(grp + 1) * LANES], dup,
                        preferred_element_type=F32).astype(BF16)
        v_all = jnp.dot(kv_all[:, kvd + grp * LANES:kvd + (grp + 1) * LANES], dup,
                        preferred_element_type=F32).astype(BF16)
        k_blk[-1, h] = kprev_scr[h]
        v_blk[-1, h] = vprev_scr[h]
        for sb in subs:
            k_blk[sb, h] = k_all[sb * blk:(sb + 1) * blk]
            v_blk[sb, h] = v_all[sb * blk:(sb + 1) * blk]
        for sb in subs:
            if sb + 1 < len(subs):
                k_blk[sb, h, "prev"] = jnp.where(not_row0, k_blk[sb, h], jnp.zeros_like(k_blk[sb, h]))
                v_blk[sb, h, "prev"] = jnp.where(not_row0, v_blk[sb, h], jnp.zeros_like(v_blk[sb, h]))
        last = len(subs) - 1
        kprev_scr[h] = jnp.where(not_row0, k_blk[last, h], jnp.zeros_like(k_blk[last, h]))
        vprev_scr[h] = jnp.where(not_row0, v_blk[last, h], jnp.zeros_like(v_blk[last, h]))

    def prev_of(blocks, sb, h):
        return blocks[-1, h] if sb == 0 else blocks[sb - 1, h, "prev"]

    k_dup = {(sb, h): jnp.concatenate([prev_of(k_blk, sb, h), k_blk[sb, h]], axis=0) for sb, h in items}
    v_aug = {(sb, h): jnp.concatenate(
        [jnp.concatenate([prev_of(v_blk, sb, h), v_blk[sb, h]], axis=0), ones_blk], axis=1)
        for sb, h in items}
    q_stack = {}
    for sb, h in items:
        q_rows = []
        for gidx in range(SWA_GROUP):
            hq = h * SWA_GROUP + gidx
            qg = q_all[sb * blk:(sb + 1) * blk, (hq * hd) // LANES * LANES:((hq * hd) // LANES + 1) * LANES]
            sel = lo if (hq * hd) % LANES == 0 else ~lo
            q_rows.append(jnp.where(sel, qg, jnp.zeros_like(qg)))
        q_stack[sb, h] = jnp.concatenate(q_rows, axis=0)
    s = {i: _dot_nt(q_stack[i], k_dup[i]) for i in items}
    first = [(n == 0).astype(jnp.int32) if sb == 0 else 0 for sb in subs]
    sbias = {(sb, h): s[sb, h] + bias_scr[first[sb], pl.ds(h * SWA_GROUP, SWA_GROUP)].reshape(
        SWA_GROUP * blk, 2 * blk) for sb, h in items}
    mx = {i: jnp.max(sbias[i], axis=-1, keepdims=True) for i in items}
    p = {i: jnp.exp(sbias[i] - mx[i]).astype(BF16) for i in items}
    oa = {i: jnp.dot(p[i], v_aug[i], preferred_element_type=F32) for i in items}
    outs = {i: oa[i][:, :LANES] / oa[i][:, LANES:] for i in items}
    for sb, h in items:
        for pair in range(SWA_GROUP // 2):
            hq0 = h * SWA_GROUP + 2 * pair
            col = (hq0 * hd) // LANES * LANES
            even = outs[sb, h][(2 * pair) * blk:(2 * pair + 1) * blk]
            odd = outs[sb, h][(2 * pair + 1) * blk:(2 * pair + 2) * blk]
            out_ref[sb * blk:(sb + 1) * blk, col:col + LANES] = jnp.where(lo, even, odd).astype(
                out_ref.dtype)


def _swa(h2, q_norm_w, kv_norm_w, w_q, w_kv, rel_table, sinks, bsz, seq):
    m, k = h2.shape
    nq = w_q.shape[1]
    nb = seq // SWA_STEP
    bmap = jnp.asarray(_t5_bucket_map())
    grid_spec = pltpu.PrefetchScalarGridSpec(
        num_scalar_prefetch=0,
        grid=(bsz, nb),
        in_specs=[
            pl.BlockSpec(memory_space=pltpu.SMEM),
            pl.BlockSpec(memory_space=pltpu.SMEM),
            pl.BlockSpec((SWA_STEP, k), lambda b, n: (b * nb + n, 0)),
            _const_spec((1, k)),
            _const_spec((1, k)),
            _resident_spec(w_q.shape),
            _resident_spec(w_kv.shape),
            _const_spec((SWA_BLOCK, 2 * SWA_BLOCK)),
        ],
        out_specs=pl.BlockSpec((SWA_STEP, nq), lambda b, n: (b * nb + n, 0)),
        scratch_shapes=[pltpu.VMEM((2, SWA_Q_HEADS, SWA_BLOCK, 2 * SWA_BLOCK), F32),
                        pltpu.VMEM((SWA_KV_HEADS, SWA_BLOCK, LANES), BF16),
                        pltpu.VMEM((SWA_KV_HEADS, SWA_BLOCK, LANES), BF16)],
    )
    return pl.pallas_call(
        _swa_kernel,
        grid_spec=grid_spec,
        out_shape=jax.ShapeDtypeStruct((m, nq), BF16),
        compiler_params=_params("arbitrary", "arbitrary"),
        name="swa",
    )(rel_table, sinks, h2, q_norm_w, kv_norm_w, w_q, w_kv, bmap)


def kernel(x, a_norm_w, a_w_in, a_conv_w, a_a_log, a_dt_bias, a_out_norm_w, a_w_out,
           kv_norm_w, w_kv, b_norm_w, b_w_q, b_sinks, b_w_o, rel_bias_table,
           ffn_norm_w, ffn_w_up, ffn_conv_w, ffn_conv_b, ffn_w_down, final_norm_w):
    bsz, seq, d = x.shape
    n_a = a_w_in.shape[0]
    n_b = b_w_q.shape[0]
    depth = n_a + n_b
    h = x.reshape(bsz * seq, d)
    head_order = np.concatenate([np.arange(0, GDN_V_HEADS, 2), np.arange(1, GDN_V_HEADS, 2)])
    gate_half = jnp.where(jnp.arange(2 * D_FF) < D_FF, 0.5, 1.0).astype(F32)
    for layer in range(depth):
        if layer < n_a:
            i = layer
            w_in = a_w_in[i]
            w_ba = w_in[:, GDN_MAIN_DIM:].astype(BF16)
            w_at = w_ba[:, np.concatenate([head_order, GDN_V_HEADS + head_order])].T
            qkv, z, ba, bat = _in_proj(h, a_norm_w[i].reshape(1, d),
                                       w_in[:, :GDN_MAIN_DIM].astype(BF16), w_ba, w_at,
                                       0.5 * a_conv_w[i], bsz, seq)
            o = _gdn_core(qkv, z, ba, bat, a_a_log[i], a_dt_bias[i], a_out_norm_w[i],
                          head_order, bsz, seq)
            w_mix = a_w_out[i].astype(BF16)
        else:
            j = layer - n_a
            assert n_b == 1, "shared-KV reuse across several attention layers is not implemented"
            o = _swa(h, b_norm_w[j].reshape(1, d), kv_norm_w.reshape(1, d),
                     b_w_q[j].astype(BF16), w_kv.astype(BF16), rel_bias_table, b_sinks[j], bsz, seq)
            w_mix = b_w_o[j].astype(BF16)
        last = layer == depth - 1
        h = _ffn(h, o, w_mix, ffn_norm_w[layer].reshape(1, d), ffn_w_up[layer].astype(BF16),
                 ffn_conv_w[layer] * gate_half, (ffn_conv_b[layer] * gate_half).reshape(1, 2 * D_FF),
                 ffn_w_down[layer].astype(BF16), final_norm_w.reshape(1, d), bsz, seq, last)
    return h.reshape(bsz, seq, d)
```

```python
import functools
import math

import jax
import jax.numpy as jnp
import numpy as np
from jax import lax
from jax.experimental import pallas as pl
from jax.experimental.pallas import tpu as pltpu

F32 = jnp.float32
BF16 = jnp.bfloat16

EPS = 1e-6
NEG_INF = -1e30

D_MODEL = 1024
GDN_QK_HEADS = 8
GDN_V_HEADS = 16
GDN_HEAD_DIM = 128
GDN_QK_DIM = GDN_QK_HEADS * GDN_HEAD_DIM
GDN_V_DIM = GDN_V_HEADS * GDN_HEAD_DIM
GDN_CONV_DIM = 2 * GDN_QK_DIM + GDN_V_DIM
GDN_MAIN_DIM = GDN_CONV_DIM + GDN_V_DIM
GDN_CONV_WIDTH = 4
GDN_CHUNK = 64

SWA_Q_HEADS = 16
SWA_KV_HEADS = 4
SWA_GROUP = SWA_Q_HEADS // SWA_KV_HEADS
SWA_HEAD_DIM = 64
SWA_WINDOW = 128
SWA_BLOCK = 128
SWA_STEP = 512
REL_BUCKETS = 32
REL_MAX_DISTANCE = 128

D_FF = 2816
FFN_CONV_WIDTH = 3
FFN_CHUNK = 256
CONV_ROWS = 64
PREP_CHUNKS = 4

SUBLANES = 8
LANES = 128
VMEM_LIMIT_BYTES = 56 * 1024 * 1024

TM_PROJ = 256
TM_GDN = 256
TM_FFN = 512
FFN_SUB = 256


def _params(*sem):
    return pltpu.CompilerParams(dimension_semantics=sem, vmem_limit_bytes=VMEM_LIMIT_BYTES)


def _const_spec(shape):
    zeros = (0,) * len(shape)
    return pl.BlockSpec(shape, lambda *_: zeros)


def _resident_spec(shape):
    zeros = (0,) * len(shape)
    return pl.BlockSpec(shape, lambda *_: zeros, pipeline_mode=pl.Buffered(1))


def _dot(a, b):
    return jnp.dot(a.astype(BF16), b.astype(BF16), preferred_element_type=F32)


def _dot_nt(a, b):
    return lax.dot_general(a.astype(BF16), b.astype(BF16), (((1,), (1,)), ((), ())),
                           preferred_element_type=F32)


def _rms_scale(x):
    return x * lax.rsqrt(jnp.mean(x * x, axis=-1, keepdims=True) + EPS)


def _silu(x):
    return _silu_of_half(0.5 * x)


def _silu_of_half(half):
    return half + half * jnp.tanh(half)


def _shift_rows(cur, prev8, j):
    if j == 0:
        return cur
    nrow, ncol = cur.shape
    tiles = jnp.concatenate([prev8, cur], axis=0).reshape(nrow // SUBLANES + 1, SUBLANES, ncol)
    rot = pltpu.roll(tiles, j, 1)
    row = lax.broadcasted_iota(jnp.int32, (1, SUBLANES, ncol), 1)
    return jnp.where(row < j, rot[:-1], rot[1:]).reshape(nrow, ncol)


def _split3(x):
    hi = x.astype(BF16)
    r1 = x - hi.astype(F32)
    mid = r1.astype(BF16)
    lo = (r1 - mid.astype(F32)).astype(BF16)
    return hi, mid, lo


def _in_proj_kernel(x_ref, nw_ref, w_hbm, wba_ref, wbat_ref, cw_ref,
                    qkv_ref, z_ref, ba_ref, bat_ref, carry_scr, wm_ref, w_sem):
    t = pl.program_id(1)

    @pl.when(t == 0)
    def _():
        carry_scr[...] = jnp.zeros_like(carry_scr)

    @pl.when((pl.program_id(0) == 0) & (t == 0))
    def _():
        def slab_copy(ci):
            return pltpu.make_async_copy(w_hbm.at[:, pl.ds(ci * wm_ref.shape[2], wm_ref.shape[2])],
                                         wm_ref.at[ci], w_sem.at[ci])

        for ci in range(wm_ref.shape[0]):
            slab_copy(ci).start()
        for ci in range(wm_ref.shape[0]):
            slab_copy(ci).wait()

    tm = x_ref.shape[0]
    hd = GDN_HEAD_DIM
    hn = (_rms_scale(x_ref[...]) * nw_ref[...]).astype(BF16)
    step = 2 * hd
    zevery = GDN_CONV_DIM // GDN_V_DIM
    for c0 in range(0, GDN_CONV_DIM, step):
        if (c0 // step) % zevery == 0:
            z0 = c0 // zevery
            z_ref[:, z0:z0 + step] = jnp.dot(
                hn, wm_ref[(GDN_CONV_DIM + z0) // step], preferred_element_type=F32)
        sl = slice(c0, c0 + step)
        cur = jnp.dot(hn, wm_ref[c0 // step], preferred_element_type=F32)
        prev8 = carry_scr[:, sl]
        carry_scr[:, sl] = cur[tm - SUBLANES:]
        for hc in range(c0, c0 + step, hd):
            lanes = slice(hc - c0, hc - c0 + hd)
            wts = [cw_ref[GDN_CONV_WIDTH - 1 - j:GDN_CONV_WIDTH - j, hc:hc + hd]
                   for j in range(GDN_CONV_WIDTH)]
            for r0 in range(0, tm, CONV_ROWS):
                piece = cur[r0:r0 + CONV_ROWS, lanes]
                before = prev8[:, lanes] if r0 == 0 else cur[r0 - SUBLANES:r0, lanes]
                yh = piece * wts[0]
                for j in range(1, GDN_CONV_WIDTH):
                    yh = yh + _shift_rows(piece, before, j) * wts[j]
                yh = _silu_of_half(yh)
                if hc < 2 * GDN_QK_DIM:
                    yh = yh * lax.rsqrt(jnp.sum(yh * yh, axis=-1, keepdims=True) + EPS)
                    if hc < GDN_QK_DIM:
                        yh = yh * (hd ** -0.5)
                qkv_ref[r0:r0 + CONV_ROWS, hc:hc + hd] = yh.astype(qkv_ref.dtype)
    ba_ref[...] = jnp.dot(hn, wba_ref[...], preferred_element_type=F32)
    bat_ref[...] = _dot_nt(wbat_ref[...], hn)


def _in_proj(x2, norm_w, w_in, w_ba, w_bat, conv_w, bsz, seq):
    m, k = x2.shape
    slab = 2 * GDN_HEAD_DIM
    nslab = GDN_MAIN_DIM // slab
    nb = w_ba.shape[1]
    nbt = w_bat.shape[0]
    tm = TM_PROJ
    nt = seq // tm

    def row_map(b, t):
        return (b * nt + t, 0)

    return pl.pallas_call(
        _in_proj_kernel,
        grid=(bsz, nt),
        in_specs=[
            pl.BlockSpec((tm, k), row_map),
            _const_spec((1, k)),
            pl.BlockSpec(memory_space=pl.ANY),
            _const_spec((k, nb)),
            _const_spec((nbt, k)),
            _const_spec((GDN_CONV_WIDTH, GDN_CONV_DIM)),
        ],
        out_specs=[
            pl.BlockSpec((tm, GDN_CONV_DIM), row_map),
            pl.BlockSpec((tm, GDN_V_DIM), row_map),
            pl.BlockSpec((tm, nb), row_map),
            pl.BlockSpec((nbt, tm), lambda b, t: (0, b * nt + t)),
        ],
        out_shape=[
            jax.ShapeDtypeStruct((m, GDN_CONV_DIM), BF16),
            jax.ShapeDtypeStruct((m, GDN_V_DIM), F32),
            jax.ShapeDtypeStruct((m, nb), F32),
            jax.ShapeDtypeStruct((nbt, m), F32),
        ],
        scratch_shapes=[pltpu.VMEM((SUBLANES, GDN_CONV_DIM), F32),
                        pltpu.VMEM((nslab, k, slab), BF16),
                        pltpu.SemaphoreType.DMA((nslab,))],
        compiler_params=_params("arbitrary", "arbitrary"),
        name="gdn_in_proj",
    )(x2, norm_w, w_in, w_ba, w_bat, conv_w)


def _gdn_core_kernel(qkv_ref, z_ref, ba_ref, bat_ref, alog_ref, dtb_ref,
                     alogt_ref, dtbt_ref, onw_ref, out_ref,
                     s_scr, gcx_scr, betap_scr, rows_scr, u_scr, lhs_scr, a_scr, kt_scr):
    t = pl.program_id(1)
    tm = qkv_ref.shape[0]
    c = GDN_CHUNK
    nchunk = tm // c
    hd = GDN_HEAD_DIM
    nh = GDN_V_HEADS
    npair = GDN_QK_HEADS
    heads = range(nh)
    pairs = range(npair)

    @pl.when(t == 0)
    def _():
        s_scr[...] = jnp.zeros_like(s_scr)

    beta = 1.0 / (1.0 + jnp.exp(-ba_ref[:, :nh]))
    a_lin = ba_ref[:, nh:2 * nh] + dtb_ref[...]
    g = -jnp.exp(alog_ref[...]) * (jnp.maximum(a_lin, 0.0) + jnp.log1p(jnp.exp(-jnp.abs(a_lin))))
    row = lax.broadcasted_iota(jnp.int32, (tm, tm), 0)
    col = lax.broadcasted_iota(jnp.int32, (tm, tm), 1)
    same = (row // c) == (col // c)
    lblk = (same & (col <= row)).astype(BF16)
    ublk = (same & (row <= col)).astype(BF16)
    gc = jnp.dot(jnp.concatenate([lblk] * 3, axis=1), jnp.concatenate(_split3(g), axis=0),
                 preferred_element_type=F32)
    e_row = lax.broadcasted_iota(jnp.int32, (3 * nh, npair * 2 * c), 0) % nh
    e_col = lax.broadcasted_iota(jnp.int32, (3 * nh, npair * 2 * c), 1)
    expand = (e_row == e_col // c).astype(BF16)
    betap_scr[...] = jnp.dot(jnp.concatenate(_split3(beta), axis=1), expand,
                             preferred_element_type=F32)
    e_row = lax.broadcasted_iota(jnp.int32, (3 * nh, nh * hd), 0) % nh
    e_col = lax.broadcasted_iota(jnp.int32, (3 * nh, nh * hd), 1)
    expand_hd = (e_row == e_col // hd).astype(BF16)
    gcx_scr[...] = jnp.dot(jnp.concatenate(_split3(gc), axis=1), expand_hd,
                           preferred_element_type=F32)

    beta_t = 1.0 / (1.0 + jnp.exp(-bat_ref[:nh, :]))
    a_lin_t = bat_ref[nh:, :] + dtbt_ref[...]
    g_t = -jnp.exp(alogt_ref[...]) * (jnp.maximum(a_lin_t, 0.0)
                                      + jnp.log1p(jnp.exp(-jnp.abs(a_lin_t))))
    gct = jnp.dot(jnp.concatenate(_split3(g_t), axis=1), jnp.concatenate([ublk] * 3, axis=0),
                  preferred_element_type=F32)
    begct = beta_t * jnp.exp(gct)
    for ci in range(nchunk):
        for slot, arr in enumerate((gct, beta_t, begct)):
            rows_scr[slot, ci] = jnp.concatenate([arr[:npair, ci * c:(ci + 1) * c],
                                                  arr[npair:, ci * c:(ci + 1) * c]], axis=1)

    ri = lax.broadcasted_iota(jnp.int32, (c, 2 * c), 0)
    cj = lax.broadcasted_iota(jnp.int32, (c, 2 * c), 1) % c
    left = lax.broadcasted_iota(jnp.int32, (1, 2 * c), 1) < c
    tril = cj <= ri
    strict = cj < ri
    eye = (ri == cj).astype(F32)
    onw = onw_ref[...]
    zero_hd = jnp.zeros((c, hd), BF16)

    def block_diag(xc):
        xb = xc.astype(BF16)
        zero = jnp.zeros_like(xb)
        return jnp.concatenate([jnp.where(left, xb, zero), jnp.where(left, zero, xb)], axis=0)

    def two_blocks(top, bottom):
        return jnp.concatenate([jnp.concatenate([top, zero_hd], axis=1),
                                jnp.concatenate([zero_hd, bottom], axis=1)], axis=0)

    def prepare(chunk_ids):
        items = [(n, j) for n in range(len(chunk_ids)) for j in pairs]
        rows = [pl.ds(pl.multiple_of(ci * c, c), c) for ci in chunk_ids]
        q = {(n, j): qkv_ref[rows[n], j * hd:(j + 1) * hd] for n, j in items}
        k = {(n, j): qkv_ref[rows[n], GDN_QK_DIM + j * hd:GDN_QK_DIM + (j + 1) * hd]
             for n, j in items}
        gq = {i: _dot_nt(jnp.concatenate([k[i], q[i]], axis=0), jnp.concatenate([k[i], k[i]], axis=0))
              for i in items}
        dec = {}
        for n, j in items:
            gc_pair = jnp.where(left, gcx_scr[rows[n], 2 * j * hd:2 * j * hd + 2 * c],
                                gcx_scr[rows[n], (2 * j + 1) * hd:(2 * j + 1) * hd + 2 * c])
            diff = gc_pair - rows_scr[0, chunk_ids[n], pl.ds(j, 1), :]
            dec[n, j] = jnp.where(tril, jnp.exp(jnp.where(tril, diff, 0.0)), 0.0)
        m = {(n, j): jnp.where(strict, betap_scr[rows[n], j * 2 * c:(j + 1) * 2 * c]
                               * gq[n, j][:c] * dec[n, j], 0.0) for n, j in items}
        for n, j in items:
            a_scr[chunk_ids[n], j] = (gq[n, j][c:] * dec[n, j]).astype(BF16)
        x = {i: eye - m[i] for i in items}
        p = {i: _dot(m[i], block_diag(m[i])) for i in items}
        levels = int(math.log2(c)) - 1
        for lvl in range(levels):
            if lvl + 1 < levels:
                xp = {i: _dot(jnp.concatenate([x[i], p[i]], axis=0), block_diag(p[i])) for i in items}
                x = {i: x[i] + xp[i][:c] for i in items}
                p = {i: xp[i][c:] for i in items}
            else:
                xp = {i: _dot(x[i], block_diag(p[i])) for i in items}
                x = {i: x[i] + xp[i] for i in items}
        u, w = {}, {}
        for n, j in items:
            ci = chunk_ids[n]
            t_u = x[n, j] * rows_scr[1, ci, pl.ds(j, 1), :]
            t_w = x[n, j] * rows_scr[2, ci, pl.ds(j, 1), :]
            v_a = qkv_ref[rows[n], 2 * GDN_QK_DIM + 2 * j * hd:2 * GDN_QK_DIM + (2 * j + 1) * hd]
            v_b = qkv_ref[rows[n], 2 * GDN_QK_DIM + (2 * j + 1) * hd:2 * GDN_QK_DIM + (2 * j + 2) * hd]
            u[n, j] = _dot(t_u, two_blocks(v_a, v_b))
            w[n, j] = _dot(t_w, two_blocks(k[n, j], k[n, j]))
        for n, j in items:
            ci = chunk_ids[n]
            for e in range(2):
                h = 2 * j + e
                u_scr[ci, h] = u[n, j][:, e * hd:(e + 1) * hd]
                lhs_scr[ci, h] = jnp.concatenate(
                    [w[n, j][:, e * hd:(e + 1) * hd].astype(BF16), q[n, j]], axis=0)
            if j % 2 == 0:
                k2 = qkv_ref[rows[n], GDN_QK_DIM + j * hd:GDN_QK_DIM + (j + 2) * hd]
                kt_scr[ci, j // 2] = jnp.concatenate(
                    [k2[:, :hd], k2[:, hd:]], axis=0).astype(F32).T.astype(BF16)

    def prepare_body(i, carry):
        prepare([i * PREP_CHUNKS + n for n in range(PREP_CHUNKS)])
        return carry

    lax.fori_loop(0, nchunk // PREP_CHUNKS, prepare_body, 0)

    def state_body(ci, carry):
        r0 = pl.multiple_of(ci * c, c)
        rows = pl.ds(r0, c)
        gcx = [gcx_scr[rows, h * hd:(h + 1) * hd] for h in heads]
        egc = [jnp.exp(gcx[h]) for h in heads]
        ekd = [jnp.exp(gcx[h][c - 1:c, :] - gcx[h]) for h in heads]
        s_old = [s_scr[j] for j in pairs]
        pq = [_dot(lhs_scr[ci, h], s_old[h // 2][:, (h % 2) * hd:(h % 2 + 1) * hd])
              for h in heads]
        vn = [u_scr[ci, h] - pq[h][:c] for h in heads]
        vnb = [vn[h].astype(BF16) for h in heads]
        avn = [_dot(a_scr[ci, j], two_blocks(vnb[2 * j], vnb[2 * j + 1])) for j in pairs]
        vs = [jnp.concatenate([vn[2 * j] * ekd[2 * j], vn[2 * j + 1] * ekd[2 * j + 1]],
                              axis=1).astype(BF16) for j in pairs]
        zero_vs = jnp.zeros_like(vs[0])
        ds = [_dot(kt_scr[ci, j // 2], jnp.concatenate(
            [vs[j], zero_vs] if j % 2 == 0 else [zero_vs, vs[j]], axis=0)) for j in pairs]
        for j in pairs:
            egl = jnp.exp(jnp.concatenate([gcx[2 * j][c - 1:c, :], gcx[2 * j + 1][c - 1:c, :]], axis=1))
            s_scr[j] = s_old[j] * egl + ds[j]
        for h in heads:
            o_h = (pq[h][c:] * egc[h]
                   + avn[h // 2][:, (h % 2) * hd:(h % 2 + 1) * hd])
            z_h = z_ref[rows, h * hd:(h + 1) * hd]
            out_ref[rows, h * hd:(h + 1) * hd] = (
                _rms_scale(o_h) * onw * _silu(z_h)).astype(out_ref.dtype)
        return carry

    lax.fori_loop(0, nchunk, state_body, 0)


def _gdn_core(qkv, z, ba, bat, a_log, dt_bias, out_norm_w, head_order, bsz, seq):
    m = qkv.shape[0]
    tm = TM_GDN
    nt = seq // tm
    nh = GDN_V_HEADS
    nb = ba.shape[1]
    nchunk = tm // GDN_CHUNK

    def cur_map(b, t):
        return (b * nt + t, 0)

    return pl.pallas_call(
        _gdn_core_kernel,
        grid=(bsz, nt),
        in_specs=[
            pl.BlockSpec((tm, GDN_CONV_DIM), cur_map),
            pl.BlockSpec((tm, GDN_V_DIM), cur_map),
            pl.BlockSpec((tm, nb), cur_map),
            pl.BlockSpec((2 * nh, tm), lambda b, t: (0, b * nt + t)),
            _const_spec((1, nh)),
            _const_spec((1, nh)),
            _const_spec((nh, 1)),
            _const_spec((nh, 1)),
            _const_spec((1, GDN_HEAD_DIM)),
        ],
        out_specs=pl.BlockSpec((tm, GDN_V_DIM), cur_map),
        out_shape=jax.ShapeDtypeStruct((m, GDN_V_DIM), BF16),
        scratch_shapes=[
            pltpu.VMEM((GDN_QK_HEADS, GDN_HEAD_DIM, 2 * GDN_HEAD_DIM), F32),
            pltpu.VMEM((tm, nh * GDN_HEAD_DIM), F32),
            pltpu.VMEM((tm, nh * GDN_CHUNK), F32),
            pltpu.VMEM((3, nchunk, GDN_QK_HEADS, 2 * GDN_CHUNK), F32),
            pltpu.VMEM((nchunk, nh, GDN_CHUNK, GDN_HEAD_DIM), F32),
            pltpu.VMEM((nchunk, nh, 2 * GDN_CHUNK, GDN_HEAD_DIM), BF16),
            pltpu.VMEM((nchunk, GDN_QK_HEADS, GDN_CHUNK, 2 * GDN_CHUNK), BF16),
            pltpu.VMEM((nchunk, GDN_QK_HEADS // 2, GDN_HEAD_DIM, 2 * GDN_CHUNK), BF16),
        ],
        compiler_params=_params("parallel", "arbitrary"),
        name="gdn_core",
    )(qkv, z, ba, bat, a_log.reshape(1, nh), dt_bias.reshape(1, nh),
      a_log[head_order].reshape(nh, 1), dt_bias[head_order].reshape(nh, 1),
      out_norm_w.reshape(1, GDN_HEAD_DIM))


def _ffn_kernel(res_ref, x_ref, wpre_ref, nw_ref, wup_ref, cw_ref, cb_ref, wdn_ref, fnw_ref, out_ref,
                carry_scr, act_scr, h_scr, *, final_norm):
    t = pl.program_id(1)

    @pl.when(t == 0)
    def _():
        carry_scr[...] = jnp.zeros_like(carry_scr)

    nsub = res_ref.shape[0] // FFN_SUB

    def up_phase(si):
        rows = slice(si * FFN_SUB, (si + 1) * FFN_SUB)
        h = res_ref[rows, :] + jnp.dot(x_ref[rows, :], wpre_ref[...], preferred_element_type=F32)
        h_scr[si] = h
        hn = (_rms_scale(h) * nw_ref[...]).astype(BF16)
        for c0 in range(0, D_FF, FFN_CHUNK):
            halves = []
            for off in (0, D_FF):
                sl = slice(off + c0, off + c0 + FFN_CHUNK)
                cur = jnp.dot(hn, wup_ref[:, sl], preferred_element_type=F32)
                prev8 = carry_scr[:, sl]
                carry_scr[:, sl] = cur[FFN_SUB - SUBLANES:]
                y = cur * cw_ref[FFN_CONV_WIDTH - 1:FFN_CONV_WIDTH, sl] + cb_ref[:, sl]
                for j in range(1, FFN_CONV_WIDTH):
                    y = y + (_shift_rows(cur, prev8, j)
                             * cw_ref[FFN_CONV_WIDTH - 1 - j:FFN_CONV_WIDTH - j, sl])
                halves.append(y)
            act_scr[si, :, c0:c0 + FFN_CHUNK] = (_silu_of_half(halves[0]) * halves[1]).astype(BF16)

    def down_phase(si):
        rows = slice(si * FFN_SUB, (si + 1) * FFN_SUB)
        act = act_scr[si]
        res = h_scr[si] + jnp.dot(act, wdn_ref[...], preferred_element_type=F32)
        if final_norm:
            res = _rms_scale(res) * fnw_ref[...]
        out_ref[rows, :] = res

    up_phase(0)
    for si in range(1, nsub):
        up_phase(si)
        down_phase(si - 1)
    down_phase(nsub - 1)


def _ffn(res, x2, w_pre, norm_w, w_up, conv_w, conv_b, w_down, final_w, bsz, seq, final_norm):
    m, k = res.shape
    tm = TM_FFN
    nt = seq // tm

    def row_map(b, t):
        return (b * nt + t, 0)

    return pl.pallas_call(
        functools.partial(_ffn_kernel, final_norm=final_norm),
        grid=(bsz, nt),
        in_specs=[
            pl.BlockSpec((tm, k), row_map),
            pl.BlockSpec((tm, x2.shape[1]), row_map),
            _resident_spec(w_pre.shape),
            _const_spec((1, k)),
            _resident_spec(w_up.shape),
            _const_spec((FFN_CONV_WIDTH, 2 * D_FF)),
            _const_spec((1, 2 * D_FF)),
            _resident_spec(w_down.shape),
            _const_spec((1, k)),
        ],
        out_specs=pl.BlockSpec((tm, k), row_map),
        out_shape=jax.ShapeDtypeStruct((m, k), F32),
        scratch_shapes=[
            pltpu.VMEM((SUBLANES, 2 * D_FF), F32),
            pltpu.VMEM((tm // FFN_SUB, FFN_SUB, D_FF), BF16),
            pltpu.VMEM((tm // FFN_SUB, FFN_SUB, k), F32),
        ],
        compiler_params=_params("parallel", "arbitrary"),
        name="ffn_final" if final_norm else "ffn",
    )(res, x2, w_pre, norm_w, w_up, conv_w, conv_b, w_down, final_w)


def _t5_bucket_map():
    qi = np.arange(SWA_BLOCK)[:, None]
    ki = np.arange(2 * SWA_BLOCK)[None, :]
    dist = qi + SWA_BLOCK - ki
    n = np.maximum(dist, 0)
    max_exact = REL_BUCKETS // 2
    nf = np.maximum(n, 1).astype(np.float32)
    large = max_exact + (np.log(nf / np.float32(max_exact)).astype(np.float32)
                         / np.float32(math.log(REL_MAX_DISTANCE / max_exact))
                         * np.float32(REL_BUCKETS - max_exact)).astype(np.int32)
    large = np.minimum(large, REL_BUCKETS - 1)
    bucket = np.where(n < max_exact, n, large).astype(np.int32)
    in_window = (dist >= 0) & (dist < SWA_WINDOW)
    return np.where(in_window, bucket, -1).astype(np.int32)


def _swa_kernel(table_ref, sinks_ref, h_ref, qnw_ref, kvnw_ref, wq_ref, wkv_ref, bmap_ref, out_ref,
                bias_scr, kprev_scr, vprev_scr):
    n = pl.program_id(1)
    blk = SWA_BLOCK
    hd = SWA_HEAD_DIM
    kvd = SWA_KV_HEADS * hd

    @pl.when(n == 0)
    def _():
        kprev_scr[...] = jnp.zeros_like(kprev_scr)
        vprev_scr[...] = jnp.zeros_like(vprev_scr)

    @pl.when((pl.program_id(0) == 0) & (n == 0))
    def _():
        bmap = bmap_ref[...]
        for hq in range(SWA_Q_HEADS):
            bias_scr[0, hq] = jnp.where(bmap < 0, NEG_INF, 0.0)

        def bucket_body(b, carry):
            hit = bmap == b
            for hq in range(SWA_Q_HEADS):
                bias_scr[0, hq] = jnp.where(hit, table_ref[b, hq], bias_scr[0, hq])
            return carry

        lax.fori_loop(0, REL_BUCKETS, bucket_body, 0)
        key_col = lax.broadcasted_iota(jnp.int32, (blk, 2 * blk), 1)
        for hq in range(SWA_Q_HEADS):
            bias_scr[1, hq] = jnp.where(key_col >= blk, bias_scr[0, hq], NEG_INF)
        for hq in range(SWA_Q_HEADS):
            for tbl in range(2):
                bias_scr[tbl, hq] = jnp.where(key_col == 0, sinks_ref[hq], bias_scr[tbl, hq])

    y = _rms_scale(h_ref[...])
    q_all = (jnp.dot((y * qnw_ref[...]).astype(BF16), wq_ref[...], preferred_element_type=F32)
             * (hd ** -0.5)).astype(BF16)
    kv_all = jnp.dot((y * kvnw_ref[...]).astype(BF16), wkv_ref[...],
                     preferred_element_type=F32).astype(BF16)

    lane = lax.broadcasted_iota(jnp.int32, (1, LANES), 1)
    lo = lane < hd
    ones_blk = jnp.ones((2 * blk, LANES), BF16)
    not_row0 = lax.broadcasted_iota(jnp.int32, (blk, 1), 0) > 0
    kv_heads = range(SWA_KV_HEADS)
    subs = range(h_ref.shape[0] // blk)
    items = [(sb, h) for sb in subs for h in kv_heads]
    d_row = lax.broadcasted_iota(jnp.int32, (LANES, LANES), 0)
    d_col = lax.broadcasted_iota(jnp.int32, (LANES, LANES), 1) % hd
    k_blk, v_blk = {}, {}
    for h in kv_heads:
        grp = (h * hd) // LANES
        dup = (d_row == d_col + (h * hd) % LANES).astype(BF16)
        k_all = jnp.dot(kv_all[:, grp * LANES:(grp + 1) * LANES], dup,
                        preferred_element_type=F32).astype(BF16)
        v_all = jnp.dot(kv_all[:, kvd + grp * LANES:kvd + (grp + 1) * LANES], dup,
                        preferred_element_type=F32).astype(BF16)
        k_blk[-1, h] = kprev_scr[h]
        v_blk[-1, h] = vprev_scr[h]
        for sb in subs:
            k_blk[sb, h] = k_all[sb * blk:(sb + 1) * blk]
            v_blk[sb, h] = v_all[sb * blk:(sb + 1) * blk]
        for sb in subs:
            if sb + 1 < len(subs):
                k_blk[sb, h, "prev"] = jnp.where(not_row0, k_blk[sb, h], jnp.zeros_like(k_blk[sb, h]))
                v_blk[sb, h, "prev"] = jnp.where(not_row0, v_blk[sb, h], jnp.zeros_like(v_blk[sb, h]))
        last = len(subs) - 1
        kprev_scr[h] = jnp.where(not_row0, k_blk[last, h], jnp.zeros_like(k_blk[last, h]))
        vprev_scr[h] = jnp.where(not_row0, v_blk[last, h], jnp.zeros_like(v_blk[last, h]))

    def prev_of(blocks, sb, h):
        return blocks[-1, h] if sb == 0 else blocks[sb - 1, h, "prev"]

    k_dup = {(sb, h): jnp.concatenate([prev_of(k_blk, sb, h), k_blk[sb, h]], axis=0) for sb, h in items}
    v_aug = {(sb, h): jnp.concatenate(
        [jnp.concatenate([prev_of(v_blk, sb, h), v_blk[sb, h]], axis=0), ones_blk], axis=1)
        for sb, h in items}
    q_stack = {}
    for sb, h in items:
        q_rows = []
        for gidx in range(SWA_GROUP):
            hq = h * SWA_GROUP + gidx
            qg = q_all[sb * blk:(sb + 1) * blk, (hq * hd) // LANES * LANES:((hq * hd) // LANES + 1) * LANES]
            sel = lo if (hq * hd) % LANES == 0 else ~lo
            q_rows.append(jnp.where(sel, qg, jnp.zeros_like(qg)))
        q_stack[sb, h] = jnp.concatenate(q_rows, axis=0)
    s = {i: _dot_nt(q_stack[i], k_dup[i]) for i in items}
    first = [(n == 0).astype(jnp.int32) if sb == 0 else 0 for sb in subs]
    sbias = {(sb, h): s[sb, h] + bias_scr[first[sb], pl.ds(h * SWA_GROUP, SWA_GROUP)].reshape(
        SWA_GROUP * blk, 2 * blk) for sb, h in items}
    mx = {i: jnp.max(sbias[i], axis=-1, keepdims=True) for i in items}
    p = {i: jnp.exp(sbias[i] - mx[i]).astype(BF16) for i in items}
    oa = {i: jnp.dot(p[i], v_aug[i], preferred_element_type=F32) for i in items}
    outs = {i: oa[i][:, :LANES] / oa[i][:, LANES:] for i in items}
    for sb, h in items:
        for pair in range(SWA_GROUP // 2):
            hq0 = h * SWA_GROUP + 2 * pair
            col = (hq0 * hd) // LANES * LANES
            even = outs[sb, h][(2 * pair) * blk:(2 * pair + 1) * blk]
            odd = outs[sb, h][(2 * pair + 1) * blk:(2 * pair + 2) * blk]
            out_ref[sb * blk:(sb + 1) * blk, col:col + LANES] = jnp.where(lo, even, odd).astype(
                out_ref.dtype)


def _swa(h2, q_norm_w, kv_norm_w, w_q, w_kv, rel_table, sinks, bsz, seq):
    m, k = h2.shape
    nq = w_q.shape[1]
    nb = seq // SWA_STEP
    bmap = jnp.asarray(_t5_bucket_map())
    grid_spec = pltpu.PrefetchScalarGridSpec(
        num_scalar_prefetch=0,
        grid=(bsz, nb),
        in_specs=[
            pl.BlockSpec(memory_space=pltpu.SMEM),
            pl.BlockSpec(memory_space=pltpu.SMEM),
            pl.BlockSpec((SWA_STEP, k), lambda b, n: (b * nb + n, 0)),
            _const_spec((1, k)),
            _const_spec((1, k)),
            _resident_spec(w_q.shape),
            _resident_spec(w_kv.shape),
            _const_spec((SWA_BLOCK, 2 * SWA_BLOCK)),
        ],
        out_specs=pl.BlockSpec((SWA_STEP, nq), lambda b, n: (b * nb + n, 0)),
        scratch_shapes=[pltpu.VMEM((2, SWA_Q_HEADS, SWA_BLOCK, 2 * SWA_BLOCK), F32),
                        pltpu.VMEM((SWA_KV_HEADS, SWA_BLOCK, LANES), BF16),
                        pltpu.VMEM((SWA_KV_HEADS, SWA_BLOCK, LANES), BF16)],
    )
    return pl.pallas_call(
        _swa_kernel,
        grid_spec=grid_spec,
        out_shape=jax.ShapeDtypeStruct((m, nq), BF16),
        compiler_params=_params("arbitrary", "arbitrary"),
        name="swa",
    )(rel_table, sinks, h2, q_norm_w, kv_norm_w, w_q, w_kv, bmap)


def kernel(x, a_norm_w, a_w_in, a_conv_w, a_a_log, a_dt_bias, a_out_norm_w, a_w_out,
           kv_norm_w, w_kv, b_norm_w, b_w_q, b_sinks, b_w_o, rel_bias_table,
           ffn_norm_w, ffn_w_up, ffn_conv_w, ffn_conv_b, ffn_w_down, final_norm_w):
    bsz, seq, d = x.shape
    n_a = a_w_in.shape[0]
    n_b = b_w_q.shape[0]
    depth = n_a + n_b
    h = x.reshape(bsz * seq, d)
    head_order = np.concatenate([np.arange(0, GDN_V_HEADS, 2), np.arange(1, GDN_V_HEADS, 2)])
    gate_half = jnp.where(jnp.arange(2 * D_FF) < D_FF, 0.5, 1.0).astype(F32)
    for layer in range(depth):
        if layer < n_a:
            i = layer
            w_in = a_w_in[i].astype(BF16)
            w_ba = w_in[:, GDN_MAIN_DIM:]
            w_at = w_ba[:, np.concatenate([head_order, GDN_V_HEADS + head_order])].T
            qkv, z, ba, bat = _in_proj(h, a_norm_w[i].reshape(1, d), w_in, w_ba, w_at,
                                       0.5 * a_conv_w[i], bsz, seq)
            o = _gdn_core(qkv, z, ba, bat, a_a_log[i], a_dt_bias[i], a_out_norm_w[i],
                          head_order, bsz, seq)
            w_mix = a_w_out[i].astype(BF16)
        else:
            j = layer - n_a
            assert n_b == 1, "shared-KV reuse across several attention layers is not implemented"
            o = _swa(h, b_norm_w[j].reshape(1, d), kv_norm_w.reshape(1, d),
                     b_w_q[j].astype(BF16), w_kv.astype(BF16), rel_bias_table, b_sinks[j], bsz, seq)
            w_mix = b_w_o[j].astype(BF16)
        last = layer == depth - 1
        h = _ffn(h, o, w_mix, ffn_norm_w[layer].reshape(1, d), ffn_w_up[layer].astype(BF16),
                 ffn_conv_w[layer] * gate_half, (ffn_conv_b[layer] * gate_half).reshape(1, 2 * D_FF),
                 ffn_w_down[layer].astype(BF16), final_norm_w.reshape(1, d), bsz, seq, last)
    return h.reshape(bsz, seq, d)
```

```python
import functools
import math

import jax
import jax.numpy as jnp
import numpy as np
from jax import lax
from jax.experimental import pallas as pl
from jax.experimental.pallas import tpu as pltpu

F32 = jnp.float32
BF16 = jnp.bfloat16

EPS = 1e-6
NEG_INF = -1e30

D_MODEL = 1024
GDN_QK_HEADS = 8
GDN_V_HEADS = 16
GDN_HEAD_DIM = 128
GDN_QK_DIM = GDN_QK_HEADS * GDN_HEAD_DIM
GDN_V_DIM = GDN_V_HEADS * GDN_HEAD_DIM
GDN_CONV_DIM = 2 * GDN_QK_DIM + GDN_V_DIM
GDN_MAIN_DIM = GDN_CONV_DIM + GDN_V_DIM
GDN_CONV_WIDTH = 4
GDN_CHUNK = 64

SWA_Q_HEADS = 16
SWA_KV_HEADS = 4
SWA_GROUP = SWA_Q_HEADS // SWA_KV_HEADS
SWA_HEAD_DIM = 64
SWA_WINDOW = 128
SWA_BLOCK = 128
SWA_STEP = 512
REL_BUCKETS = 32
REL_MAX_DISTANCE = 128

D_FF = 2816
FFN_CONV_WIDTH = 3
FFN_CHUNK = 256
CONV_ROWS = 64
PREP_CHUNKS = 4

SUBLANES = 8
LANES = 128
VMEM_LIMIT_BYTES = 56 * 1024 * 1024

TM_PROJ = 256
TM_GDN = 256
TM_FFN = 512
FFN_SUB = 256


def _params(*sem):
    return pltpu.CompilerParams(dimension_semantics=sem, vmem_limit_bytes=VMEM_LIMIT_BYTES)


def _const_spec(shape):
    zeros = (0,) * len(shape)
    return pl.BlockSpec(shape, lambda *_: zeros)


def _resident_spec(shape):
    zeros = (0,) * len(shape)
    return pl.BlockSpec(shape, lambda *_: zeros, pipeline_mode=pl.Buffered(1))


def _layer_spec(stacked_shape, layer):
    index = (layer,) + (0,) * (len(stacked_shape) - 1)
    return pl.BlockSpec((None,) + tuple(stacked_shape[1:]), lambda *_: index,
                        pipeline_mode=pl.Buffered(1))


def _dot(a, b):
    return jnp.dot(a.astype(BF16), b.astype(BF16), preferred_element_type=F32)


def _dot_nt(a, b):
    return lax.dot_general(a.astype(BF16), b.astype(BF16), (((1,), (1,)), ((), ())),
                           preferred_element_type=F32)


def _rms_scale(x):
    return x * lax.rsqrt(jnp.mean(x * x, axis=-1, keepdims=True) + EPS)


def _silu(x):
    return _silu_of_half(0.5 * x)


def _silu_of_half(half):
    return half + half * jnp.tanh(half)


def _shift_rows(cur, prev8, j):
    if j == 0:
        return cur
    nrow, ncol = cur.shape
    tiles = jnp.concatenate([prev8, cur], axis=0).reshape(nrow // SUBLANES + 1, SUBLANES, ncol)
    rot = pltpu.roll(tiles, j, 1)
    row = lax.broadcasted_iota(jnp.int32, (1, SUBLANES, ncol), 1)
    return jnp.where(row < j, rot[:-1], rot[1:]).reshape(nrow, ncol)


def _split3(x):
    hi = x.astype(BF16)
    r1 = x - hi.astype(F32)
    mid = r1.astype(BF16)
    lo = (r1 - mid.astype(F32)).astype(BF16)
    return hi, mid, lo


def _in_proj_kernel(x_ref, nw_ref, w_hbm, wba_ref, wbat_ref, cw_ref,
                    qkv_ref, z_ref, ba_ref, bat_ref, carry_scr, wm_ref, w_sem):
    t = pl.program_id(1)

    @pl.when(t == 0)
    def _():
        carry_scr[...] = jnp.zeros_like(carry_scr)

    @pl.when((pl.program_id(0) == 0) & (t == 0))
    def _():
        def slab_copy(ci):
            return pltpu.make_async_copy(w_hbm.at[:, pl.ds(ci * wm_ref.shape[2], wm_ref.shape[2])],
                                         wm_ref.at[ci], w_sem.at[ci])

        for ci in range(wm_ref.shape[0]):
            slab_copy(ci).start()
        for ci in range(wm_ref.shape[0]):
            slab_copy(ci).wait()

    tm = x_ref.shape[0]
    hd = GDN_HEAD_DIM
    hn = (_rms_scale(x_ref[...]) * nw_ref[...]).astype(BF16)
    step = 2 * hd
    zevery = GDN_CONV_DIM // GDN_V_DIM
    for c0 in range(0, GDN_CONV_DIM, step):
        if (c0 // step) % zevery == 0:
            z0 = c0 // zevery
            z_ref[:, z0:z0 + step] = jnp.dot(
                hn, wm_ref[(GDN_CONV_DIM + z0) // step], preferred_element_type=F32)
        sl = slice(c0, c0 + step)
        cur = jnp.dot(hn, wm_ref[c0 // step], preferred_element_type=F32)
        prev8 = carry_scr[:, sl]
        carry_scr[:, sl] = cur[tm - SUBLANES:]
        for hc in range(c0, c0 + step, hd):
            lanes = slice(hc - c0, hc - c0 + hd)
            wts = [cw_ref[GDN_CONV_WIDTH - 1 - j:GDN_CONV_WIDTH - j, hc:hc + hd]
                   for j in range(GDN_CONV_WIDTH)]
            for r0 in range(0, tm, CONV_ROWS):
                piece = cur[r0:r0 + CONV_ROWS, lanes]
                before = prev8[:, lanes] if r0 == 0 else cur[r0 - SUBLANES:r0, lanes]
                yh = piece * wts[0]
                for j in range(1, GDN_CONV_WIDTH):
                    yh = yh + _shift_rows(piece, before, j) * wts[j]
                yh = _silu_of_half(yh)
                if hc < 2 * GDN_QK_DIM:
                    yh = yh * lax.rsqrt(jnp.sum(yh * yh, axis=-1, keepdims=True) + EPS)
                    if hc < GDN_QK_DIM:
                        yh = yh * (hd ** -0.5)
                qkv_ref[r0:r0 + CONV_ROWS, hc:hc + hd] = yh.astype(qkv_ref.dtype)
    ba_ref[...] = jnp.dot(hn, wba_ref[...], preferred_element_type=F32)
    bat_ref[...] = _dot_nt(wbat_ref[...], hn)


def _in_proj(x2, norm_w, w_in, w_ba, w_bat, conv_w, bsz, seq):
    m, k = x2.shape
    slab = 2 * GDN_HEAD_DIM
    nslab = GDN_MAIN_DIM // slab
    nb = w_ba.shape[1]
    nbt = w_bat.shape[0]
    tm = TM_PROJ
    nt = seq // tm

    def row_map(b, t):
        return (b * nt + t, 0)

    return pl.pallas_call(
        _in_proj_kernel,
        grid=(bsz, nt),
        in_specs=[
            pl.BlockSpec((tm, k), row_map),
            _const_spec((1, k)),
            pl.BlockSpec(memory_space=pl.ANY),
            _const_spec((k, nb)),
            _const_spec((nbt, k)),
            _const_spec((GDN_CONV_WIDTH, GDN_CONV_DIM)),
        ],
        out_specs=[
            pl.BlockSpec((tm, GDN_CONV_DIM), row_map),
            pl.BlockSpec((tm, GDN_V_DIM), row_map),
            pl.BlockSpec((tm, nb), row_map),
            pl.BlockSpec((nbt, tm), lambda b, t: (0, b * nt + t)),
        ],
        out_shape=[
            jax.ShapeDtypeStruct((m, GDN_CONV_DIM), BF16),
            jax.ShapeDtypeStruct((m, GDN_V_DIM), F32),
            jax.ShapeDtypeStruct((m, nb), F32),
            jax.ShapeDtypeStruct((nbt, m), F32),
        ],
        scratch_shapes=[pltpu.VMEM((SUBLANES, GDN_CONV_DIM), F32),
                        pltpu.VMEM((nslab, k, slab), BF16),
                        pltpu.SemaphoreType.DMA((nslab,))],
        compiler_params=_params("arbitrary", "arbitrary"),
        name="gdn_in_proj",
    )(x2, norm_w, w_in, w_ba, w_bat, conv_w)


def _gdn_core_kernel(qkv_ref, z_ref, ba_ref, bat_ref, alog_ref, dtb_ref,
                     alogt_ref, dtbt_ref, onw_ref, out_ref,
                     s_scr, gcx_scr, betap_scr, rows_scr, u_scr, lhs_scr, a_scr, kt_scr):
    t = pl.program_id(1)
    tm = qkv_ref.shape[0]
    c = GDN_CHUNK
    nchunk = tm // c
    hd = GDN_HEAD_DIM
    nh = GDN_V_HEADS
    npair = GDN_QK_HEADS
    heads = range(nh)
    pairs = range(npair)

    @pl.when(t == 0)
    def _():
        s_scr[...] = jnp.zeros_like(s_scr)

    beta = 1.0 / (1.0 + jnp.exp(-ba_ref[:, :nh]))
    a_lin = ba_ref[:, nh:2 * nh] + dtb_ref[...]
    g = -jnp.exp(alog_ref[...]) * (jnp.maximum(a_lin, 0.0) + jnp.log1p(jnp.exp(-jnp.abs(a_lin))))
    row = lax.broadcasted_iota(jnp.int32, (tm, tm), 0)
    col = lax.broadcasted_iota(jnp.int32, (tm, tm), 1)
    same = (row // c) == (col // c)
    lblk = (same & (col <= row)).astype(BF16)
    ublk = (same & (row <= col)).astype(BF16)
    gc = jnp.dot(jnp.concatenate([lblk] * 3, axis=1), jnp.concatenate(_split3(g), axis=0),
                 preferred_element_type=F32)
    e_row = lax.broadcasted_iota(jnp.int32, (3 * nh, npair * 2 * c), 0) % nh
    e_col = lax.broadcasted_iota(jnp.int32, (3 * nh, npair * 2 * c), 1)
    expand = (e_row == e_col // c).astype(BF16)
    betap_scr[...] = jnp.dot(jnp.concatenate(_split3(beta), axis=1), expand,
                             preferred_element_type=F32)
    e_row = lax.broadcasted_iota(jnp.int32, (3 * nh, nh * hd), 0) % nh
    e_col = lax.broadcasted_iota(jnp.int32, (3 * nh, nh * hd), 1)
    expand_hd = (e_row == e_col // hd).astype(BF16)
    gcx_scr[...] = jnp.dot(jnp.concatenate(_split3(gc), axis=1), expand_hd,
                           preferred_element_type=F32)

    beta_t = 1.0 / (1.0 + jnp.exp(-bat_ref[:nh, :]))
    a_lin_t = bat_ref[nh:, :] + dtbt_ref[...]
    g_t = -jnp.exp(alogt_ref[...]) * (jnp.maximum(a_lin_t, 0.0)
                                      + jnp.log1p(jnp.exp(-jnp.abs(a_lin_t))))
    gct = jnp.dot(jnp.concatenate(_split3(g_t), axis=1), jnp.concatenate([ublk] * 3, axis=0),
                  preferred_element_type=F32)
    begct = beta_t * jnp.exp(gct)
    for ci in range(nchunk):
        for slot, arr in enumerate((gct, beta_t, begct)):
            rows_scr[slot, ci] = jnp.concatenate([arr[:npair, ci * c:(ci + 1) * c],
                                                  arr[npair:, ci * c:(ci + 1) * c]], axis=1)

    ri = lax.broadcasted_iota(jnp.int32, (c, 2 * c), 0)
    cj = lax.broadcasted_iota(jnp.int32, (c, 2 * c), 1) % c
    left = lax.broadcasted_iota(jnp.int32, (1, 2 * c), 1) < c
    tril = cj <= ri
    strict = cj < ri
    eye = (ri == cj).astype(F32)
    onw = onw_ref[...]
    zero_hd = jnp.zeros((c, hd), BF16)

    def block_diag(xc):
        xb = xc.astype(BF16)
        zero = jnp.zeros_like(xb)
        return jnp.concatenate([jnp.where(left, xb, zero), jnp.where(left, zero, xb)], axis=0)

    def two_blocks(top, bottom):
        return jnp.concatenate([jnp.concatenate([top, zero_hd], axis=1),
                                jnp.concatenate([zero_hd, bottom], axis=1)], axis=0)

    def prepare(chunk_ids):
        items = [(n, j) for n in range(len(chunk_ids)) for j in pairs]
        rows = [pl.ds(pl.multiple_of(ci * c, c), c) for ci in chunk_ids]
        q = {(n, j): qkv_ref[rows[n], j * hd:(j + 1) * hd] for n, j in items}
        k = {(n, j): qkv_ref[rows[n], GDN_QK_DIM + j * hd:GDN_QK_DIM + (j + 1) * hd]
             for n, j in items}
        gq = {i: _dot_nt(jnp.concatenate([k[i], q[i]], axis=0), jnp.concatenate([k[i], k[i]], axis=0))
              for i in items}
        dec = {}
        for n, j in items:
            gc_pair = jnp.where(left, gcx_scr[rows[n], 2 * j * hd:2 * j * hd + 2 * c],
                                gcx_scr[rows[n], (2 * j + 1) * hd:(2 * j + 1) * hd + 2 * c])
            diff = gc_pair - rows_scr[0, chunk_ids[n], pl.ds(j, 1), :]
            dec[n, j] = jnp.where(tril, jnp.exp(jnp.where(tril, diff, 0.0)), 0.0)
        m = {(n, j): jnp.where(strict, betap_scr[rows[n], j * 2 * c:(j + 1) * 2 * c]
                               * gq[n, j][:c] * dec[n, j], 0.0) for n, j in items}
        for n, j in items:
            a_scr[chunk_ids[n], j] = (gq[n, j][c:] * dec[n, j]).astype(BF16)
        x = {i: eye - m[i] for i in items}
        p = {i: _dot(m[i], block_diag(m[i])) for i in items}
        levels = int(math.log2(c)) - 1
        for lvl in range(levels):
            if lvl + 1 < levels:
                xp = {i: _dot(jnp.concatenate([x[i], p[i]], axis=0), block_diag(p[i])) for i in items}
                x = {i: x[i] + xp[i][:c] for i in items}
                p = {i: xp[i][c:] for i in items}
            else:
                xp = {i: _dot(x[i], block_diag(p[i])) for i in items}
                x = {i: x[i] + xp[i] for i in items}
        u, w = {}, {}
        for n, j in items:
            ci = chunk_ids[n]
            t_u = x[n, j] * rows_scr[1, ci, pl.ds(j, 1), :]
            t_w = x[n, j] * rows_scr[2, ci, pl.ds(j, 1), :]
            v_a = qkv_ref[rows[n], 2 * GDN_QK_DIM + 2 * j * hd:2 * GDN_QK_DIM + (2 * j + 1) * hd]
            v_b = qkv_ref[rows[n], 2 * GDN_QK_DIM + (2 * j + 1) * hd:2 * GDN_QK_DIM + (2 * j + 2) * hd]
            u[n, j] = _dot(t_u, two_blocks(v_a, v_b))
            w[n, j] = _dot(t_w, two_blocks(k[n, j], k[n, j]))
        for n, j in items:
            ci = chunk_ids[n]
            for e in range(2):
                h = 2 * j + e
                u_scr[ci, h] = u[n, j][:, e * hd:(e + 1) * hd]
                lhs_scr[ci, h] = jnp.concatenate(
                    [w[n, j][:, e * hd:(e + 1) * hd].astype(BF16), q[n, j]], axis=0)
            if j % 2 == 0:
                k2 = qkv_ref[rows[n], GDN_QK_DIM + j * hd:GDN_QK_DIM + (j + 2) * hd]
                kt_scr[ci, j // 2] = jnp.concatenate(
                    [k2[:, :hd], k2[:, hd:]], axis=0).astype(F32).T.astype(BF16)

    def prepare_body(i, carry):
        prepare([i * PREP_CHUNKS + n for n in range(PREP_CHUNKS)])
        return carry

    lax.fori_loop(0, nchunk // PREP_CHUNKS, prepare_body, 0)

    def state_body(ci, carry):
        r0 = pl.multiple_of(ci * c, c)
        rows = pl.ds(r0, c)
        gcx = [gcx_scr[rows, h * hd:(h + 1) * hd] for h in heads]
        egc = [jnp.exp(gcx[h]) for h in heads]
        ekd = [jnp.exp(gcx[h][c - 1:c, :] - gcx[h]) for h in heads]
        s_old = [s_scr[j] for j in pairs]
        pq = [_dot(lhs_scr[ci, h], s_old[h // 2][:, (h % 2) * hd:(h % 2 + 1) * hd])
              for h in heads]
        vn = [u_scr[ci, h] - pq[h][:c] for h in heads]
        vnb = [vn[h].astype(BF16) for h in heads]
        avn = [_dot(a_scr[ci, j], two_blocks(vnb[2 * j], vnb[2 * j + 1])) for j in pairs]
        vs = [jnp.concatenate([vn[2 * j] * ekd[2 * j], vn[2 * j + 1] * ekd[2 * j + 1]],
                              axis=1).astype(BF16) for j in pairs]
        zero_vs = jnp.zeros_like(vs[0])
        ds = [_dot(kt_scr[ci, j // 2], jnp.concatenate(
            [vs[j], zero_vs] if j % 2 == 0 else [zero_vs, vs[j]], axis=0)) for j in pairs]
        for j in pairs:
            egl = jnp.exp(jnp.concatenate([gcx[2 * j][c - 1:c, :], gcx[2 * j + 1][c - 1:c, :]], axis=1))
            s_scr[j] = s_old[j] * egl + ds[j]
        for h in heads:
            o_h = (pq[h][c:] * egc[h]
                   + avn[h // 2][:, (h % 2) * hd:(h % 2 + 1) * hd])
            z_h = z_ref[rows, h * hd:(h + 1) * hd]
            out_ref[rows, h * hd:(h + 1) * hd] = (
                _rms_scale(o_h) * onw * _silu(z_h)).astype(out_ref.dtype)
        return carry

    lax.fori_loop(0, nchunk, state_body, 0)


def _gdn_core(qkv, z, ba, bat, a_log, dt_bias, out_norm_w, head_order, bsz, seq):
    m = qkv.shape[0]
    tm = TM_GDN
    nt = seq // tm
    nh = GDN_V_HEADS
    nb = ba.shape[1]
    nchunk = tm // GDN_CHUNK

    def cur_map(b, t):
        return (b * nt + t, 0)

    return pl.pallas_call(
        _gdn_core_kernel,
        grid=(bsz, nt),
        in_specs=[
            pl.BlockSpec((tm, GDN_CONV_DIM), cur_map),
            pl.BlockSpec((tm, GDN_V_DIM), cur_map),
            pl.BlockSpec((tm, nb), cur_map),
            pl.BlockSpec((2 * nh, tm), lambda b, t: (0, b * nt + t)),
            _const_spec((1, nh)),
            _const_spec((1, nh)),
            _const_spec((nh, 1)),
            _const_spec((nh, 1)),
            _const_spec((1, GDN_HEAD_DIM)),
        ],
        out_specs=pl.BlockSpec((tm, GDN_V_DIM), cur_map),
        out_shape=jax.ShapeDtypeStruct((m, GDN_V_DIM), BF16),
        scratch_shapes=[
            pltpu.VMEM((GDN_QK_HEADS, GDN_HEAD_DIM, 2 * GDN_HEAD_DIM), F32),
            pltpu.VMEM((tm, nh * GDN_HEAD_DIM), F32),
            pltpu.VMEM((tm, nh * GDN_CHUNK), F32),
            pltpu.VMEM((3, nchunk, GDN_QK_HEADS, 2 * GDN_CHUNK), F32),
            pltpu.VMEM((nchunk, nh, GDN_CHUNK, GDN_HEAD_DIM), F32),
            pltpu.VMEM((nchunk, nh, 2 * GDN_CHUNK, GDN_HEAD_DIM), BF16),
            pltpu.VMEM((nchunk, GDN_QK_HEADS, GDN_CHUNK, 2 * GDN_CHUNK), BF16),
            pltpu.VMEM((nchunk, GDN_QK_HEADS // 2, GDN_HEAD_DIM, 2 * GDN_CHUNK), BF16),
        ],
        compiler_params=_params("parallel", "arbitrary"),
        name="gdn_core",
    )(qkv, z, ba, bat, a_log.reshape(1, nh), dt_bias.reshape(1, nh),
      a_log[head_order].reshape(nh, 1), dt_bias[head_order].reshape(nh, 1),
      out_norm_w.reshape(1, GDN_HEAD_DIM))


def _ffn_kernel(res_ref, x_ref, wpre_ref, nw_ref, wup_ref, cw_ref, cb_ref, wdn_ref, fnw_ref, out_ref,
                carry_scr, act_scr, h_scr, *, final_norm):
    t = pl.program_id(1)

    @pl.when(t == 0)
    def _():
        carry_scr[...] = jnp.zeros_like(carry_scr)

    nsub = res_ref.shape[0] // FFN_SUB

    def up_phase(si):
        rows = slice(si * FFN_SUB, (si + 1) * FFN_SUB)
        h = res_ref[rows, :] + jnp.dot(x_ref[rows, :], wpre_ref[...], preferred_element_type=F32)
        h_scr[si] = h
        hn = (_rms_scale(h) * nw_ref[...]).astype(BF16)
        for c0 in range(0, D_FF, FFN_CHUNK):
            halves = []
            for off in (0, D_FF):
                sl = slice(off + c0, off + c0 + FFN_CHUNK)
                cur = jnp.dot(hn, wup_ref[:, sl], preferred_element_type=F32)
                prev8 = carry_scr[:, sl]
                carry_scr[:, sl] = cur[FFN_SUB - SUBLANES:]
                y = cur * cw_ref[FFN_CONV_WIDTH - 1:FFN_CONV_WIDTH, sl] + cb_ref[:, sl]
                for j in range(1, FFN_CONV_WIDTH):
                    y = y + (_shift_rows(cur, prev8, j)
                             * cw_ref[FFN_CONV_WIDTH - 1 - j:FFN_CONV_WIDTH - j, sl])
                halves.append(y)
            act_scr[si, :, c0:c0 + FFN_CHUNK] = (_silu_of_half(halves[0]) * halves[1]).astype(BF16)

    def down_phase(si):
        rows = slice(si * FFN_SUB, (si + 1) * FFN_SUB)
        act = act_scr[si]
        res = h_scr[si] + jnp.dot(act, wdn_ref[...], preferred_element_type=F32)
        if final_norm:
            res = _rms_scale(res) * fnw_ref[...]
        out_ref[rows, :] = res

    up_phase(0)
    for si in range(1, nsub):
        up_phase(si)
        down_phase(si - 1)
    down_phase(nsub - 1)


def _ffn(res, x2, w_pre, norm_w, w_up_all, conv_w, conv_b, w_down_all, layer, final_w, bsz, seq,
         final_norm):
    m, k = res.shape
    tm = TM_FFN
    nt = seq // tm

    def row_map(b, t):
        return (b * nt + t, 0)

    return pl.pallas_call(
        functools.partial(_ffn_kernel, final_norm=final_norm),
        grid=(bsz, nt),
        in_specs=[
            pl.BlockSpec((tm, k), row_map),
            pl.BlockSpec((tm, x2.shape[1]), row_map),
            _resident_spec(w_pre.shape),
            _const_spec((1, k)),
            _layer_spec(w_up_all.shape, layer),
            _const_spec((FFN_CONV_WIDTH, 2 * D_FF)),
            _const_spec((1, 2 * D_FF)),
            _layer_spec(w_down_all.shape, layer),
            _const_spec((1, k)),
        ],
        out_specs=pl.BlockSpec((tm, k), row_map),
        out_shape=jax.ShapeDtypeStruct((m, k), F32),
        scratch_shapes=[
            pltpu.VMEM((SUBLANES, 2 * D_FF), F32),
            pltpu.VMEM((tm // FFN_SUB, FFN_SUB, D_FF), BF16),
            pltpu.VMEM((tm // FFN_SUB, FFN_SUB, k), F32),
        ],
        compiler_params=_params("parallel", "arbitrary"),
        name="ffn_final" if final_norm else "ffn",
    )(res, x2, w_pre, norm_w, w_up_all, conv_w, conv_b, w_down_all, final_w)


def _t5_bucket_map():
    qi = np.arange(SWA_BLOCK)[:, None]
    ki = np.arange(2 * SWA_BLOCK)[None, :]
    dist = qi + SWA_BLOCK - ki
    n = np.maximum(dist, 0)
    max_exact = REL_BUCKETS // 2
    nf = np.maximum(n, 1).astype(np.float32)
    large = max_exact + (np.log(nf / np.float32(max_exact)).astype(np.float32)
                         / np.float32(math.log(REL_MAX_DISTANCE / max_exact))
                         * np.float32(REL_BUCKETS - max_exact)).astype(np.int32)
    large = np.minimum(large, REL_BUCKETS - 1)
    bucket = np.where(n < max_exact, n, large).astype(np.int32)
    in_window = (dist >= 0) & (dist < SWA_WINDOW)
    return np.where(in_window, bucket, -1).astype(np.int32)


def _swa_kernel(table_ref, sinks_ref, h_ref, qnw_ref, kvnw_ref, wq_ref, wkv_ref, bmap_ref, out_ref,
                bias_scr, kprev_scr, vprev_scr):
    n = pl.program_id(1)
    blk = SWA_BLOCK
    hd = SWA_HEAD_DIM
    kvd = SWA_KV_HEADS * hd

    @pl.when(n == 0)
    def _():
        kprev_scr[...] = jnp.zeros_like(kprev_scr)
        vprev_scr[...] = jnp.zeros_like(vprev_scr)

    @pl.when((pl.program_id(0) == 0) & (n == 0))
    def _():
        bmap = bmap_ref[...]
        for hq in range(SWA_Q_HEADS):
            bias_scr[0, hq] = jnp.where(bmap < 0, NEG_INF, 0.0)

        def bucket_body(b, carry):
            hit = bmap == b
            for hq in range(SWA_Q_HEADS):
                bias_scr[0, hq] = jnp.where(hit, table_ref[b, hq], bias_scr[0, hq])
            return carry

        lax.fori_loop(0, REL_BUCKETS, bucket_body, 0)
        key_col = lax.broadcasted_iota(jnp.int32, (blk, 2 * blk), 1)
        for hq in range(SWA_Q_HEADS):
            bias_scr[1, hq] = jnp.where(key_col >= blk, bias_scr[0, hq], NEG_INF)
        for hq in range(SWA_Q_HEADS):
            for tbl in range(2):
                bias_scr[tbl, hq] = jnp.where(key_col == 0, sinks_ref[hq], bias_scr[tbl, hq])

    y = _rms_scale(h_ref[...])
    q_all = (jnp.dot((y * qnw_ref[...]).astype(BF16), wq_ref[...], preferred_element_type=F32)
             * (hd ** -0.5)).astype(BF16)
    kv_all = jnp.dot((y * kvnw_ref[...]).astype(BF16), wkv_ref[...],
                     preferred_element_type=F32).astype(BF16)

    lane = lax.broadcasted_iota(jnp.int32, (1, LANES), 1)
    lo = lane < hd
    ones_blk = jnp.ones((2 * blk, LANES), BF16)
    not_row0 = lax.broadcasted_iota(jnp.int32, (blk, 1), 0) > 0
    kv_heads = range(SWA_KV_HEADS)
    subs = range(h_ref.shape[0] // blk)
    items = [(sb, h) for sb in subs for h in kv_heads]
    d_row = lax.broadcasted_iota(jnp.int32, (LANES, LANES), 0)
    d_col = lax.broadcasted_iota(jnp.int32, (LANES, LANES), 1) % hd
    k_blk, v_blk = {}, {}
    for h in kv_heads:
        grp = (h * hd) // LANES
        dup = (d_row == d_col + (h * hd) % LANES).astype(BF16)
        k_all = jnp.dot(kv_all[:, grp * LANES:(grp + 1) * LANES], dup,
                        preferred_element_type=F32).astype(BF16)
        v_all = jnp.dot(kv_all[:, kvd + grp * LANES:kvd + (grp + 1) * LANES], dup,
                        preferred_element_type=F32).astype(BF16)
        k_blk[-1, h] = kprev_scr[h]
        v_blk[-1, h] = vprev_scr[h]
        for sb in subs:
            k_blk[sb, h] = k_all[sb * blk:(sb + 1) * blk]
            v_blk[sb, h] = v_all[sb * blk:(sb + 1) * blk]
        for sb in subs:
            if sb + 1 < len(subs):
                k_blk[sb, h, "prev"] = jnp.where(not_row0, k_blk[sb, h], jnp.zeros_like(k_blk[sb, h]))
                v_blk[sb, h, "prev"] = jnp.where(not_row0, v_blk[sb, h], jnp.zeros_like(v_blk[sb, h]))
        last = len(subs) - 1
        kprev_scr[h] = jnp.where(not_row0, k_blk[last, h], jnp.zeros_like(k_blk[last, h]))
        vprev_scr[h] = jnp.where(not_row0, v_blk[last, h], jnp.zeros_like(v_blk[last, h]))

    def prev_of(blocks, sb, h):
        return blocks[-1, h] if sb == 0 else blocks[sb - 1, h, "prev"]

    k_dup = {(sb, h): jnp.concatenate([prev_of(k_blk, sb, h), k_blk[sb, h]], axis=0) for sb, h in items}
    v_aug = {(sb, h): jnp.concatenate(
        [jnp.concatenate([prev_of(v_blk, sb, h), v_blk[sb, h]], axis=0), ones_blk], axis=1)
        for sb, h in items}
    q_stack = {}
    for sb, h in items:
        q_rows = []
        for gidx in range(SWA_GROUP):
            hq = h * SWA_GROUP + gidx
            qg = q_all[sb * blk:(sb + 1) * blk, (hq * hd) // LANES * LANES:((hq * hd) // LANES + 1) * LANES]
            sel = lo if (hq * hd) % LANES == 0 else ~lo
            q_rows.append(jnp.where(sel, qg, jnp.zeros_like(qg)))
        q_stack[sb, h] = jnp.concatenate(q_rows, axis=0)
    s = {i: _dot_nt(q_stack[i], k_dup[i]) for i in items}
    first = [(n == 0).astype(jnp.int32) if sb == 0 else 0 for sb in subs]
    sbias = {(sb, h): s[sb, h] + bias_scr[first[sb], pl.ds(h * SWA_GROUP, SWA_GROUP)].reshape(
        SWA_GROUP * blk, 2 * blk) for sb, h in items}
    mx = {i: jnp.max(sbias[i], axis=-1, keepdims=True) for i in items}
    p = {i: jnp.exp(sbias[i] - mx[i]).astype(BF16) for i in items}
    oa = {i: jnp.dot(p[i], v_aug[i], preferred_element_type=F32) for i in items}
    outs = {i: oa[i][:, :LANES] / oa[i][:, LANES:] for i in items}
    for sb, h in items:
        for pair in range(SWA_GROUP // 2):
            hq0 = h * SWA_GROUP + 2 * pair
            col = (hq0 * hd) // LANES * LANES
            even = outs[sb, h][(2 * pair) * blk:(2 * pair + 1) * blk]
            odd = outs[sb, h][(2 * pair + 1) * blk:(2 * pair + 2) * blk]
            out_ref[sb * blk:(sb + 1) * blk, col:col + LANES] = jnp.where(lo, even, odd).astype(
                out_ref.dtype)


def _swa(h2, q_norm_w, kv_norm_w, w_q, w_kv, rel_table, sinks, bsz, seq):
    m, k = h2.shape
    nq = w_q.shape[1]
    nb = seq // SWA_STEP
    bmap = jnp.asarray(_t5_bucket_map())
    grid_spec = pltpu.PrefetchScalarGridSpec(
        num_scalar_prefetch=0,
        grid=(bsz, nb),
        in_specs=[
            pl.BlockSpec(memory_space=pltpu.SMEM),
            pl.BlockSpec(memory_space=pltpu.SMEM),
            pl.BlockSpec((SWA_STEP, k), lambda b, n: (b * nb + n, 0)),
            _const_spec((1, k)),
            _const_spec((1, k)),
            _resident_spec(w_q.shape),
            _resident_spec(w_kv.shape),
            _const_spec((SWA_BLOCK, 2 * SWA_BLOCK)),
        ],
        out_specs=pl.BlockSpec((SWA_STEP, nq), lambda b, n: (b * nb + n, 0)),
        scratch_shapes=[pltpu.VMEM((2, SWA_Q_HEADS, SWA_BLOCK, 2 * SWA_BLOCK), F32),
                        pltpu.VMEM((SWA_KV_HEADS, SWA_BLOCK, LANES), BF16),
                        pltpu.VMEM((SWA_KV_HEADS, SWA_BLOCK, LANES), BF16)],
    )
    return pl.pallas_call(
        _swa_kernel,
        grid_spec=grid_spec,
        out_shape=jax.ShapeDtypeStruct((m, nq), BF16),
        compiler_params=_params("arbitrary", "arbitrary"),
        name="swa",
    )(rel_table, sinks, h2, q_norm_w, kv_norm_w, w_q, w_kv, bmap)


def kernel(x, a_norm_w, a_w_in, a_conv_w, a_a_log, a_dt_bias, a_out_norm_w, a_w_out,
           kv_norm_w, w_kv, b_norm_w, b_w_q, b_sinks, b_w_o, rel_bias_table,
           ffn_norm_w, ffn_w_up, ffn_conv_w, ffn_conv_b, ffn_w_down, final_norm_w):
    bsz, seq, d = x.shape
    n_a = a_w_in.shape[0]
    n_b = b_w_q.shape[0]
    depth = n_a + n_b
    h = x.reshape(bsz * seq, d)
    head_order = np.concatenate([np.arange(0, GDN_V_HEADS, 2), np.arange(1, GDN_V_HEADS, 2)])
    gate_half = jnp.where(jnp.arange(2 * D_FF) < D_FF, 0.5, 1.0).astype(F32)
    w_up_all = ffn_w_up.astype(BF16)
    w_down_all = ffn_w_down.astype(BF16)
    for layer in range(depth):
        if layer < n_a:
            i = layer
            w_in = a_w_in[i].astype(BF16)
            w_ba = w_in[:, GDN_MAIN_DIM:]
            w_at = w_ba[:, np.concatenate([head_order, GDN_V_HEADS + head_order])].T
            qkv, z, ba, bat = _in_proj(h, a_norm_w[i].reshape(1, d), w_in, w_ba, w_at,
                                       0.5 * a_conv_w[i], bsz, seq)
            o = _gdn_core(qkv, z, ba, bat, a_a_log[i], a_dt_bias[i], a_out_norm_w[i],
                          head_order, bsz, seq)
            w_mix = a_w_out[i].astype(BF16)
        else:
            j = layer - n_a
            assert n_b == 1, "shared-KV reuse across several attention layers is not implemented"
            o = _swa(h, b_norm_w[j].reshape(1, d), kv_norm_w.reshape(1, d),
                     b_w_q[j].astype(BF16), w_kv.astype(BF16), rel_bias_table, b_sinks[j], bsz, seq)
            w_mix = b_w_o[j].astype(BF16)
        last = layer == depth - 1
        h = _ffn(h, o, w_mix, ffn_norm_w[layer].reshape(1, d), w_up_all,
                 ffn_conv_w[layer] * gate_half, (ffn_conv_b[layer] * gate_half).reshape(1, 2 * D_FF),
                 w_down_all, layer, final_norm_w.reshape(1, d), bsz, seq, last)
    return h.reshape(bsz, seq, d)
```

```python
import functools
import math

import jax
import jax.numpy as jnp
import numpy as np
from jax import lax
from jax.experimental import pallas as pl
from jax.experimental.pallas import tpu as pltpu

F32 = jnp.float32
BF16 = jnp.bfloat16

EPS = 1e-6
NEG_INF = -1e30

D_MODEL = 1024
GDN_QK_HEADS = 8
GDN_V_HEADS = 16
GDN_HEAD_DIM = 128
GDN_QK_DIM = GDN_QK_HEADS * GDN_HEAD_DIM
GDN_V_DIM = GDN_V_HEADS * GDN_HEAD_DIM
GDN_CONV_DIM = 2 * GDN_QK_DIM + GDN_V_DIM
GDN_MAIN_DIM = GDN_CONV_DIM + GDN_V_DIM
GDN_CONV_WIDTH = 4
GDN_CHUNK = 64

SWA_Q_HEADS = 16
SWA_KV_HEADS = 4
SWA_GROUP = SWA_Q_HEADS // SWA_KV_HEADS
SWA_HEAD_DIM = 64
SWA_WINDOW = 128
SWA_BLOCK = 128
SWA_STEP = 512
REL_BUCKETS = 32
REL_MAX_DISTANCE = 128

D_FF = 2816
FFN_CONV_WIDTH = 3
FFN_CHUNK = 256
CONV_ROWS = 64
PREP_CHUNKS = 4

SUBLANES = 8
LANES = 128
VMEM_LIMIT_BYTES = 56 * 1024 * 1024

TM_PROJ = 256
TM_GDN = 256
TM_FFN = 512
FFN_SUB = 256


def _params(*sem):
    return pltpu.CompilerParams(dimension_semantics=sem, vmem_limit_bytes=VMEM_LIMIT_BYTES)


def _const_spec(shape):
    zeros = (0,) * len(shape)
    return pl.BlockSpec(shape, lambda *_: zeros)


def _resident_spec(shape):
    zeros = (0,) * len(shape)
    return pl.BlockSpec(shape, lambda *_: zeros, pipeline_mode=pl.Buffered(1))


def _layer_spec(stacked_shape, layer):
    index = (layer,) + (0,) * (len(stacked_shape) - 1)
    return pl.BlockSpec((None,) + tuple(stacked_shape[1:]), lambda *_: index,
                        pipeline_mode=pl.Buffered(1))


def _dot(a, b):
    return jnp.dot(a.astype(BF16), b.astype(BF16), preferred_element_type=F32)


def _dot_nt(a, b):
    return lax.dot_general(a.astype(BF16), b.astype(BF16), (((1,), (1,)), ((), ())),
                           preferred_element_type=F32)


def _rms_scale(x):
    return x * lax.rsqrt(jnp.mean(x * x, axis=-1, keepdims=True) + EPS)


def _silu(x):
    return _silu_of_half(0.5 * x)


def _silu_of_half(half):
    return half + half * jnp.tanh(half)


def _shift_rows(cur, prev8, j):
    if j == 0:
        return cur
    nrow, ncol = cur.shape
    tiles = jnp.concatenate([prev8, cur], axis=0).reshape(nrow // SUBLANES + 1, SUBLANES, ncol)
    rot = pltpu.roll(tiles, j, 1)
    row = lax.broadcasted_iota(jnp.int32, (1, SUBLANES, ncol), 1)
    return jnp.where(row < j, rot[:-1], rot[1:]).reshape(nrow, ncol)


def _split3(x):
    hi = x.astype(BF16)
    r1 = x - hi.astype(F32)
    mid = r1.astype(BF16)
    lo = (r1 - mid.astype(F32)).astype(BF16)
    return hi, mid, lo


def _in_proj_kernel(x_ref, nw_ref, w_hbm, wba_ref, wbat_ref, cw_ref,
                    qkv_ref, z_ref, ba_ref, bat_ref, carry_scr, wm_ref, w_sem):
    t = pl.program_id(1)

    @pl.when(t == 0)
    def _():
        carry_scr[...] = jnp.zeros_like(carry_scr)

    @pl.when((pl.program_id(0) == 0) & (t == 0))
    def _():
        def slab_copy(ci):
            return pltpu.make_async_copy(w_hbm.at[:, pl.ds(ci * wm_ref.shape[2], wm_ref.shape[2])],
                                         wm_ref.at[ci], w_sem.at[ci])

        for ci in range(wm_ref.shape[0]):
            slab_copy(ci).start()
        for ci in range(wm_ref.shape[0]):
            slab_copy(ci).wait()

    tm = x_ref.shape[0]
    hd = GDN_HEAD_DIM
    hn = (_rms_scale(x_ref[...]) * nw_ref[...]).astype(BF16)
    step = 2 * hd
    zevery = GDN_CONV_DIM // GDN_V_DIM
    for c0 in range(0, GDN_CONV_DIM, step):
        if (c0 // step) % zevery == 0:
            z0 = c0 // zevery
            z_ref[:, z0:z0 + step] = jnp.dot(
                hn, wm_ref[(GDN_CONV_DIM + z0) // step], preferred_element_type=F32)
        sl = slice(c0, c0 + step)
        cur = jnp.dot(hn, wm_ref[c0 // step], preferred_element_type=F32)
        prev8 = carry_scr[:, sl]
        carry_scr[:, sl] = cur[tm - SUBLANES:]
        for hc in range(c0, c0 + step, hd):
            lanes = slice(hc - c0, hc - c0 + hd)
            wts = [cw_ref[GDN_CONV_WIDTH - 1 - j:GDN_CONV_WIDTH - j, hc:hc + hd]
                   for j in range(GDN_CONV_WIDTH)]
            for r0 in range(0, tm, CONV_ROWS):
                piece = cur[r0:r0 + CONV_ROWS, lanes]
                before = prev8[:, lanes] if r0 == 0 else cur[r0 - SUBLANES:r0, lanes]
                yh = piece * wts[0]
                for j in range(1, GDN_CONV_WIDTH):
                    yh = yh + _shift_rows(piece, before, j) * wts[j]
                yh = _silu_of_half(yh)
                if hc < 2 * GDN_QK_DIM:
                    yh = yh * lax.rsqrt(jnp.sum(yh * yh, axis=-1, keepdims=True) + EPS)
                    if hc < GDN_QK_DIM:
                        yh = yh * (hd ** -0.5)
                qkv_ref[r0:r0 + CONV_ROWS, hc:hc + hd] = yh.astype(qkv_ref.dtype)
    ba_ref[...] = jnp.dot(hn, wba_ref[...], preferred_element_type=F32)
    bat_ref[...] = _dot_nt(wbat_ref[...], hn)


def _in_proj(x2, norm_w, w_in, w_ba, w_bat, conv_w, bsz, seq):
    m, k = x2.shape
    slab = 2 * GDN_HEAD_DIM
    nslab = GDN_MAIN_DIM // slab
    nb = w_ba.shape[1]
    nbt = w_bat.shape[0]
    tm = TM_PROJ
    nt = seq // tm

    def row_map(b, t):
        return (b * nt + t, 0)

    return pl.pallas_call(
        _in_proj_kernel,
        grid=(bsz, nt),
        in_specs=[
            pl.BlockSpec((tm, k), row_map),
            _const_spec((1, k)),
            pl.BlockSpec(memory_space=pl.ANY),
            _const_spec((k, nb)),
            _const_spec((nbt, k)),
            _const_spec((GDN_CONV_WIDTH, GDN_CONV_DIM)),
        ],
        out_specs=[
            pl.BlockSpec((tm, GDN_CONV_DIM), row_map),
            pl.BlockSpec((tm, GDN_V_DIM), row_map),
            pl.BlockSpec((tm, nb), row_map),
            pl.BlockSpec((nbt, tm), lambda b, t: (0, b * nt + t)),
        ],
        out_shape=[
            jax.ShapeDtypeStruct((m, GDN_CONV_DIM), BF16),
            jax.ShapeDtypeStruct((m, GDN_V_DIM), F32),
            jax.ShapeDtypeStruct((m, nb), F32),
            jax.ShapeDtypeStruct((nbt, m), F32),
        ],
        scratch_shapes=[pltpu.VMEM((SUBLANES, GDN_CONV_DIM), F32),
                        pltpu.VMEM((nslab, k, slab), BF16),
                        pltpu.SemaphoreType.DMA((nslab,))],
        compiler_params=_params("arbitrary", "arbitrary"),
        name="gdn_in_proj",
    )(x2, norm_w, w_in, w_ba, w_bat, conv_w)


def _gdn_core_kernel(qkv_ref, z_ref, ba_ref, bat_ref, alog_ref, dtb_ref,
                     alogt_ref, dtbt_ref, onw_ref, out_ref,
                     s_scr, gcx_scr, betap_scr, rows_scr, u_scr, lhs_scr, a_scr, kt_scr):
    t = pl.program_id(1)
    tm = qkv_ref.shape[0]
    c = GDN_CHUNK
    nchunk = tm // c
    hd = GDN_HEAD_DIM
    nh = GDN_V_HEADS
    npair = GDN_QK_HEADS
    heads = range(nh)
    pairs = range(npair)

    @pl.when(t == 0)
    def _():
        s_scr[...] = jnp.zeros_like(s_scr)

    beta = 1.0 / (1.0 + jnp.exp(-ba_ref[:, :nh]))
    a_lin = ba_ref[:, nh:2 * nh] + dtb_ref[...]
    g = -jnp.exp(alog_ref[...]) * (jnp.maximum(a_lin, 0.0) + jnp.log1p(jnp.exp(-jnp.abs(a_lin))))
    row = lax.broadcasted_iota(jnp.int32, (tm, tm), 0)
    col = lax.broadcasted_iota(jnp.int32, (tm, tm), 1)
    same = (row // c) == (col // c)
    lblk = (same & (col <= row)).astype(BF16)
    ublk = (same & (row <= col)).astype(BF16)
    gc = jnp.dot(jnp.concatenate([lblk] * 3, axis=1), jnp.concatenate(_split3(g), axis=0),
                 preferred_element_type=F32)
    e_row = lax.broadcasted_iota(jnp.int32, (3 * nh, npair * 2 * c), 0) % nh
    e_col = lax.broadcasted_iota(jnp.int32, (3 * nh, npair * 2 * c), 1)
    expand = (e_row == e_col // c).astype(BF16)
    betap_scr[...] = jnp.dot(jnp.concatenate(_split3(beta), axis=1), expand,
                             preferred_element_type=F32)
    e_row = lax.broadcasted_iota(jnp.int32, (3 * nh, nh * hd), 0) % nh
    e_col = lax.broadcasted_iota(jnp.int32, (3 * nh, nh * hd), 1)
    expand_hd = (e_row == e_col // hd).astype(BF16)
    gcx_scr[...] = jnp.dot(jnp.concatenate(_split3(gc), axis=1), expand_hd,
                           preferred_element_type=F32)

    beta_t = 1.0 / (1.0 + jnp.exp(-bat_ref[:nh, :]))
    a_lin_t = bat_ref[nh:, :] + dtbt_ref[...]
    g_t = -jnp.exp(alogt_ref[...]) * (jnp.maximum(a_lin_t, 0.0)
                                      + jnp.log1p(jnp.exp(-jnp.abs(a_lin_t))))
    gct = jnp.dot(jnp.concatenate(_split3(g_t), axis=1), jnp.concatenate([ublk] * 3, axis=0),
                  preferred_element_type=F32)
    begct = beta_t * jnp.exp(gct)
    for ci in range(nchunk):
        for slot, arr in enumerate((gct, beta_t, begct)):
            rows_scr[slot, ci] = jnp.concatenate([arr[:npair, ci * c:(ci + 1) * c],
                                                  arr[npair:, ci * c:(ci + 1) * c]], axis=1)

    ri = lax.broadcasted_iota(jnp.int32, (c, 2 * c), 0)
    cj = lax.broadcasted_iota(jnp.int32, (c, 2 * c), 1) % c
    left = lax.broadcasted_iota(jnp.int32, (1, 2 * c), 1) < c
    tril = cj <= ri
    strict = cj < ri
    eye = (ri == cj).astype(F32)
    onw = onw_ref[...]
    zero_hd = jnp.zeros((c, hd), BF16)

    def block_diag(xc):
        xb = xc.astype(BF16)
        zero = jnp.zeros_like(xb)
        return jnp.concatenate([jnp.where(left, xb, zero), jnp.where(left, zero, xb)], axis=0)

    def two_blocks(top, bottom):
        return jnp.concatenate([jnp.concatenate([top, zero_hd], axis=1),
                                jnp.concatenate([zero_hd, bottom], axis=1)], axis=0)

    def prepare(chunk_ids):
        items = [(n, j) for n in range(len(chunk_ids)) for j in pairs]
        rows = [pl.ds(pl.multiple_of(ci * c, c), c) for ci in chunk_ids]
        q = {(n, j): qkv_ref[rows[n], j * hd:(j + 1) * hd] for n, j in items}
        k = {(n, j): qkv_ref[rows[n], GDN_QK_DIM + j * hd:GDN_QK_DIM + (j + 1) * hd]
             for n, j in items}
        gq = {i: _dot_nt(jnp.concatenate([k[i], q[i]], axis=0), jnp.concatenate([k[i], k[i]], axis=0))
              for i in items}
        dec = {}
        for n, j in items:
            gc_pair = jnp.where(left, gcx_scr[rows[n], 2 * j * hd:2 * j * hd + 2 * c],
                                gcx_scr[rows[n], (2 * j + 1) * hd:(2 * j + 1) * hd + 2 * c])
            diff = gc_pair - rows_scr[0, chunk_ids[n], pl.ds(j, 1), :]
            dec[n, j] = jnp.where(tril, jnp.exp(jnp.where(tril, diff, 0.0)), 0.0)
        m = {(n, j): jnp.where(strict, betap_scr[rows[n], j * 2 * c:(j + 1) * 2 * c]
                               * gq[n, j][:c] * dec[n, j], 0.0) for n, j in items}
        for n, j in items:
            a_scr[chunk_ids[n], j] = (gq[n, j][c:] * dec[n, j]).astype(BF16)
        x = {i: eye - m[i] for i in items}
        p = {i: _dot(m[i], block_diag(m[i])) for i in items}
        levels = int(math.log2(c)) - 1
        for lvl in range(levels):
            if lvl + 1 < levels:
                xp = {i: _dot(jnp.concatenate([x[i], p[i]], axis=0), block_diag(p[i])) for i in items}
                x = {i: x[i] + xp[i][:c] for i in items}
                p = {i: xp[i][c:] for i in items}
            else:
                xp = {i: _dot(x[i], block_diag(p[i])) for i in items}
                x = {i: x[i] + xp[i] for i in items}
        u, w = {}, {}
        for n, j in items:
            ci = chunk_ids[n]
            t_u = x[n, j] * rows_scr[1, ci, pl.ds(j, 1), :]
            t_w = x[n, j] * rows_scr[2, ci, pl.ds(j, 1), :]
            v_a = qkv_ref[rows[n], 2 * GDN_QK_DIM + 2 * j * hd:2 * GDN_QK_DIM + (2 * j + 1) * hd]
            v_b = qkv_ref[rows[n], 2 * GDN_QK_DIM + (2 * j + 1) * hd:2 * GDN_QK_DIM + (2 * j + 2) * hd]
            u[n, j] = _dot(t_u, two_blocks(v_a, v_b))
            w[n, j] = _dot(t_w, two_blocks(k[n, j], k[n, j]))
        for n, j in items:
            ci = chunk_ids[n]
            for e in range(2):
                h = 2 * j + e
                u_scr[ci, h] = u[n, j][:, e * hd:(e + 1) * hd]
                lhs_scr[ci, h] = jnp.concatenate(
                    [w[n, j][:, e * hd:(e + 1) * hd].astype(BF16), q[n, j]], axis=0)
            if j % 2 == 0:
                k2 = qkv_ref[rows[n], GDN_QK_DIM + j * hd:GDN_QK_DIM + (j + 2) * hd]
                kt_scr[ci, j // 2] = jnp.concatenate(
                    [k2[:, :hd], k2[:, hd:]], axis=0).astype(F32).T.astype(BF16)

    def prepare_body(i, carry):
        prepare([i * PREP_CHUNKS + n for n in range(PREP_CHUNKS)])
        return carry

    lax.fori_loop(0, nchunk // PREP_CHUNKS, prepare_body, 0)

    def state_body(ci, carry):
        r0 = pl.multiple_of(ci * c, c)
        rows = pl.ds(r0, c)
        gcx = [gcx_scr[rows, h * hd:(h + 1) * hd] for h in heads]
        egc = [jnp.exp(gcx[h]) for h in heads]
        ekd = [jnp.exp(gcx[h][c - 1:c, :] - gcx[h]) for h in heads]
        s_old = [s_scr[j] for j in pairs]
        pq = [_dot(lhs_scr[ci, h], s_old[h // 2][:, (h % 2) * hd:(h % 2 + 1) * hd])
              for h in heads]
        vn = [u_scr[ci, h] - pq[h][:c] for h in heads]
        vnb = [vn[h].astype(BF16) for h in heads]
        avn = [_dot(a_scr[ci, j], two_blocks(vnb[2 * j], vnb[2 * j + 1])) for j in pairs]
        vs = [jnp.concatenate([vn[2 * j] * ekd[2 * j], vn[2 * j + 1] * ekd[2 * j + 1]],
                              axis=1).astype(BF16) for j in pairs]
        zero_vs = jnp.zeros_like(vs[0])
        ds = [_dot(kt_scr[ci, j // 2], jnp.concatenate(
            [vs[j], zero_vs] if j % 2 == 0 else [zero_vs, vs[j]], axis=0)) for j in pairs]
        for j in pairs:
            egl = jnp.exp(jnp.concatenate([gcx[2 * j][c - 1:c, :], gcx[2 * j + 1][c - 1:c, :]], axis=1))
            s_scr[j] = s_old[j] * egl + ds[j]
        for h in heads:
            o_h = (pq[h][c:] * egc[h]
                   + avn[h // 2][:, (h % 2) * hd:(h % 2 + 1) * hd])
            z_h = z_ref[rows, h * hd:(h + 1) * hd]
            out_ref[rows, h * hd:(h + 1) * hd] = (
                _rms_scale(o_h) * onw * _silu(z_h)).astype(out_ref.dtype)
        return carry

    lax.fori_loop(0, nchunk, state_body, 0)


def _gdn_core(qkv, z, ba, bat, a_log, dt_bias, out_norm_w, head_order, bsz, seq):
    m = qkv.shape[0]
    tm = TM_GDN
    nt = seq // tm
    nh = GDN_V_HEADS
    nb = ba.shape[1]
    nchunk = tm // GDN_CHUNK

    def cur_map(b, t):
        return (b * nt + t, 0)

    return pl.pallas_call(
        _gdn_core_kernel,
        grid=(bsz, nt),
        in_specs=[
            pl.BlockSpec((tm, GDN_CONV_DIM), cur_map),
            pl.BlockSpec((tm, GDN_V_DIM), cur_map),
            pl.BlockSpec((tm, nb), cur_map),
            pl.BlockSpec((2 * nh, tm), lambda b, t: (0, b * nt + t)),
            _const_spec((1, nh)),
            _const_spec((1, nh)),
            _const_spec((nh, 1)),
            _const_spec((nh, 1)),
            _const_spec((1, GDN_HEAD_DIM)),
        ],
        out_specs=pl.BlockSpec((tm, GDN_V_DIM), cur_map),
        out_shape=jax.ShapeDtypeStruct((m, GDN_V_DIM), BF16),
        scratch_shapes=[
            pltpu.VMEM((GDN_QK_HEADS, GDN_HEAD_DIM, 2 * GDN_HEAD_DIM), F32),
            pltpu.VMEM((tm, nh * GDN_HEAD_DIM), F32),
            pltpu.VMEM((tm, nh * GDN_CHUNK), F32),
            pltpu.VMEM((3, nchunk, GDN_QK_HEADS, 2 * GDN_CHUNK), F32),
            pltpu.VMEM((nchunk, nh, GDN_CHUNK, GDN_HEAD_DIM), F32),
            pltpu.VMEM((nchunk, nh, 2 * GDN_CHUNK, GDN_HEAD_DIM), BF16),
            pltpu.VMEM((nchunk, GDN_QK_HEADS, GDN_CHUNK, 2 * GDN_CHUNK), BF16),
            pltpu.VMEM((nchunk, GDN_QK_HEADS // 2, GDN_HEAD_DIM, 2 * GDN_CHUNK), BF16),
        ],
        compiler_params=_params("parallel", "arbitrary"),
        name="gdn_core",
    )(qkv, z, ba, bat, a_log.reshape(1, nh), dt_bias.reshape(1, nh),
      a_log[head_order].reshape(nh, 1), dt_bias[head_order].reshape(nh, 1),
      out_norm_w.reshape(1, GDN_HEAD_DIM))


def _gdn_layer_kernel(x_ref, nw_ref, w_hbm, wba_ref, wbat_ref, cw_ref, alog_ref, dtb_ref,
                      alogt_ref, dtbt_ref, onw_ref, out_ref,
                      carry_scr, wm_ref, w_sem, qkv_scr, z_scr, ba_scr, bat_scr, *core_scratch):
    _in_proj_kernel(x_ref, nw_ref, w_hbm, wba_ref, wbat_ref, cw_ref,
                    qkv_scr, z_scr, ba_scr, bat_scr, carry_scr, wm_ref, w_sem)
    _gdn_core_kernel(qkv_scr, z_scr, ba_scr, bat_scr, alog_ref, dtb_ref, alogt_ref, dtbt_ref,
                     onw_ref, out_ref, *core_scratch)


def _gdn_layer(x2, norm_w, w_in, w_ba, w_bat, conv_w, a_log, dt_bias, out_norm_w, head_order,
               bsz, seq):
    m, k = x2.shape
    tm = TM_GDN
    nt = seq // tm
    nh = GDN_V_HEADS
    nb = w_ba.shape[1]
    nbt = w_bat.shape[0]
    slab = 2 * GDN_HEAD_DIM
    nslab = GDN_MAIN_DIM // slab
    nchunk = tm // GDN_CHUNK

    def row_map(b, t):
        return (b * nt + t, 0)

    return pl.pallas_call(
        _gdn_layer_kernel,
        grid=(bsz, nt),
        in_specs=[
            pl.BlockSpec((tm, k), row_map),
            _const_spec((1, k)),
            pl.BlockSpec(memory_space=pl.ANY),
            _const_spec((k, nb)),
            _const_spec((nbt, k)),
            _const_spec((GDN_CONV_WIDTH, GDN_CONV_DIM)),
            _const_spec((1, nh)),
            _const_spec((1, nh)),
            _const_spec((nh, 1)),
            _const_spec((nh, 1)),
            _const_spec((1, GDN_HEAD_DIM)),
        ],
        out_specs=pl.BlockSpec((tm, GDN_V_DIM), row_map),
        out_shape=jax.ShapeDtypeStruct((m, GDN_V_DIM), BF16),
        scratch_shapes=[
            pltpu.VMEM((SUBLANES, GDN_CONV_DIM), F32),
            pltpu.VMEM((nslab, k, slab), BF16),
            pltpu.SemaphoreType.DMA((nslab,)),
            pltpu.VMEM((tm, GDN_CONV_DIM), BF16),
            pltpu.VMEM((tm, GDN_V_DIM), F32),
            pltpu.VMEM((tm, nb), F32),
            pltpu.VMEM((nbt, tm), F32),
            pltpu.VMEM((GDN_QK_HEADS, GDN_HEAD_DIM, 2 * GDN_HEAD_DIM), F32),
            pltpu.VMEM((tm, nh * GDN_HEAD_DIM), F32),
            pltpu.VMEM((tm, nh * GDN_CHUNK), F32),
            pltpu.VMEM((3, nchunk, GDN_QK_HEADS, 2 * GDN_CHUNK), F32),
            pltpu.VMEM((nchunk, nh, GDN_CHUNK, GDN_HEAD_DIM), F32),
            pltpu.VMEM((nchunk, nh, 2 * GDN_CHUNK, GDN_HEAD_DIM), BF16),
            pltpu.VMEM((nchunk, GDN_QK_HEADS, GDN_CHUNK, 2 * GDN_CHUNK), BF16),
            pltpu.VMEM((nchunk, GDN_QK_HEADS // 2, GDN_HEAD_DIM, 2 * GDN_CHUNK), BF16),
        ],
        compiler_params=_params("arbitrary", "arbitrary"),
        name="gdn_layer",
    )(x2, norm_w, w_in, w_ba, w_bat, conv_w, a_log.reshape(1, nh), dt_bias.reshape(1, nh),
      a_log[head_order].reshape(nh, 1), dt_bias[head_order].reshape(nh, 1),
      out_norm_w.reshape(1, GDN_HEAD_DIM))


def _ffn_kernel(res_ref, x_ref, wpre_ref, nw_ref, wup_ref, cw_ref, cb_ref, wdn_ref, fnw_ref, out_ref,
                carry_scr, act_scr, h_scr, *, final_norm):
    t = pl.program_id(1)

    @pl.when(t == 0)
    def _():
        carry_scr[...] = jnp.zeros_like(carry_scr)

    nsub = res_ref.shape[0] // FFN_SUB

    def up_phase(si):
        rows = slice(si * FFN_SUB, (si + 1) * FFN_SUB)
        h = res_ref[rows, :] + jnp.dot(x_ref[rows, :], wpre_ref[...], preferred_element_type=F32)
        h_scr[si] = h
        hn = (_rms_scale(h) * nw_ref[...]).astype(BF16)
        for c0 in range(0, D_FF, FFN_CHUNK):
            halves = []
            for off in (0, D_FF):
                sl = slice(off + c0, off + c0 + FFN_CHUNK)
                cur = jnp.dot(hn, wup_ref[:, sl], preferred_element_type=F32)
                prev8 = carry_scr[:, sl]
                carry_scr[:, sl] = cur[FFN_SUB - SUBLANES:]
                y = cur * cw_ref[FFN_CONV_WIDTH - 1:FFN_CONV_WIDTH, sl] + cb_ref[:, sl]
                for j in range(1, FFN_CONV_WIDTH):
                    y = y + (_shift_rows(cur, prev8, j)
                             * cw_ref[FFN_CONV_WIDTH - 1 - j:FFN_CONV_WIDTH - j, sl])
                halves.append(y)
            act_scr[si, :, c0:c0 + FFN_CHUNK] = (_silu_of_half(halves[0]) * halves[1]).astype(BF16)

    def down_phase(si):
        rows = slice(si * FFN_SUB, (si + 1) * FFN_SUB)
        act = act_scr[si]
        res = h_scr[si] + jnp.dot(act, wdn_ref[...], preferred_element_type=F32)
        if final_norm:
            res = _rms_scale(res) * fnw_ref[...]
        out_ref[rows, :] = res

    up_phase(0)
    for si in range(1, nsub):
        up_phase(si)
        down_phase(si - 1)
    down_phase(nsub - 1)


def _ffn(res, x2, w_pre, norm_w, w_up_all, conv_w, conv_b, w_down_all, layer, final_w, bsz, seq,
         final_norm):
    m, k = res.shape
    tm = TM_FFN
    nt = seq // tm

    def row_map(b, t):
        return (b * nt + t, 0)

    return pl.pallas_call(
        functools.partial(_ffn_kernel, final_norm=final_norm),
        grid=(bsz, nt),
        in_specs=[
            pl.BlockSpec((tm, k), row_map),
            pl.BlockSpec((tm, x2.shape[1]), row_map),
            _resident_spec(w_pre.shape),
            _const_spec((1, k)),
            _layer_spec(w_up_all.shape, layer),
            _const_spec((FFN_CONV_WIDTH, 2 * D_FF)),
            _const_spec((1, 2 * D_FF)),
            _layer_spec(w_down_all.shape, layer),
            _const_spec((1, k)),
        ],
        out_specs=pl.BlockSpec((tm, k), row_map),
        out_shape=jax.ShapeDtypeStruct((m, k), F32),
        scratch_shapes=[
            pltpu.VMEM((SUBLANES, 2 * D_FF), F32),
            pltpu.VMEM((tm // FFN_SUB, FFN_SUB, D_FF), BF16),
            pltpu.VMEM((tm // FFN_SUB, FFN_SUB, k), F32),
        ],
        compiler_params=_params("parallel", "arbitrary"),
        name="ffn_final" if final_norm else "ffn",
    )(res, x2, w_pre, norm_w, w_up_all, conv_w, conv_b, w_down_all, final_w)


def _t5_bucket_map():
    qi = np.arange(SWA_BLOCK)[:, None]
    ki = np.arange(2 * SWA_BLOCK)[None, :]
    dist = qi + SWA_BLOCK - ki
    n = np.maximum(dist, 0)
    max_exact = REL_BUCKETS // 2
    nf = np.maximum(n, 1).astype(np.float32)
    large = max_exact + (np.log(nf / np.float32(max_exact)).astype(np.float32)
                         / np.float32(math.log(REL_MAX_DISTANCE / max_exact))
                         * np.float32(REL_BUCKETS - max_exact)).astype(np.int32)
    large = np.minimum(large, REL_BUCKETS - 1)
    bucket = np.where(n < max_exact, n, large).astype(np.int32)
    in_window = (dist >= 0) & (dist < SWA_WINDOW)
    return np.where(in_window, bucket, -1).astype(np.int32)


def _swa_kernel(table_ref, sinks_ref, h_ref, qnw_ref, kvnw_ref, wq_ref, wkv_ref, bmap_ref, out_ref,
                bias_scr, kprev_scr, vprev_scr):
    n = pl.program_id(1)
    blk = SWA_BLOCK
    hd = SWA_HEAD_DIM
    kvd = SWA_KV_HEADS * hd

    @pl.when(n == 0)
    def _():
        kprev_scr[...] = jnp.zeros_like(kprev_scr)
        vprev_scr[...] = jnp.zeros_like(vprev_scr)

    @pl.when((pl.program_id(0) == 0) & (n == 0))
    def _():
        bmap = bmap_ref[...]
        for hq in range(SWA_Q_HEADS):
            bias_scr[0, hq] = jnp.where(bmap < 0, NEG_INF, 0.0)

        def bucket_body(b, carry):
            hit = bmap == b
            for hq in range(SWA_Q_HEADS):
                bias_scr[0, hq] = jnp.where(hit, table_ref[b, hq], bias_scr[0, hq])
            return carry

        lax.fori_loop(0, REL_BUCKETS, bucket_body, 0)
        key_col = lax.broadcasted_iota(jnp.int32, (blk, 2 * blk), 1)
        for hq in range(SWA_Q_HEADS):
            bias_scr[1, hq] = jnp.where(key_col >= blk, bias_scr[0, hq], NEG_INF)
        for hq in range(SWA_Q_HEADS):
            for tbl in range(2):
                bias_scr[tbl, hq] = jnp.where(key_col == 0, sinks_ref[hq], bias_scr[tbl, hq])

    y = _rms_scale(h_ref[...])
    q_all = (jnp.dot((y * qnw_ref[...]).astype(BF16), wq_ref[...], preferred_element_type=F32)
             * (hd ** -0.5)).astype(BF16)
    kv_all = jnp.dot((y * kvnw_ref[...]).astype(BF16), wkv_ref[...],
                     preferred_element_type=F32).astype(BF16)

    lane = lax.broadcasted_iota(jnp.int32, (1, LANES), 1)
    lo = lane < hd
    ones_blk = jnp.ones((2 * blk, LANES), BF16)
    not_row0 = lax.broadcasted_iota(jnp.int32, (blk, 1), 0) > 0
    kv_heads = range(SWA_KV_HEADS)
    subs = range(h_ref.shape[0] // blk)
    items = [(sb, h) for sb in subs for h in kv_heads]
    d_row = lax.broadcasted_iota(jnp.int32, (LANES, LANES), 0)
    d_col = lax.broadcasted_iota(jnp.int32, (LANES, LANES), 1) % hd
    k_blk, v_blk = {}, {}
    for h in kv_heads:
        grp = (h * hd) // LANES
        dup = (d_row == d_col + (h * hd) % LANES).astype(BF16)
        k_all = jnp.dot(kv_all[:, grp * LANES:(grp + 1) * LANES], dup,
                        preferred_element_type=F32).astype(BF16)
        v_all = jnp.dot(kv_all[:, kvd + grp * LANES:kvd + (grp + 1) * LANES], dup,
                        preferred_element_type=F32).astype(BF16)
        k_blk[-1, h] = kprev_scr[h]
        v_blk[-1, h] = vprev_scr[h]
        for sb in subs:
            k_blk[sb, h] = k_all[sb * blk:(sb + 1) * blk]
            v_blk[sb, h] = v_all[sb * blk:(sb + 1) * blk]
        for sb in subs:
            if sb + 1 < len(subs):
                k_blk[sb, h, "prev"] = jnp.where(not_row0, k_blk[sb, h], jnp.zeros_like(k_blk[sb, h]))
                v_blk[sb, h, "prev"] = jnp.where(not_row0, v_blk[sb, h], jnp.zeros_like(v_blk[sb, h]))
        last = len(subs) - 1
        kprev_scr[h] = jnp.where(not_row0, k_blk[last, h], jnp.zeros_like(k_blk[last, h]))
        vprev_scr[h] = jnp.where(not_row0, v_blk[last, h], jnp.zeros_like(v_blk[last, h]))

    def prev_of(blocks, sb, h):
        return blocks[-1, h] if sb == 0 else blocks[sb - 1, h, "prev"]

    k_dup = {(sb, h): jnp.concatenate([prev_of(k_blk, sb, h), k_blk[sb, h]], axis=0) for sb, h in items}
    v_aug = {(sb, h): jnp.concatenate(
        [jnp.concatenate([prev_of(v_blk, sb, h), v_blk[sb, h]], axis=0), ones_blk], axis=1)
        for sb, h in items}
    q_stack = {}
    for sb, h in items:
        q_rows = []
        for gidx in range(SWA_GROUP):
            hq = h * SWA_GROUP + gidx
            qg = q_all[sb * blk:(sb + 1) * blk, (hq * hd) // LANES * LANES:((hq * hd) // LANES + 1) * LANES]
            sel = lo if (hq * hd) % LANES == 0 else ~lo
            q_rows.append(jnp.where(sel, qg, jnp.zeros_like(qg)))
        q_stack[sb, h] = jnp.concatenate(q_rows, axis=0)
    s = {i: _dot_nt(q_stack[i], k_dup[i]) for i in items}
    first = [(n == 0).astype(jnp.int32) if sb == 0 else 0 for sb in subs]
    sbias = {(sb, h): s[sb, h] + bias_scr[first[sb], pl.ds(h * SWA_GROUP, SWA_GROUP)].reshape(
        SWA_GROUP * blk, 2 * blk) for sb, h in items}
    mx = {i: jnp.max(sbias[i], axis=-1, keepdims=True) for i in items}
    p = {i: jnp.exp(sbias[i] - mx[i]).astype(BF16) for i in items}
    oa = {i: jnp.dot(p[i], v_aug[i], preferred_element_type=F32) for i in items}
    outs = {i: oa[i][:, :LANES] / oa[i][:, LANES:] for i in items}
    for sb, h in items:
        for pair in range(SWA_GROUP // 2):
            hq0 = h * SWA_GROUP + 2 * pair
            col = (hq0 * hd) // LANES * LANES
            even = outs[sb, h][(2 * pair) * blk:(2 * pair + 1) * blk]
            odd = outs[sb, h][(2 * pair + 1) * blk:(2 * pair + 2) * blk]
            out_ref[sb * blk:(sb + 1) * blk, col:col + LANES] = jnp.where(lo, even, odd).astype(
                out_ref.dtype)


def _swa(h2, q_norm_w, kv_norm_w, w_q, w_kv, rel_table, sinks, bsz, seq):
    m, k = h2.shape
    nq = w_q.shape[1]
    nb = seq // SWA_STEP
    bmap = jnp.asarray(_t5_bucket_map())
    grid_spec = pltpu.PrefetchScalarGridSpec(
        num_scalar_prefetch=0,
        grid=(bsz, nb),
        in_specs=[
            pl.BlockSpec(memory_space=pltpu.SMEM),
            pl.BlockSpec(memory_space=pltpu.SMEM),
            pl.BlockSpec((SWA_STEP, k), lambda b, n: (b * nb + n, 0)),
            _const_spec((1, k)),
            _const_spec((1, k)),
            _resident_spec(w_q.shape),
            _resident_spec(w_kv.shape),
            _const_spec((SWA_BLOCK, 2 * SWA_BLOCK)),
        ],
        out_specs=pl.BlockSpec((SWA_STEP, nq), lambda b, n: (b * nb + n, 0)),
        scratch_shapes=[pltpu.VMEM((2, SWA_Q_HEADS, SWA_BLOCK, 2 * SWA_BLOCK), F32),
                        pltpu.VMEM((SWA_KV_HEADS, SWA_BLOCK, LANES), BF16),
                        pltpu.VMEM((SWA_KV_HEADS, SWA_BLOCK, LANES), BF16)],
    )
    return pl.pallas_call(
        _swa_kernel,
        grid_spec=grid_spec,
        out_shape=jax.ShapeDtypeStruct((m, nq), BF16),
        compiler_params=_params("arbitrary", "arbitrary"),
        name="swa",
    )(rel_table, sinks, h2, q_norm_w, kv_norm_w, w_q, w_kv, bmap)


def kernel(x, a_norm_w, a_w_in, a_conv_w, a_a_log, a_dt_bias, a_out_norm_w, a_w_out,
           kv_norm_w, w_kv, b_norm_w, b_w_q, b_sinks, b_w_o, rel_bias_table,
           ffn_norm_w, ffn_w_up, ffn_conv_w, ffn_conv_b, ffn_w_down, final_norm_w):
    bsz, seq, d = x.shape
    n_a = a_w_in.shape[0]
    n_b = b_w_q.shape[0]
    depth = n_a + n_b
    h = x.reshape(bsz * seq, d)
    head_order = np.concatenate([np.arange(0, GDN_V_HEADS, 2), np.arange(1, GDN_V_HEADS, 2)])
    gate_half = jnp.where(jnp.arange(2 * D_FF) < D_FF, 0.5, 1.0).astype(F32)
    w_up_all = ffn_w_up.astype(BF16)
    w_down_all = ffn_w_down.astype(BF16)
    for layer in range(depth):
        if layer < n_a:
            i = layer
            w_in = a_w_in[i].astype(BF16)
            w_ba = w_in[:, GDN_MAIN_DIM:]
            w_at = w_ba[:, np.concatenate([head_order, GDN_V_HEADS + head_order])].T
            o = _gdn_layer(h, a_norm_w[i].reshape(1, d), w_in, w_ba, w_at, 0.5 * a_conv_w[i],
                           a_a_log[i], a_dt_bias[i], a_out_norm_w[i], head_order, bsz, seq)
            w_mix = a_w_out[i].astype(BF16)
        else:
            j = layer - n_a
            assert n_b == 1, "shared-KV reuse across several attention layers is not implemented"
            o = _swa(h, b_norm_w[j].reshape(1, d), kv_norm_w.reshape(1, d),
                     b_w_q[j].astype(BF16), w_kv.astype(BF16), rel_bias_table, b_sinks[j], bsz, seq)
            w_mix = b_w_o[j].astype(BF16)
        last = layer == depth - 1
        h = _ffn(h, o, w_mix, ffn_norm_w[layer].reshape(1, d), w_up_all,
                 ffn_conv_w[layer] * gate_half, (ffn_conv_b[layer] * gate_half).reshape(1, 2 * D_FF),
                 w_down_all, layer, final_norm_w.reshape(1, d), bsz, seq, last)
    return h.reshape(bsz, seq, d)
```

```python
import functools
import math

import jax
import jax.numpy as jnp
import numpy as np
from jax import lax
from jax.experimental import pallas as pl
from jax.experimental.pallas import tpu as pltpu

F32 = jnp.float32
BF16 = jnp.bfloat16

EPS = 1e-6
NEG_INF = -1e30

D_MODEL = 1024
GDN_QK_HEADS = 8
GDN_V_HEADS = 16
GDN_HEAD_DIM = 128
GDN_QK_DIM = GDN_QK_HEADS * GDN_HEAD_DIM
GDN_V_DIM = GDN_V_HEADS * GDN_HEAD_DIM
GDN_CONV_DIM = 2 * GDN_QK_DIM + GDN_V_DIM
GDN_MAIN_DIM = GDN_CONV_DIM + GDN_V_DIM
GDN_CONV_WIDTH = 4
GDN_CHUNK = 64

SWA_Q_HEADS = 16
SWA_KV_HEADS = 4
SWA_GROUP = SWA_Q_HEADS // SWA_KV_HEADS
SWA_HEAD_DIM = 64
SWA_WINDOW = 128
SWA_BLOCK = 128
SWA_STEP = 1024
REL_BUCKETS = 32
REL_MAX_DISTANCE = 128

D_FF = 2816
FFN_CONV_WIDTH = 3
FFN_CHUNK = 256
PREP_CHUNKS = 4

SUBLANES = 8
LANES = 128
VMEM_LIMIT_BYTES = 56 * 1024 * 1024

TM_PROJ = 256
TM_GDN = 256
TM_FFN = 512
FFN_SUB = 256


def _params(*sem):
    return pltpu.CompilerParams(dimension_semantics=sem, vmem_limit_bytes=VMEM_LIMIT_BYTES)


def _const_spec(shape):
    zeros = (0,) * len(shape)
    return pl.BlockSpec(shape, lambda *_: zeros)


def _resident_spec(shape):
    zeros = (0,) * len(shape)
    return pl.BlockSpec(shape, lambda *_: zeros, pipeline_mode=pl.Buffered(1))


def _layer_spec(stacked_shape, layer):
    index = (layer,) + (0,) * (len(stacked_shape) - 1)
    return pl.BlockSpec((None,) + tuple(stacked_shape[1:]), lambda *_: index,
                        pipeline_mode=pl.Buffered(1))


def _dot(a, b):
    return jnp.dot(a.astype(BF16), b.astype(BF16), preferred_element_type=F32)


def _dot_nt(a, b):
    return lax.dot_general(a.astype(BF16), b.astype(BF16), (((1,), (1,)), ((), ())),
                           preferred_element_type=F32)


def _rms_scale(x):
    return x * lax.rsqrt(jnp.mean(x * x, axis=-1, keepdims=True) + EPS)


def _silu(x):
    return _silu_of_half(0.5 * x)


def _silu_of_half(half):
    return half + half * jnp.tanh(half)


def _shift_rows(cur, prev8, j):
    if j == 0:
        return cur
    nrow, ncol = cur.shape
    tiles = jnp.concatenate([prev8, cur], axis=0).reshape(nrow // SUBLANES + 1, SUBLANES, ncol)
    rot = pltpu.roll(tiles, j, 1)
    row = lax.broadcasted_iota(jnp.int32, (1, SUBLANES, ncol), 1)
    return jnp.where(row < j, rot[:-1], rot[1:]).reshape(nrow, ncol)


def _split3(x):
    hi = x.astype(BF16)
    r1 = x - hi.astype(F32)
    mid = r1.astype(BF16)
    lo = (r1 - mid.astype(F32)).astype(BF16)
    return hi, mid, lo


def _in_proj_kernel(x_ref, nw_ref, w_hbm, wba_ref, wbat_ref, cw_ref,
                    qkv_ref, z_ref, ba_ref, bat_ref, carry_scr, wm_ref, w_sem):
    t = pl.program_id(1)

    @pl.when(t == 0)
    def _():
        carry_scr[...] = jnp.zeros_like(carry_scr)

    @pl.when((pl.program_id(0) == 0) & (t == 0))
    def _():
        def slab_copy(ci):
            return pltpu.make_async_copy(w_hbm.at[:, pl.ds(ci * wm_ref.shape[2], wm_ref.shape[2])],
                                         wm_ref.at[ci], w_sem.at[ci])

        for ci in range(wm_ref.shape[0]):
            slab_copy(ci).start()
        for ci in range(wm_ref.shape[0]):
            slab_copy(ci).wait()

    tm = x_ref.shape[0]
    hd = GDN_HEAD_DIM
    hn = (_rms_scale(x_ref[...]) * nw_ref[...]).astype(BF16)
    step = 2 * hd
    zevery = GDN_CONV_DIM // GDN_V_DIM
    for c0 in range(0, GDN_CONV_DIM, step):
        if (c0 // step) % zevery == 0:
            z0 = c0 // zevery
            z_ref[:, z0:z0 + step] = jnp.dot(
                hn, wm_ref[(GDN_CONV_DIM + z0) // step], preferred_element_type=F32)
        sl = slice(c0, c0 + step)
        cur = jnp.dot(hn, wm_ref[c0 // step], preferred_element_type=F32)
        prev8 = carry_scr[:, sl]
        carry_scr[:, sl] = cur[tm - SUBLANES:]
        y = cur * cw_ref[GDN_CONV_WIDTH - 1:GDN_CONV_WIDTH, sl]
        for j in range(1, GDN_CONV_WIDTH):
            y = y + _shift_rows(cur, prev8, j) * cw_ref[GDN_CONV_WIDTH - 1 - j:GDN_CONV_WIDTH - j, sl]
        y = _silu_of_half(y)
        for hc in range(c0, c0 + step, hd):
            yh = y[:, hc - c0:hc - c0 + hd]
            if hc < 2 * GDN_QK_DIM:
                yh = yh * lax.rsqrt(jnp.sum(yh * yh, axis=-1, keepdims=True) + EPS)
                if hc < GDN_QK_DIM:
                    yh = yh * (hd ** -0.5)
            qkv_ref[:, hc:hc + hd] = yh.astype(qkv_ref.dtype)
    ba_ref[...] = jnp.dot(hn, wba_ref[...], preferred_element_type=F32)
    bat_ref[...] = _dot_nt(wbat_ref[...], hn)


def _in_proj(x2, norm_w, w_in, w_ba, w_bat, conv_w, bsz, seq):
    m, k = x2.shape
    slab = 2 * GDN_HEAD_DIM
    nslab = GDN_MAIN_DIM // slab
    nb = w_ba.shape[1]
    nbt = w_bat.shape[0]
    tm = TM_PROJ
    nt = seq // tm

    def row_map(b, t):
        return (b * nt + t, 0)

    return pl.pallas_call(
        _in_proj_kernel,
        grid=(bsz, nt),
        in_specs=[
            pl.BlockSpec((tm, k), row_map),
            _const_spec((1, k)),
            pl.BlockSpec(memory_space=pl.ANY),
            _const_spec((k, nb)),
            _const_spec((nbt, k)),
            _const_spec((GDN_CONV_WIDTH, GDN_CONV_DIM)),
        ],
        out_specs=[
            pl.BlockSpec((tm, GDN_CONV_DIM), row_map),
            pl.BlockSpec((tm, GDN_V_DIM), row_map),
            pl.BlockSpec((tm, nb), row_map),
            pl.BlockSpec((nbt, tm), lambda b, t: (0, b * nt + t)),
        ],
        out_shape=[
            jax.ShapeDtypeStruct((m, GDN_CONV_DIM), BF16),
            jax.ShapeDtypeStruct((m, GDN_V_DIM), F32),
            jax.ShapeDtypeStruct((m, nb), F32),
            jax.ShapeDtypeStruct((nbt, m), F32),
        ],
        scratch_shapes=[pltpu.VMEM((SUBLANES, GDN_CONV_DIM), F32),
                        pltpu.VMEM((nslab, k, slab), BF16),
                        pltpu.SemaphoreType.DMA((nslab,))],
        compiler_params=_params("arbitrary", "arbitrary"),
        name="gdn_in_proj",
    )(x2, norm_w, w_in, w_ba, w_bat, conv_w)


def _gdn_core_kernel(qkv_ref, z_ref, ba_ref, bat_ref, alog_ref, dtb_ref,
                     alogt_ref, dtbt_ref, onw_ref, out_ref,
                     s_scr, gcx_scr, betap_scr, rows_scr, u_scr, lhs_scr, a_scr, kt_scr):
    t = pl.program_id(1)
    tm = qkv_ref.shape[0]
    c = GDN_CHUNK
    nchunk = tm // c
    hd = GDN_HEAD_DIM
    nh = GDN_V_HEADS
    npair = GDN_QK_HEADS
    heads = range(nh)
    pairs = range(npair)

    @pl.when(t == 0)
    def _():
        s_scr[...] = jnp.zeros_like(s_scr)

    beta = 1.0 / (1.0 + jnp.exp(-ba_ref[:, :nh]))
    a_lin = ba_ref[:, nh:2 * nh] + dtb_ref[...]
    g = -jnp.exp(alog_ref[...]) * (jnp.maximum(a_lin, 0.0) + jnp.log1p(jnp.exp(-jnp.abs(a_lin))))
    row = lax.broadcasted_iota(jnp.int32, (tm, tm), 0)
    col = lax.broadcasted_iota(jnp.int32, (tm, tm), 1)
    same = (row // c) == (col // c)
    lblk = (same & (col <= row)).astype(BF16)
    ublk = (same & (row <= col)).astype(BF16)
    gc = jnp.dot(jnp.concatenate([lblk] * 3, axis=1), jnp.concatenate(_split3(g), axis=0),
                 preferred_element_type=F32)
    e_row = lax.broadcasted_iota(jnp.int32, (3 * nh, npair * 2 * c), 0) % nh
    e_col = lax.broadcasted_iota(jnp.int32, (3 * nh, npair * 2 * c), 1)
    expand = (e_row == e_col // c).astype(BF16)
    betap_scr[...] = jnp.dot(jnp.concatenate(_split3(beta), axis=1), expand,
                             preferred_element_type=F32)
    e_row = lax.broadcasted_iota(jnp.int32, (3 * nh, nh * hd), 0) % nh
    e_col = lax.broadcasted_iota(jnp.int32, (3 * nh, nh * hd), 1)
    expand_hd = (e_row == e_col // hd).astype(BF16)
    gcx_scr[...] = jnp.dot(jnp.concatenate(_split3(gc), axis=1), expand_hd,
                           preferred_element_type=F32)

    beta_t = 1.0 / (1.0 + jnp.exp(-bat_ref[:nh, :]))
    a_lin_t = bat_ref[nh:, :] + dtbt_ref[...]
    g_t = -jnp.exp(alogt_ref[...]) * (jnp.maximum(a_lin_t, 0.0)
                                      + jnp.log1p(jnp.exp(-jnp.abs(a_lin_t))))
    gct = jnp.dot(jnp.concatenate(_split3(g_t), axis=1), jnp.concatenate([ublk] * 3, axis=0),
                  preferred_element_type=F32)
    begct = beta_t * jnp.exp(gct)
    for ci in range(nchunk):
        for slot, arr in enumerate((gct, beta_t, begct)):
            rows_scr[slot, ci] = jnp.concatenate([arr[:npair, ci * c:(ci + 1) * c],
                                                  arr[npair:, ci * c:(ci + 1) * c]], axis=1)

    ri = lax.broadcasted_iota(jnp.int32, (c, 2 * c), 0)
    cj = lax.broadcasted_iota(jnp.int32, (c, 2 * c), 1) % c
    left = lax.broadcasted_iota(jnp.int32, (1, 2 * c), 1) < c
    tril = cj <= ri
    strict = cj < ri
    eye = (ri == cj).astype(F32)
    onw = onw_ref[...]
    zero_hd = jnp.zeros((c, hd), BF16)

    def block_diag(xc):
        xb = xc.astype(BF16)
        zero = jnp.zeros_like(xb)
        return jnp.concatenate([jnp.where(left, xb, zero), jnp.where(left, zero, xb)], axis=0)

    def two_blocks(top, bottom):
        return jnp.concatenate([jnp.concatenate([top, zero_hd], axis=1),
                                jnp.concatenate([zero_hd, bottom], axis=1)], axis=0)

    def prepare(chunk_ids):
        items = [(n, j) for n in range(len(chunk_ids)) for j in pairs]
        rows = [pl.ds(pl.multiple_of(ci * c, c), c) for ci in chunk_ids]
        q = {(n, j): qkv_ref[rows[n], j * hd:(j + 1) * hd] for n, j in items}
        k = {(n, j): qkv_ref[rows[n], GDN_QK_DIM + j * hd:GDN_QK_DIM + (j + 1) * hd]
             for n, j in items}
        gq = {i: _dot_nt(jnp.concatenate([k[i], q[i]], axis=0), jnp.concatenate([k[i], k[i]], axis=0))
              for i in items}
        dec = {}
        for n, j in items:
            gc_pair = jnp.where(left, gcx_scr[rows[n], 2 * j * hd:2 * j * hd + 2 * c],
                                gcx_scr[rows[n], (2 * j + 1) * hd:(2 * j + 1) * hd + 2 * c])
            diff = gc_pair - rows_scr[0, chunk_ids[n], pl.ds(j, 1), :]
            dec[n, j] = jnp.where(tril, jnp.exp(jnp.where(tril, diff, 0.0)), 0.0)
        m = {(n, j): jnp.where(strict, betap_scr[rows[n], j * 2 * c:(j + 1) * 2 * c]
                               * gq[n, j][:c] * dec[n, j], 0.0) for n, j in items}
        for n, j in items:
            a_scr[chunk_ids[n], j] = (gq[n, j][c:] * dec[n, j]).astype(BF16)
        x = {i: eye - m[i] for i in items}
        p = {i: _dot(m[i], block_diag(m[i])) for i in items}
        levels = int(math.log2(c)) - 1
        for lvl in range(levels):
            if lvl + 1 < levels:
                xp = {i: _dot(jnp.concatenate([x[i], p[i]], axis=0), block_diag(p[i])) for i in items}
                x = {i: x[i] + xp[i][:c] for i in items}
                p = {i: xp[i][c:] for i in items}
            else:
                xp = {i: _dot(x[i], block_diag(p[i])) for i in items}
                x = {i: x[i] + xp[i] for i in items}
        u, w = {}, {}
        for n, j in items:
            ci = chunk_ids[n]
            t_u = x[n, j] * rows_scr[1, ci, pl.ds(j, 1), :]
            t_w = x[n, j] * rows_scr[2, ci, pl.ds(j, 1), :]
            v_a = qkv_ref[rows[n], 2 * GDN_QK_DIM + 2 * j * hd:2 * GDN_QK_DIM + (2 * j + 1) * hd]
            v_b = qkv_ref[rows[n], 2 * GDN_QK_DIM + (2 * j + 1) * hd:2 * GDN_QK_DIM + (2 * j + 2) * hd]
            u[n, j] = _dot(t_u, two_blocks(v_a, v_b))
            w[n, j] = _dot(t_w, two_blocks(k[n, j], k[n, j]))
        for n, j in items:
            ci = chunk_ids[n]
            for e in range(2):
                h = 2 * j + e
                u_scr[ci, h] = u[n, j][:, e * hd:(e + 1) * hd]
                lhs_scr[ci, h] = jnp.concatenate(
                    [w[n, j][:, e * hd:(e + 1) * hd].astype(BF16), q[n, j]], axis=0)
            if j % 2 == 0:
                k2 = qkv_ref[rows[n], GDN_QK_DIM + j * hd:GDN_QK_DIM + (j + 2) * hd]
                kt_scr[ci, j // 2] = jnp.concatenate(
                    [k2[:, :hd], k2[:, hd:]], axis=0).astype(F32).T.astype(BF16)

    def prepare_body(i, carry):
        prepare([i * PREP_CHUNKS + n for n in range(PREP_CHUNKS)])
        return carry

    lax.fori_loop(0, nchunk // PREP_CHUNKS, prepare_body, 0)

    def state_body(ci, carry):
        r0 = pl.multiple_of(ci * c, c)
        rows = pl.ds(r0, c)
        gcx = [gcx_scr[rows, h * hd:(h + 1) * hd] for h in heads]
        egc = [jnp.exp(gcx[h]) for h in heads]
        ekd = [jnp.exp(gcx[h][c - 1:c, :] - gcx[h]) for h in heads]
        s_old = [s_scr[j] for j in pairs]
        pq = [_dot(lhs_scr[ci, h], s_old[h // 2][:, (h % 2) * hd:(h % 2 + 1) * hd])
              for h in heads]
        vn = [u_scr[ci, h] - pq[h][:c] for h in heads]
        vnb = [vn[h].astype(BF16) for h in heads]
        avn = [_dot(a_scr[ci, j], two_blocks(vnb[2 * j], vnb[2 * j + 1])) for j in pairs]
        vs = [jnp.concatenate([vn[2 * j] * ekd[2 * j], vn[2 * j + 1] * ekd[2 * j + 1]],
                              axis=1).astype(BF16) for j in pairs]
        zero_vs = jnp.zeros_like(vs[0])
        ds = [_dot(kt_scr[ci, j // 2], jnp.concatenate(
            [vs[j], zero_vs] if j % 2 == 0 else [zero_vs, vs[j]], axis=0)) for j in pairs]
        for j in pairs:
            egl = jnp.exp(jnp.concatenate([gcx[2 * j][c - 1:c, :], gcx[2 * j + 1][c - 1:c, :]], axis=1))
            s_scr[j] = s_old[j] * egl + ds[j]
        for h in heads:
            o_h = (pq[h][c:] * egc[h]
                   + avn[h // 2][:, (h % 2) * hd:(h % 2 + 1) * hd])
            z_h = z_ref[rows, h * hd:(h + 1) * hd]
            out_ref[rows, h * hd:(h + 1) * hd] = (
                _rms_scale(o_h) * onw * _silu(z_h)).astype(out_ref.dtype)
        return carry

    lax.fori_loop(0, nchunk, state_body, 0)


def _gdn_core(qkv, z, ba, bat, a_log, dt_bias, out_norm_w, head_order, bsz, seq):
    m = qkv.shape[0]
    tm = TM_GDN
    nt = seq // tm
    nh = GDN_V_HEADS
    nb = ba.shape[1]
    nchunk = tm // GDN_CHUNK

    def cur_map(b, t):
        return (b * nt + t, 0)

    return pl.pallas_call(
        _gdn_core_kernel,
        grid=(bsz, nt),
        in_specs=[
            pl.BlockSpec((tm, GDN_CONV_DIM), cur_map),
            pl.BlockSpec((tm, GDN_V_DIM), cur_map),
            pl.BlockSpec((tm, nb), cur_map),
            pl.BlockSpec((2 * nh, tm), lambda b, t: (0, b * nt + t)),
            _const_spec((1, nh)),
            _const_spec((1, nh)),
            _const_spec((nh, 1)),
            _const_spec((nh, 1)),
            _const_spec((1, GDN_HEAD_DIM)),
        ],
        out_specs=pl.BlockSpec((tm, GDN_V_DIM), cur_map),
        out_shape=jax.ShapeDtypeStruct((m, GDN_V_DIM), BF16),
        scratch_shapes=[
            pltpu.VMEM((GDN_QK_HEADS, GDN_HEAD_DIM, 2 * GDN_HEAD_DIM), F32),
            pltpu.VMEM((tm, nh * GDN_HEAD_DIM), F32),
            pltpu.VMEM((tm, nh * GDN_CHUNK), F32),
            pltpu.VMEM((3, nchunk, GDN_QK_HEADS, 2 * GDN_CHUNK), F32),
            pltpu.VMEM((nchunk, nh, GDN_CHUNK, GDN_HEAD_DIM), F32),
            pltpu.VMEM((nchunk, nh, 2 * GDN_CHUNK, GDN_HEAD_DIM), BF16),
            pltpu.VMEM((nchunk, GDN_QK_HEADS, GDN_CHUNK, 2 * GDN_CHUNK), BF16),
            pltpu.VMEM((nchunk, GDN_QK_HEADS // 2, GDN_HEAD_DIM, 2 * GDN_CHUNK), BF16),
        ],
        compiler_params=_params("parallel", "arbitrary"),
        name="gdn_core",
    )(qkv, z, ba, bat, a_log.reshape(1, nh), dt_bias.reshape(1, nh),
      a_log[head_order].reshape(nh, 1), dt_bias[head_order].reshape(nh, 1),
      out_norm_w.reshape(1, GDN_HEAD_DIM))


def _ffn_kernel(res_ref, x_ref, wpre_ref, nw_ref, wup_ref, cw_ref, cb_ref, wdn_ref, fnw_ref, out_ref,
                carry_scr, act_scr, h_scr, *, final_norm):
    t = pl.program_id(1)

    @pl.when(t == 0)
    def _():
        carry_scr[...] = jnp.zeros_like(carry_scr)

    nsub = res_ref.shape[0] // FFN_SUB

    def up_phase(si):
        rows = slice(si * FFN_SUB, (si + 1) * FFN_SUB)
        h = res_ref[rows, :] + jnp.dot(x_ref[rows, :], wpre_ref[...], preferred_element_type=F32)
        h_scr[si] = h
        hn = (_rms_scale(h) * nw_ref[...]).astype(BF16)
        for c0 in range(0, D_FF, FFN_CHUNK):
            halves = []
            for off in (0, D_FF):
                sl = slice(off + c0, off + c0 + FFN_CHUNK)
                cur = jnp.dot(hn, wup_ref[:, sl], preferred_element_type=F32)
                prev8 = carry_scr[:, sl]
                carry_scr[:, sl] = cur[FFN_SUB - SUBLANES:]
                y = cur * cw_ref[FFN_CONV_WIDTH - 1:FFN_CONV_WIDTH, sl] + cb_ref[:, sl]
                for j in range(1, FFN_CONV_WIDTH):
                    y = y + (_shift_rows(cur, prev8, j)
                             * cw_ref[FFN_CONV_WIDTH - 1 - j:FFN_CONV_WIDTH - j, sl])
                halves.append(y)
            act_scr[si, :, c0:c0 + FFN_CHUNK] = (_silu_of_half(halves[0]) * halves[1]).astype(BF16)

    def down_phase(si):
        rows = slice(si * FFN_SUB, (si + 1) * FFN_SUB)
        act = act_scr[si]
        res = h_scr[si] + jnp.dot(act, wdn_ref[...], preferred_element_type=F32)
        if final_norm:
            res = _rms_scale(res) * fnw_ref[...]
        out_ref[rows, :] = res

    up_phase(0)
    for si in range(1, nsub):
        up_phase(si)
        down_phase(si - 1)
    down_phase(nsub - 1)


def _ffn(res, x2, w_pre, norm_w, w_up_all, conv_w, conv_b, w_down_all, layer, final_w, bsz, seq,
         final_norm):
    m, k = res.shape
    tm = TM_FFN
    nt = seq // tm

    def row_map(b, t):
        return (b * nt + t, 0)

    return pl.pallas_call(
        functools.partial(_ffn_kernel, final_norm=final_norm),
        grid=(bsz, nt),
        in_specs=[
            pl.BlockSpec((tm, k), row_map),
            pl.BlockSpec((tm, x2.shape[1]), row_map),
            _resident_spec(w_pre.shape),
            _const_spec((1, k)),
            _layer_spec(w_up_all.shape, layer),
            _const_spec((FFN_CONV_WIDTH, 2 * D_FF)),
            _const_spec((1, 2 * D_FF)),
            _layer_spec(w_down_all.shape, layer),
            _const_spec((1, k)),
        ],
        out_specs=pl.BlockSpec((tm, k), row_map),
        out_shape=jax.ShapeDtypeStruct((m, k), F32),
        scratch_shapes=[
            pltpu.VMEM((SUBLANES, 2 * D_FF), F32),
            pltpu.VMEM((tm // FFN_SUB, FFN_SUB, D_FF), BF16),
            pltpu.VMEM((tm // FFN_SUB, FFN_SUB, k), F32),
        ],
        compiler_params=_params("parallel", "arbitrary"),
        name="ffn_final" if final_norm else "ffn",
    )(res, x2, w_pre, norm_w, w_up_all, conv_w, conv_b, w_down_all, final_w)


def _t5_bucket_map():
    qi = np.arange(SWA_BLOCK)[:, None]
    ki = np.arange(2 * SWA_BLOCK)[None, :]
    dist = qi + SWA_BLOCK - ki
    n = np.maximum(dist, 0)
    max_exact = REL_BUCKETS // 2
    nf = np.maximum(n, 1).astype(np.float32)
    large = max_exact + (np.log(nf / np.float32(max_exact)).astype(np.float32)
                         / np.float32(math.log(REL_MAX_DISTANCE / max_exact))
                         * np.float32(REL_BUCKETS - max_exact)).astype(np.int32)
    large = np.minimum(large, REL_BUCKETS - 1)
    bucket = np.where(n < max_exact, n, large).astype(np.int32)
    in_window = (dist >= 0) & (dist < SWA_WINDOW)
    return np.where(in_window, bucket, -1).astype(np.int32)


def _swa_kernel(table_ref, sinks_ref, h_ref, qnw_ref, kvnw_ref, wq_ref, wkv_ref, bmap_ref, out_ref,
                bias_scr, kprev_scr, vprev_scr):
    n = pl.program_id(1)
    blk = SWA_BLOCK
    hd = SWA_HEAD_DIM
    kvd = SWA_KV_HEADS * hd

    @pl.when(n == 0)
    def _():
        kprev_scr[...] = jnp.zeros_like(kprev_scr)
        vprev_scr[...] = jnp.zeros_like(vprev_scr)

    @pl.when((pl.program_id(0) == 0) & (n == 0))
    def _():
        bmap = bmap_ref[...]
        for hq in range(SWA_Q_HEADS):
            bias_scr[0, hq] = jnp.where(bmap < 0, NEG_INF, 0.0)

        def bucket_body(b, carry):
            hit = bmap == b
            for hq in range(SWA_Q_HEADS):
                bias_scr[0, hq] = jnp.where(hit, table_ref[b, hq], bias_scr[0, hq])
            return carry

        lax.fori_loop(0, REL_BUCKETS, bucket_body, 0)
        key_col = lax.broadcasted_iota(jnp.int32, (blk, 2 * blk), 1)
        for hq in range(SWA_Q_HEADS):
            bias_scr[1, hq] = jnp.where(key_col >= blk, bias_scr[0, hq], NEG_INF)
        for hq in range(SWA_Q_HEADS):
            for tbl in range(2):
                bias_scr[tbl, hq] = jnp.where(key_col == 0, sinks_ref[hq], bias_scr[tbl, hq])

    y = _rms_scale(h_ref[...])
    q_all = (jnp.dot((y * qnw_ref[...]).astype(BF16), wq_ref[...], preferred_element_type=F32)
             * (hd ** -0.5)).astype(BF16)
    kv_all = jnp.dot((y * kvnw_ref[...]).astype(BF16), wkv_ref[...],
                     preferred_element_type=F32).astype(BF16)

    lane = lax.broadcasted_iota(jnp.int32, (1, LANES), 1)
    lo = lane < hd
    ones_blk = jnp.ones((2 * blk, LANES), BF16)
    not_row0 = lax.broadcasted_iota(jnp.int32, (blk, 1), 0) > 0
    kv_heads = range(SWA_KV_HEADS)
    subs = range(h_ref.shape[0] // blk)
    items = [(sb, h) for sb in subs for h in kv_heads]
    d_row = lax.broadcasted_iota(jnp.int32, (LANES, LANES), 0)
    d_col = lax.broadcasted_iota(jnp.int32, (LANES, LANES), 1) % hd
    k_blk, v_blk = {}, {}
    for h in kv_heads:
        grp = (h * hd) // LANES
        dup = (d_row == d_col + (h * hd) % LANES).astype(BF16)
        k_all = jnp.dot(kv_all[:, grp * LANES:(grp + 1) * LANES], dup,
                        preferred_element_type=F32).astype(BF16)
        v_all = jnp.dot(kv_all[:, kvd + grp * LANES:kvd + (grp + 1) * LANES], dup,
                        preferred_element_type=F32).astype(BF16)
        k_blk[-1, h] = kprev_scr[h]
        v_blk[-1, h] = vprev_scr[h]
        for sb in subs:
            k_blk[sb, h] = k_all[sb * blk:(sb + 1) * blk]
            v_blk[sb, h] = v_all[sb * blk:(sb + 1) * blk]
        for sb in subs:
            if sb + 1 < len(subs):
                k_blk[sb, h, "prev"] = jnp.where(not_row0, k_blk[sb, h], jnp.zeros_like(k_blk[sb, h]))
                v_blk[sb, h, "prev"] = jnp.where(not_row0, v_blk[sb, h], jnp.zeros_like(v_blk[sb, h]))
        last = len(subs) - 1
        kprev_scr[h] = jnp.where(not_row0, k_blk[last, h], jnp.zeros_like(k_blk[last, h]))
        vprev_scr[h] = jnp.where(not_row0, v_blk[last, h], jnp.zeros_like(v_blk[last, h]))

    def prev_of(blocks, sb, h):
        return blocks[-1, h] if sb == 0 else blocks[sb - 1, h, "prev"]

    k_dup = {(sb, h): jnp.concatenate([prev_of(k_blk, sb, h), k_blk[sb, h]], axis=0) for sb, h in items}
    v_aug = {(sb, h): jnp.concatenate(
        [jnp.concatenate([prev_of(v_blk, sb, h), v_blk[sb, h]], axis=0), ones_blk], axis=1)
        for sb, h in items}
    q_stack = {}
    for sb, h in items:
        q_rows = []
        for gidx in range(SWA_GROUP):
            hq = h * SWA_GROUP + gidx
            qg = q_all[sb * blk:(sb + 1) * blk, (hq * hd) // LANES * LANES:((hq * hd) // LANES + 1) * LANES]
            sel = lo if (hq * hd) % LANES == 0 else ~lo
            q_rows.append(jnp.where(sel, qg, jnp.zeros_like(qg)))
        q_stack[sb, h] = jnp.concatenate(q_rows, axis=0)
    s = {i: _dot_nt(q_stack[i], k_dup[i]) for i in items}
    first = [(n == 0).astype(jnp.int32) if sb == 0 else 0 for sb in subs]
    sbias = {(sb, h): s[sb, h] + bias_scr[first[sb], pl.ds(h * SWA_GROUP, SWA_GROUP)].reshape(
        SWA_GROUP * blk, 2 * blk) for sb, h in items}
    mx = {i: jnp.max(sbias[i], axis=-1, keepdims=True) for i in items}
    p = {i: jnp.exp(sbias[i] - mx[i]).astype(BF16) for i in items}
    oa = {i: jnp.dot(p[i], v_aug[i], preferred_element_type=F32) for i in items}
    outs = {i: oa[i][:, :LANES] / oa[i][:, LANES:] for i in items}
    for sb, h in items:
        for pair in range(SWA_GROUP // 2):
            hq0 = h * SWA_GROUP + 2 * pair
            col = (hq0 * hd) // LANES * LANES
            even = outs[sb, h][(2 * pair) * blk:(2 * pair + 1) * blk]
            odd = outs[sb, h][(2 * pair + 1) * blk:(2 * pair + 2) * blk]
            out_ref[sb * blk:(sb + 1) * blk, col:col + LANES] = jnp.where(lo, even, odd).astype(
                out_ref.dtype)


def _swa(h2, q_norm_w, kv_norm_w, w_q, w_kv, rel_table, sinks, bsz, seq):
    m, k = h2.shape
    nq = w_q.shape[1]
    nb = seq // SWA_STEP
    bmap = jnp.asarray(_t5_bucket_map())
    grid_spec = pltpu.PrefetchScalarGridSpec(
        num_scalar_prefetch=0,
        grid=(bsz, nb),
        in_specs=[
            pl.BlockSpec(memory_space=pltpu.SMEM),
            pl.BlockSpec(memory_space=pltpu.SMEM),
            pl.BlockSpec((SWA_STEP, k), lambda b, n: (b * nb + n, 0)),
            _const_spec((1, k)),
            _const_spec((1, k)),
            _resident_spec(w_q.shape),
            _resident_spec(w_kv.shape),
            _const_spec((SWA_BLOCK, 2 * SWA_BLOCK)),
        ],
        out_specs=pl.BlockSpec((SWA_STEP, nq), lambda b, n: (b * nb + n, 0)),
        scratch_shapes=[pltpu.VMEM((2, SWA_Q_HEADS, SWA_BLOCK, 2 * SWA_BLOCK), F32),
                        pltpu.VMEM((SWA_KV_HEADS, SWA_BLOCK, LANES), BF16),
                        pltpu.VMEM((SWA_KV_HEADS, SWA_BLOCK, LANES), BF16)],
    )
    return pl.pallas_call(
        _swa_kernel,
        grid_spec=grid_spec,
        out_shape=jax.ShapeDtypeStruct((m, nq), BF16),
        compiler_params=_params("arbitrary", "arbitrary"),
        name="swa",
    )(rel_table, sinks, h2, q_norm_w, kv_norm_w, w_q, w_kv, bmap)


def kernel(x, a_norm_w, a_w_in, a_conv_w, a_a_log, a_dt_bias, a_out_norm_w, a_w_out,
           kv_norm_w, w_kv, b_norm_w, b_w_q, b_sinks, b_w_o, rel_bias_table,
           ffn_norm_w, ffn_w_up, ffn_conv_w, ffn_conv_b, ffn_w_down, final_norm_w):
    bsz, seq, d = x.shape
    n_a = a_w_in.shape[0]
    n_b = b_w_q.shape[0]
    depth = n_a + n_b
    h = x.reshape(bsz * seq, d)
    head_order = np.concatenate([np.arange(0, GDN_V_HEADS, 2), np.arange(1, GDN_V_HEADS, 2)])
    gate_half = jnp.where(jnp.arange(2 * D_FF) < D_FF, 0.5, 1.0).astype(F32)
    w_up_all = ffn_w_up.astype(BF16)
    w_down_all = ffn_w_down.astype(BF16)
    for layer in range(depth):
        if layer < n_a:
            i = layer
            w_in = a_w_in[i].astype(BF16)
            w_ba = w_in[:, GDN_MAIN_DIM:]
            w_at = w_ba[:, np.concatenate([head_order, GDN_V_HEADS + head_order])].T
            qkv, z, ba, bat = _in_proj(h, a_norm_w[i].reshape(1, d), w_in, w_ba, w_at,
                                       0.5 * a_conv_w[i], bsz, seq)
            o = _gdn_core(qkv, z, ba, bat, a_a_log[i], a_dt_bias[i], a_out_norm_w[i],
                          head_order, bsz, seq)
            w_mix = a_w_out[i].astype(BF16)
        else:
            j = layer - n_a
            assert n_b == 1, "shared-KV reuse across several attention layers is not implemented"
            o = _swa(h, b_norm_w[j].reshape(1, d), kv_norm_w.reshape(1, d),
                     b_w_q[j].astype(BF16), w_kv.astype(BF16), rel_bias_table, b_sinks[j], bsz, seq)
            w_mix = b_w_o[j].astype(BF16)
        last = layer == depth - 1
        h = _ffn(h, o, w_mix, ffn_norm_w[layer].reshape(1, d), w_up_all,
                 ffn_conv_w[layer] * gate_half, (ffn_conv_b[layer] * gate_half).reshape(1, 2 * D_FF),
                 w_down_all, layer, final_norm_w.reshape(1, d), bsz, seq, last)
    return h.reshape(bsz, seq, d)
```

```python
import functools
import math

import jax
import jax.numpy as jnp
import numpy as np
from jax import lax
from jax.experimental import pallas as pl
from jax.experimental.pallas import tpu as pltpu

F32 = jnp.float32
BF16 = jnp.bfloat16

EPS = 1e-6
NEG_INF = -1e30

D_MODEL = 1024
GDN_QK_HEADS = 8
GDN_V_HEADS = 16
GDN_HEAD_DIM = 128
GDN_QK_DIM = GDN_QK_HEADS * GDN_HEAD_DIM
GDN_V_DIM = GDN_V_HEADS * GDN_HEAD_DIM
GDN_CONV_DIM = 2 * GDN_QK_DIM + GDN_V_DIM
GDN_MAIN_DIM = GDN_CONV_DIM + GDN_V_DIM
GDN_CONV_WIDTH = 4
GDN_CHUNK = 64

SWA_Q_HEADS = 16
SWA_KV_HEADS = 4
SWA_GROUP = SWA_Q_HEADS // SWA_KV_HEADS
SWA_HEAD_DIM = 64
SWA_WINDOW = 128
SWA_BLOCK = 128
SWA_STEP = 1024
REL_BUCKETS = 32
REL_MAX_DISTANCE = 128

D_FF = 2816
FFN_CONV_WIDTH = 3
FFN_CHUNK = 256
PREP_CHUNKS = 4

SUBLANES = 8
LANES = 128
VMEM_LIMIT_BYTES = 56 * 1024 * 1024

TM_PROJ = 256
TM_GDN = 256
TM_FFN = 512
FFN_STAGE = 3
FFN_SUB = 256


def _params(*sem):
    return pltpu.CompilerParams(dimension_semantics=sem, vmem_limit_bytes=VMEM_LIMIT_BYTES)


def _const_spec(shape):
    zeros = (0,) * len(shape)
    return pl.BlockSpec(shape, lambda *_: zeros)


def _resident_spec(shape):
    zeros = (0,) * len(shape)
    return pl.BlockSpec(shape, lambda *_: zeros, pipeline_mode=pl.Buffered(1))


def _layer_spec(stacked_shape, layer):
    index = (layer,) + (0,) * (len(stacked_shape) - 1)
    return pl.BlockSpec((None,) + tuple(stacked_shape[1:]), lambda *_: index,
                        pipeline_mode=pl.Buffered(1))


def _dot(a, b):
    return jnp.dot(a.astype(BF16), b.astype(BF16), preferred_element_type=F32)


def _dot_nt(a, b):
    return lax.dot_general(a.astype(BF16), b.astype(BF16), (((1,), (1,)), ((), ())),
                           preferred_element_type=F32)


def _rms_scale(x):
    return x * lax.rsqrt(jnp.mean(x * x, axis=-1, keepdims=True) + EPS)


def _silu(x):
    return _silu_of_half(0.5 * x)


def _silu_of_half(half):
    return half + half * jnp.tanh(half)


def _shift_rows(cur, prev8, j):
    if j == 0:
        return cur
    nrow, ncol = cur.shape
    tiles = jnp.concatenate([prev8, cur], axis=0).reshape(nrow // SUBLANES + 1, SUBLANES, ncol)
    rot = pltpu.roll(tiles, j, 1)
    row = lax.broadcasted_iota(jnp.int32, (1, SUBLANES, ncol), 1)
    return jnp.where(row < j, rot[:-1], rot[1:]).reshape(nrow, ncol)


def _split3(x):
    hi = x.astype(BF16)
    r1 = x - hi.astype(F32)
    mid = r1.astype(BF16)
    lo = (r1 - mid.astype(F32)).astype(BF16)
    return hi, mid, lo


def _in_proj_kernel(x_ref, nw_ref, w_hbm, wba_ref, wbat_ref, cw_ref,
                    qkv_ref, z_ref, ba_ref, bat_ref, carry_scr, wm_ref, w_sem):
    t = pl.program_id(1)

    @pl.when(t == 0)
    def _():
        carry_scr[...] = jnp.zeros_like(carry_scr)

    @pl.when((pl.program_id(0) == 0) & (t == 0))
    def _():
        def slab_copy(ci):
            return pltpu.make_async_copy(w_hbm.at[:, pl.ds(ci * wm_ref.shape[2], wm_ref.shape[2])],
                                         wm_ref.at[ci], w_sem.at[ci])

        for ci in range(wm_ref.shape[0]):
            slab_copy(ci).start()
        for ci in range(wm_ref.shape[0]):
            slab_copy(ci).wait()

    tm = x_ref.shape[0]
    hd = GDN_HEAD_DIM
    hn = (_rms_scale(x_ref[...]) * nw_ref[...]).astype(BF16)
    step = 2 * hd
    zevery = GDN_CONV_DIM // GDN_V_DIM
    for c0 in range(0, GDN_CONV_DIM, step):
        if (c0 // step) % zevery == 0:
            z0 = c0 // zevery
            z_ref[:, z0:z0 + step] = jnp.dot(
                hn, wm_ref[(GDN_CONV_DIM + z0) // step], preferred_element_type=F32)
        sl = slice(c0, c0 + step)
        cur = jnp.dot(hn, wm_ref[c0 // step], preferred_element_type=F32)
        prev8 = carry_scr[:, sl]
        carry_scr[:, sl] = cur[tm - SUBLANES:]
        y = cur * cw_ref[GDN_CONV_WIDTH - 1:GDN_CONV_WIDTH, sl]
        for j in range(1, GDN_CONV_WIDTH):
            y = y + _shift_rows(cur, prev8, j) * cw_ref[GDN_CONV_WIDTH - 1 - j:GDN_CONV_WIDTH - j, sl]
        y = _silu_of_half(y)
        for hc in range(c0, c0 + step, hd):
            yh = y[:, hc - c0:hc - c0 + hd]
            if hc < 2 * GDN_QK_DIM:
                yh = yh * lax.rsqrt(jnp.sum(yh * yh, axis=-1, keepdims=True) + EPS)
                if hc < GDN_QK_DIM:
                    yh = yh * (hd ** -0.5)
            qkv_ref[:, hc:hc + hd] = yh.astype(qkv_ref.dtype)
    ba_ref[...] = jnp.dot(hn, wba_ref[...], preferred_element_type=F32)
    bat_ref[...] = _dot_nt(wbat_ref[...], hn)


def _in_proj(x2, norm_w, w_in, w_ba, w_bat, conv_w, bsz, seq):
    m, k = x2.shape
    slab = 2 * GDN_HEAD_DIM
    nslab = GDN_MAIN_DIM // slab
    nb = w_ba.shape[1]
    nbt = w_bat.shape[0]
    tm = TM_PROJ
    nt = seq // tm

    def row_map(b, t):
        return (b * nt + t, 0)

    return pl.pallas_call(
        _in_proj_kernel,
        grid=(bsz, nt),
        in_specs=[
            pl.BlockSpec((tm, k), row_map),
            _const_spec((1, k)),
            pl.BlockSpec(memory_space=pl.ANY),
            _const_spec((k, nb)),
            _const_spec((nbt, k)),
            _const_spec((GDN_CONV_WIDTH, GDN_CONV_DIM)),
        ],
        out_specs=[
            pl.BlockSpec((tm, GDN_CONV_DIM), row_map),
            pl.BlockSpec((tm, GDN_V_DIM), row_map),
            pl.BlockSpec((tm, nb), row_map),
            pl.BlockSpec((nbt, tm), lambda b, t: (0, b * nt + t)),
        ],
        out_shape=[
            jax.ShapeDtypeStruct((m, GDN_CONV_DIM), BF16),
            jax.ShapeDtypeStruct((m, GDN_V_DIM), F32),
            jax.ShapeDtypeStruct((m, nb), F32),
            jax.ShapeDtypeStruct((nbt, m), F32),
        ],
        scratch_shapes=[pltpu.VMEM((SUBLANES, GDN_CONV_DIM), F32),
                        pltpu.VMEM((nslab, k, slab), BF16),
                        pltpu.SemaphoreType.DMA((nslab,))],
        compiler_params=_params("arbitrary", "arbitrary"),
        name="gdn_in_proj",
    )(x2, norm_w, w_in, w_ba, w_bat, conv_w)


def _gdn_core_kernel(qkv_ref, z_ref, ba_ref, bat_ref, alog_ref, dtb_ref,
                     alogt_ref, dtbt_ref, onw_ref, out_ref,
                     s_scr, gcx_scr, betap_scr, rows_scr, u_scr, lhs_scr, a_scr, kt_scr):
    t = pl.program_id(1)
    tm = qkv_ref.shape[0]
    c = GDN_CHUNK
    nchunk = tm // c
    hd = GDN_HEAD_DIM
    nh = GDN_V_HEADS
    npair = GDN_QK_HEADS
    heads = range(nh)
    pairs = range(npair)

    @pl.when(t == 0)
    def _():
        s_scr[...] = jnp.zeros_like(s_scr)

    beta = 1.0 / (1.0 + jnp.exp(-ba_ref[:, :nh]))
    a_lin = ba_ref[:, nh:2 * nh] + dtb_ref[...]
    g = -jnp.exp(alog_ref[...]) * (jnp.maximum(a_lin, 0.0) + jnp.log1p(jnp.exp(-jnp.abs(a_lin))))
    row = lax.broadcasted_iota(jnp.int32, (tm, tm), 0)
    col = lax.broadcasted_iota(jnp.int32, (tm, tm), 1)
    same = (row // c) == (col // c)
    lblk = (same & (col <= row)).astype(BF16)
    ublk = (same & (row <= col)).astype(BF16)
    gc = jnp.dot(jnp.concatenate([lblk] * 3, axis=1), jnp.concatenate(_split3(g), axis=0),
                 preferred_element_type=F32)
    e_row = lax.broadcasted_iota(jnp.int32, (3 * nh, npair * 2 * c), 0) % nh
    e_col = lax.broadcasted_iota(jnp.int32, (3 * nh, npair * 2 * c), 1)
    expand = (e_row == e_col // c).astype(BF16)
    betap_scr[...] = jnp.dot(jnp.concatenate(_split3(beta), axis=1), expand,
                             preferred_element_type=F32)
    e_row = lax.broadcasted_iota(jnp.int32, (3 * nh, nh * hd), 0) % nh
    e_col = lax.broadcasted_iota(jnp.int32, (3 * nh, nh * hd), 1)
    expand_hd = (e_row == e_col // hd).astype(BF16)
    gcx_scr[...] = jnp.dot(jnp.concatenate(_split3(gc), axis=1), expand_hd,
                           preferred_element_type=F32)

    beta_t = 1.0 / (1.0 + jnp.exp(-bat_ref[:nh, :]))
    a_lin_t = bat_ref[nh:, :] + dtbt_ref[...]
    g_t = -jnp.exp(alogt_ref[...]) * (jnp.maximum(a_lin_t, 0.0)
                                      + jnp.log1p(jnp.exp(-jnp.abs(a_lin_t))))
    gct = jnp.dot(jnp.concatenate(_split3(g_t), axis=1), jnp.concatenate([ublk] * 3, axis=0),
                  preferred_element_type=F32)
    begct = beta_t * jnp.exp(gct)
    for ci in range(nchunk):
        for slot, arr in enumerate((gct, beta_t, begct)):
            rows_scr[slot, ci] = jnp.concatenate([arr[:npair, ci * c:(ci + 1) * c],
                                                  arr[npair:, ci * c:(ci + 1) * c]], axis=1)

    ri = lax.broadcasted_iota(jnp.int32, (c, 2 * c), 0)
    cj = lax.broadcasted_iota(jnp.int32, (c, 2 * c), 1) % c
    left = lax.broadcasted_iota(jnp.int32, (1, 2 * c), 1) < c
    tril = cj <= ri
    strict = cj < ri
    eye = (ri == cj).astype(F32)
    onw = onw_ref[...]
    zero_hd = jnp.zeros((c, hd), BF16)

    def block_diag(xc):
        xb = xc.astype(BF16)
        zero = jnp.zeros_like(xb)
        return jnp.concatenate([jnp.where(left, xb, zero), jnp.where(left, zero, xb)], axis=0)

    def two_blocks(top, bottom):
        return jnp.concatenate([jnp.concatenate([top, zero_hd], axis=1),
                                jnp.concatenate([zero_hd, bottom], axis=1)], axis=0)

    def prepare(chunk_ids):
        items = [(n, j) for n in range(len(chunk_ids)) for j in pairs]
        rows = [pl.ds(pl.multiple_of(ci * c, c), c) for ci in chunk_ids]
        q = {(n, j): qkv_ref[rows[n], j * hd:(j + 1) * hd] for n, j in items}
        k = {(n, j): qkv_ref[rows[n], GDN_QK_DIM + j * hd:GDN_QK_DIM + (j + 1) * hd]
             for n, j in items}
        gq = {i: _dot_nt(jnp.concatenate([k[i], q[i]], axis=0), jnp.concatenate([k[i], k[i]], axis=0))
              for i in items}
        dec = {}
        for n, j in items:
            gc_pair = jnp.where(left, gcx_scr[rows[n], 2 * j * hd:2 * j * hd + 2 * c],
                                gcx_scr[rows[n], (2 * j + 1) * hd:(2 * j + 1) * hd + 2 * c])
            diff = gc_pair - rows_scr[0, chunk_ids[n], pl.ds(j, 1), :]
            dec[n, j] = jnp.where(tril, jnp.exp(jnp.where(tril, diff, 0.0)), 0.0)
        m = {(n, j): jnp.where(strict, betap_scr[rows[n], j * 2 * c:(j + 1) * 2 * c]
                               * gq[n, j][:c] * dec[n, j], 0.0) for n, j in items}
        for n, j in items:
            a_scr[chunk_ids[n], j] = (gq[n, j][c:] * dec[n, j]).astype(BF16)
        x = {i: eye - m[i] for i in items}
        p = {i: _dot(m[i], block_diag(m[i])) for i in items}
        levels = int(math.log2(c)) - 1
        for lvl in range(levels):
            if lvl + 1 < levels:
                xp = {i: _dot(jnp.concatenate([x[i], p[i]], axis=0), block_diag(p[i])) for i in items}
                x = {i: x[i] + xp[i][:c] for i in items}
                p = {i: xp[i][c:] for i in items}
            else:
                xp = {i: _dot(x[i], block_diag(p[i])) for i in items}
                x = {i: x[i] + xp[i] for i in items}
        u, w = {}, {}
        for n, j in items:
            ci = chunk_ids[n]
            t_u = x[n, j] * rows_scr[1, ci, pl.ds(j, 1), :]
            t_w = x[n, j] * rows_scr[2, ci, pl.ds(j, 1), :]
            v_a = qkv_ref[rows[n], 2 * GDN_QK_DIM + 2 * j * hd:2 * GDN_QK_DIM + (2 * j + 1) * hd]
            v_b = qkv_ref[rows[n], 2 * GDN_QK_DIM + (2 * j + 1) * hd:2 * GDN_QK_DIM + (2 * j + 2) * hd]
            u[n, j] = _dot(t_u, two_blocks(v_a, v_b))
            w[n, j] = _dot(t_w, two_blocks(k[n, j], k[n, j]))
        for n, j in items:
            ci = chunk_ids[n]
            for e in range(2):
                h = 2 * j + e
                u_scr[ci, h] = u[n, j][:, e * hd:(e + 1) * hd]
                lhs_scr[ci, h] = jnp.concatenate(
                    [w[n, j][:, e * hd:(e + 1) * hd].astype(BF16), q[n, j]], axis=0)
            if j % 2 == 0:
                k2 = qkv_ref[rows[n], GDN_QK_DIM + j * hd:GDN_QK_DIM + (j + 2) * hd]
                kt_scr[ci, j // 2] = jnp.concatenate(
                    [k2[:, :hd], k2[:, hd:]], axis=0).astype(F32).T.astype(BF16)

    def prepare_body(i, carry):
        prepare([i * PREP_CHUNKS + n for n in range(PREP_CHUNKS)])
        return carry

    lax.fori_loop(0, nchunk // PREP_CHUNKS, prepare_body, 0)

    def state_body(ci, carry):
        r0 = pl.multiple_of(ci * c, c)
        rows = pl.ds(r0, c)
        gcx = [gcx_scr[rows, h * hd:(h + 1) * hd] for h in heads]
        egc = [jnp.exp(gcx[h]) for h in heads]
        ekd = [jnp.exp(gcx[h][c - 1:c, :] - gcx[h]) for h in heads]
        s_old = [s_scr[j] for j in pairs]
        pq = [_dot(lhs_scr[ci, h], s_old[h // 2][:, (h % 2) * hd:(h % 2 + 1) * hd])
              for h in heads]
        vn = [u_scr[ci, h] - pq[h][:c] for h in heads]
        vnb = [vn[h].astype(BF16) for h in heads]
        avn = [_dot(a_scr[ci, j], two_blocks(vnb[2 * j], vnb[2 * j + 1])) for j in pairs]
        vs = [jnp.concatenate([vn[2 * j] * ekd[2 * j], vn[2 * j + 1] * ekd[2 * j + 1]],
                              axis=1).astype(BF16) for j in pairs]
        zero_vs = jnp.zeros_like(vs[0])
        ds = [_dot(kt_scr[ci, j // 2], jnp.concatenate(
            [vs[j], zero_vs] if j % 2 == 0 else [zero_vs, vs[j]], axis=0)) for j in pairs]
        for j in pairs:
            egl = jnp.exp(jnp.concatenate([gcx[2 * j][c - 1:c, :], gcx[2 * j + 1][c - 1:c, :]], axis=1))
            s_scr[j] = s_old[j] * egl + ds[j]
        for h in heads:
            o_h = (pq[h][c:] * egc[h]
                   + avn[h // 2][:, (h % 2) * hd:(h % 2 + 1) * hd])
            z_h = z_ref[rows, h * hd:(h + 1) * hd]
            out_ref[rows, h * hd:(h + 1) * hd] = (
                _rms_scale(o_h) * onw * _silu(z_h)).astype(out_ref.dtype)
        return carry

    lax.fori_loop(0, nchunk, state_body, 0)


def _gdn_core(qkv, z, ba, bat, a_log, dt_bias, out_norm_w, head_order, bsz, seq):
    m = qkv.shape[0]
    tm = TM_GDN
    nt = seq // tm
    nh = GDN_V_HEADS
    nb = ba.shape[1]
    nchunk = tm // GDN_CHUNK

    def cur_map(b, t):
        return (b * nt + t, 0)

    return pl.pallas_call(
        _gdn_core_kernel,
        grid=(bsz, nt),
        in_specs=[
            pl.BlockSpec((tm, GDN_CONV_DIM), cur_map),
            pl.BlockSpec((tm, GDN_V_DIM), cur_map),
            pl.BlockSpec((tm, nb), cur_map),
            pl.BlockSpec((2 * nh, tm), lambda b, t: (0, b * nt + t)),
            _const_spec((1, nh)),
            _const_spec((1, nh)),
            _const_spec((nh, 1)),
            _const_spec((nh, 1)),
            _const_spec((1, GDN_HEAD_DIM)),
        ],
        out_specs=pl.BlockSpec((tm, GDN_V_DIM), cur_map),
        out_shape=jax.ShapeDtypeStruct((m, GDN_V_DIM), BF16),
        scratch_shapes=[
            pltpu.VMEM((GDN_QK_HEADS, GDN_HEAD_DIM, 2 * GDN_HEAD_DIM), F32),
            pltpu.VMEM((tm, nh * GDN_HEAD_DIM), F32),
            pltpu.VMEM((tm, nh * GDN_CHUNK), F32),
            pltpu.VMEM((3, nchunk, GDN_QK_HEADS, 2 * GDN_CHUNK), F32),
            pltpu.VMEM((nchunk, nh, GDN_CHUNK, GDN_HEAD_DIM), F32),
            pltpu.VMEM((nchunk, nh, 2 * GDN_CHUNK, GDN_HEAD_DIM), BF16),
            pltpu.VMEM((nchunk, GDN_QK_HEADS, GDN_CHUNK, 2 * GDN_CHUNK), BF16),
            pltpu.VMEM((nchunk, GDN_QK_HEADS // 2, GDN_HEAD_DIM, 2 * GDN_CHUNK), BF16),
        ],
        compiler_params=_params("parallel", "arbitrary"),
        name="gdn_core",
    )(qkv, z, ba, bat, a_log.reshape(1, nh), dt_bias.reshape(1, nh),
      a_log[head_order].reshape(nh, 1), dt_bias[head_order].reshape(nh, 1),
      out_norm_w.reshape(1, GDN_HEAD_DIM))


def _stream_cast(n, src_of, store, stage, sem):
    depth = stage.shape[0]

    def chunk_copy(i):
        return pltpu.make_async_copy(src_of(i), stage.at[i % depth], sem.at[i % depth])

    for i in range(min(depth, n)):
        chunk_copy(i).start()
    for i in range(n):
        chunk_copy(i).wait()
        store(i, stage[i % depth].astype(BF16))
        if i + depth < n:
            chunk_copy(i + depth).start()


def _ffn_kernel(res_ref, x_ref, wpre_ref, nw_ref, wup_hbm, cw_ref, cb_ref, wdn_hbm, fnw_ref, out_ref,
                carry_scr, act_scr, h_scr, wup_ref, wdn_ref, stage_up, stage_dn, w_sem,
                *, final_norm, layer):
    t = pl.program_id(1)

    @pl.when(t == 0)
    def _():
        carry_scr[...] = jnp.zeros_like(carry_scr)

    @pl.when((pl.program_id(0) == 0) & (t == 0))
    def _():
        cols = stage_up.shape[2]
        rows = stage_dn.shape[1]

        def store_up(i, val):
            wup_ref[:, i * cols:(i + 1) * cols] = val

        def store_dn(i, val):
            wdn_ref[i * rows:(i + 1) * rows, :] = val

        _stream_cast(wup_ref.shape[1] // cols,
                     lambda i: wup_hbm.at[layer, :, pl.ds(i * cols, cols)], store_up,
                     stage_up, w_sem.at[0])
        _stream_cast(wdn_ref.shape[0] // rows,
                     lambda i: wdn_hbm.at[layer, pl.ds(i * rows, rows), :], store_dn,
                     stage_dn, w_sem.at[1])

    nsub = res_ref.shape[0] // FFN_SUB

    def up_phase(si):
        rows = slice(si * FFN_SUB, (si + 1) * FFN_SUB)
        h = res_ref[rows, :] + jnp.dot(x_ref[rows, :], wpre_ref[...], preferred_element_type=F32)
        h_scr[si] = h
        hn = (_rms_scale(h) * nw_ref[...]).astype(BF16)
        for c0 in range(0, D_FF, FFN_CHUNK):
            halves = []
            for off in (0, D_FF):
                sl = slice(off + c0, off + c0 + FFN_CHUNK)
                cur = jnp.dot(hn, wup_ref[:, sl], preferred_element_type=F32)
                prev8 = carry_scr[:, sl]
                carry_scr[:, sl] = cur[FFN_SUB - SUBLANES:]
                y = cur * cw_ref[FFN_CONV_WIDTH - 1:FFN_CONV_WIDTH, sl] + cb_ref[:, sl]
                for j in range(1, FFN_CONV_WIDTH):
                    y = y + (_shift_rows(cur, prev8, j)
                             * cw_ref[FFN_CONV_WIDTH - 1 - j:FFN_CONV_WIDTH - j, sl])
                halves.append(y)
            act_scr[si, :, c0:c0 + FFN_CHUNK] = (_silu_of_half(halves[0]) * halves[1]).astype(BF16)

    def down_phase(si):
        rows = slice(si * FFN_SUB, (si + 1) * FFN_SUB)
        act = act_scr[si]
        res = h_scr[si] + jnp.dot(act, wdn_ref[...], preferred_element_type=F32)
        if final_norm:
            res = _rms_scale(res) * fnw_ref[...]
        out_ref[rows, :] = res

    up_phase(0)
    for si in range(1, nsub):
        up_phase(si)
        down_phase(si - 1)
    down_phase(nsub - 1)


def _ffn(res, x2, w_pre, norm_w, w_up_all, conv_w, conv_b, w_down_all, layer, final_w, bsz, seq,
         final_norm):
    m, k = res.shape
    tm = TM_FFN
    nt = seq // tm

    def row_map(b, t):
        return (b * nt + t, 0)

    return pl.pallas_call(
        functools.partial(_ffn_kernel, final_norm=final_norm, layer=layer),
        grid=(bsz, nt),
        in_specs=[
            pl.BlockSpec((tm, k), row_map),
            pl.BlockSpec((tm, x2.shape[1]), row_map),
            _resident_spec(w_pre.shape),
            _const_spec((1, k)),
            pl.BlockSpec(memory_space=pl.ANY),
            _const_spec((FFN_CONV_WIDTH, 2 * D_FF)),
            _const_spec((1, 2 * D_FF)),
            pl.BlockSpec(memory_space=pl.ANY),
            _const_spec((1, k)),
        ],
        out_specs=pl.BlockSpec((tm, k), row_map),
        out_shape=jax.ShapeDtypeStruct((m, k), F32),
        scratch_shapes=[
            pltpu.VMEM((SUBLANES, 2 * D_FF), F32),
            pltpu.VMEM((tm // FFN_SUB, FFN_SUB, D_FF), BF16),
            pltpu.VMEM((tm // FFN_SUB, FFN_SUB, k), F32),
            pltpu.VMEM((k, 2 * D_FF), BF16),
            pltpu.VMEM((D_FF, k), BF16),
            pltpu.VMEM((FFN_STAGE, k, 2 * FFN_CHUNK), F32),
            pltpu.VMEM((FFN_STAGE, FFN_CHUNK, k), F32),
            pltpu.SemaphoreType.DMA((2, FFN_STAGE)),
        ],
        compiler_params=_params("arbitrary", "arbitrary"),
        name="ffn_final" if final_norm else "ffn",
    )(res, x2, w_pre, norm_w, w_up_all, conv_w, conv_b, w_down_all, final_w)


def _t5_bucket_map():
    qi = np.arange(SWA_BLOCK)[:, None]
    ki = np.arange(2 * SWA_BLOCK)[None, :]
    dist = qi + SWA_BLOCK - ki
    n = np.maximum(dist, 0)
    max_exact = REL_BUCKETS // 2
    nf = np.maximum(n, 1).astype(np.float32)
    large = max_exact + (np.log(nf / np.float32(max_exact)).astype(np.float32)
                         / np.float32(math.log(REL_MAX_DISTANCE / max_exact))
                         * np.float32(REL_BUCKETS - max_exact)).astype(np.int32)
    large = np.minimum(large, REL_BUCKETS - 1)
    bucket = np.where(n < max_exact, n, large).astype(np.int32)
    in_window = (dist >= 0) & (dist < SWA_WINDOW)
    return np.where(in_window, bucket, -1).astype(np.int32)


def _swa_kernel(table_ref, sinks_ref, h_ref, qnw_ref, kvnw_ref, wq_ref, wkv_ref, bmap_ref, out_ref,
                bias_scr, kprev_scr, vprev_scr):
    n = pl.program_id(1)
    blk = SWA_BLOCK
    hd = SWA_HEAD_DIM
    kvd = SWA_KV_HEADS * hd

    @pl.when(n == 0)
    def _():
        kprev_scr[...] = jnp.zeros_like(kprev_scr)
        vprev_scr[...] = jnp.zeros_like(vprev_scr)

    @pl.when((pl.program_id(0) == 0) & (n == 0))
    def _():
        bmap = bmap_ref[...]
        for hq in range(SWA_Q_HEADS):
            bias_scr[0, hq] = jnp.where(bmap < 0, NEG_INF, 0.0)

        def bucket_body(b, carry):
            hit = bmap == b
            for hq in range(SWA_Q_HEADS):
                bias_scr[0, hq] = jnp.where(hit, table_ref[b, hq], bias_scr[0, hq])
            return carry

        lax.fori_loop(0, REL_BUCKETS, bucket_body, 0)
        key_col = lax.broadcasted_iota(jnp.int32, (blk, 2 * blk), 1)
        for hq in range(SWA_Q_HEADS):
            bias_scr[1, hq] = jnp.where(key_col >= blk, bias_scr[0, hq], NEG_INF)
        for hq in range(SWA_Q_HEADS):
            for tbl in range(2):
                bias_scr[tbl, hq] = jnp.where(key_col == 0, sinks_ref[hq], bias_scr[tbl, hq])

    y = _rms_scale(h_ref[...])
    q_all = (jnp.dot((y * qnw_ref[...]).astype(BF16), wq_ref[...], preferred_element_type=F32)
             * (hd ** -0.5)).astype(BF16)
    kv_all = jnp.dot((y * kvnw_ref[...]).astype(BF16), wkv_ref[...],
                     preferred_element_type=F32).astype(BF16)

    lane = lax.broadcasted_iota(jnp.int32, (1, LANES), 1)
    lo = lane < hd
    ones_blk = jnp.ones((2 * blk, LANES), BF16)
    not_row0 = lax.broadcasted_iota(jnp.int32, (blk, 1), 0) > 0
    kv_heads = range(SWA_KV_HEADS)
    subs = range(h_ref.shape[0] // blk)
    items = [(sb, h) for sb in subs for h in kv_heads]
    d_row = lax.broadcasted_iota(jnp.int32, (LANES, LANES), 0)
    d_col = lax.broadcasted_iota(jnp.int32, (LANES, LANES), 1) % hd
    k_blk, v_blk = {}, {}
    for h in kv_heads:
        grp = (h * hd) // LANES
        dup = (d_row == d_col + (h * hd) % LANES).astype(BF16)
        k_all = jnp.dot(kv_all[:, grp * LANES:(grp + 1) * LANES], dup,
                        preferred_element_type=F32).astype(BF16)
        v_all = jnp.dot(kv_all[:, kvd + grp * LANES:kvd + (grp + 1) * LANES], dup,
                        preferred_element_type=F32).astype(BF16)
        k_blk[-1, h] = kprev_scr[h]
        v_blk[-1, h] = vprev_scr[h]
        for sb in subs:
            k_blk[sb, h] = k_all[sb * blk:(sb + 1) * blk]
            v_blk[sb, h] = v_all[sb * blk:(sb + 1) * blk]
        for sb in subs:
            if sb + 1 < len(subs):
                k_blk[sb, h, "prev"] = jnp.where(not_row0, k_blk[sb, h], jnp.zeros_like(k_blk[sb, h]))
                v_blk[sb, h, "prev"] = jnp.where(not_row0, v_blk[sb, h], jnp.zeros_like(v_blk[sb, h]))
        last = len(subs) - 1
        kprev_scr[h] = jnp.where(not_row0, k_blk[last, h], jnp.zeros_like(k_blk[last, h]))
        vprev_scr[h] = jnp.where(not_row0, v_blk[last, h], jnp.zeros_like(v_blk[last, h]))

    def prev_of(blocks, sb, h):
        return blocks[-1, h] if sb == 0 else blocks[sb - 1, h, "prev"]

    k_dup = {(sb, h): jnp.concatenate([prev_of(k_blk, sb, h), k_blk[sb, h]], axis=0) for sb, h in items}
    v_aug = {(sb, h): jnp.concatenate(
        [jnp.concatenate([prev_of(v_blk, sb, h), v_blk[sb, h]], axis=0), ones_blk], axis=1)
        for sb, h in items}
    q_stack = {}
    for sb, h in items:
        q_rows = []
        for gidx in range(SWA_GROUP):
            hq = h * SWA_GROUP + gidx
            qg = q_all[sb * blk:(sb + 1) * blk, (hq * hd) // LANES * LANES:((hq * hd) // LANES + 1) * LANES]
            sel = lo if (hq * hd) % LANES == 0 else ~lo
            q_rows.append(jnp.where(sel, qg, jnp.zeros_like(qg)))
        q_stack[sb, h] = jnp.concatenate(q_rows, axis=0)
    s = {i: _dot_nt(q_stack[i], k_dup[i]) for i in items}
    first = [(n == 0).astype(jnp.int32) if sb == 0 else 0 for sb in subs]
    sbias = {(sb, h): s[sb, h] + bias_scr[first[sb], pl.ds(h * SWA_GROUP, SWA_GROUP)].reshape(
        SWA_GROUP * blk, 2 * blk) for sb, h in items}
    mx = {i: jnp.max(sbias[i], axis=-1, keepdims=True) for i in items}
    p = {i: jnp.exp(sbias[i] - mx[i]).astype(BF16) for i in items}
    oa = {i: jnp.dot(p[i], v_aug[i], preferred_element_type=F32) for i in items}
    outs = {i: oa[i][:, :LANES] / oa[i][:, LANES:] for i in items}
    for sb, h in items:
        for pair in range(SWA_GROUP // 2):
            hq0 = h * SWA_GROUP + 2 * pair
            col = (hq0 * hd) // LANES * LANES
            even = outs[sb, h][(2 * pair) * blk:(2 * pair + 1) * blk]
            odd = outs[sb, h][(2 * pair + 1) * blk:(2 * pair + 2) * blk]
            out_ref[sb * blk:(sb + 1) * blk, col:col + LANES] = jnp.where(lo, even, odd).astype(
                out_ref.dtype)


def _swa(h2, q_norm_w, kv_norm_w, w_q, w_kv, rel_table, sinks, bsz, seq):
    m, k = h2.shape
    nq = w_q.shape[1]
    nb = seq // SWA_STEP
    bmap = jnp.asarray(_t5_bucket_map())
    grid_spec = pltpu.PrefetchScalarGridSpec(
        num_scalar_prefetch=0,
        grid=(bsz, nb),
        in_specs=[
            pl.BlockSpec(memory_space=pltpu.SMEM),
            pl.BlockSpec(memory_space=pltpu.SMEM),
            pl.BlockSpec((SWA_STEP, k), lambda b, n: (b * nb + n, 0)),
            _const_spec((1, k)),
            _const_spec((1, k)),
            _resident_spec(w_q.shape),
            _resident_spec(w_kv.shape),
            _const_spec((SWA_BLOCK, 2 * SWA_BLOCK)),
        ],
        out_specs=pl.BlockSpec((SWA_STEP, nq), lambda b, n: (b * nb + n, 0)),
        scratch_shapes=[pltpu.VMEM((2, SWA_Q_HEADS, SWA_BLOCK, 2 * SWA_BLOCK), F32),
                        pltpu.VMEM((SWA_KV_HEADS, SWA_BLOCK, LANES), BF16),
                        pltpu.VMEM((SWA_KV_HEADS, SWA_BLOCK, LANES), BF16)],
    )
    return pl.pallas_call(
        _swa_kernel,
        grid_spec=grid_spec,
        out_shape=jax.ShapeDtypeStruct((m, nq), BF16),
        compiler_params=_params("arbitrary", "arbitrary"),
        name="swa",
    )(rel_table, sinks, h2, q_norm_w, kv_norm_w, w_q, w_kv, bmap)


def kernel(x, a_norm_w, a_w_in, a_conv_w, a_a_log, a_dt_bias, a_out_norm_w, a_w_out,
           kv_norm_w, w_kv, b_norm_w, b_w_q, b_sinks, b_w_o, rel_bias_table,
           ffn_norm_w, ffn_w_up, ffn_conv_w, ffn_conv_b, ffn_w_down, final_norm_w):
    bsz, seq, d = x.shape
    n_a = a_w_in.shape[0]
    n_b = b_w_q.shape[0]
    depth = n_a + n_b
    h = x.reshape(bsz * seq, d)
    head_order = np.concatenate([np.arange(0, GDN_V_HEADS, 2), np.arange(1, GDN_V_HEADS, 2)])
    gate_half = jnp.where(jnp.arange(2 * D_FF) < D_FF, 0.5, 1.0).astype(F32)
    w_up_all = ffn_w_up
    w_down_all = ffn_w_down
    for layer in range(depth):
        if layer < n_a:
            i = layer
            w_in = a_w_in[i].astype(BF16)
            w_ba = w_in[:, GDN_MAIN_DIM:]
            w_at = w_ba[:, np.concatenate([head_order, GDN_V_HEADS + head_order])].T
            qkv, z, ba, bat = _in_proj(h, a_norm_w[i].reshape(1, d), w_in, w_ba, w_at,
                                       0.5 * a_conv_w[i], bsz, seq)
            o = _gdn_core(qkv, z, ba, bat, a_a_log[i], a_dt_bias[i], a_out_norm_w[i],
                          head_order, bsz, seq)
            w_mix = a_w_out[i].astype(BF16)
        else:
            j = layer - n_a
            assert n_b == 1, "shared-KV reuse across several attention layers is not implemented"
            o = _swa(h, b_norm_w[j].reshape(1, d), kv_norm_w.reshape(1, d),
                     b_w_q[j].astype(BF16), w_kv.astype(BF16), rel_bias_table, b_sinks[j], bsz, seq)
            w_mix = b_w_o[j].astype(BF16)
        last = layer == depth - 1
        h = _ffn(h, o, w_mix, ffn_norm_w[layer].reshape(1, d), w_up_all,
                 ffn_conv_w[layer] * gate_half, (ffn_conv_b[layer] * gate_half).reshape(1, 2 * D_FF),
                 w_down_all, layer, final_norm_w.reshape(1, d), bsz, seq, last)
    return h.reshape(bsz, seq, d)
```
